```python
import math
import jax, jax.numpy as jnp
from jax import lax
import numpy as np

D_MODEL = 1024
BATCH = 2
SEQ = 8192
DEPTH = 1

HEAD_DIM = 64
ROT_DIM = HEAD_DIM // 4
ROPE_THETA = 500000.0
BLOCK = 128
EPS = 1e-6
DIL_PAIRS = ((128, 1), (512, 4), (2048, 16))
A_SLOTS = 8
A_HEADS = A_SLOTS * len(DIL_PAIRS)
B_HEADS = 8
B_KV = 2
CMP_LEN = 32
CMP_STRIDE = 16
CMP_HIDDEN = 4 * HEAD_DIM
SEL_LEN = 64
SEL_TOP = 16
WIN_LEN = 512
FORCE = 1e4
D_FF = 4 * D_MODEL
A_QKV = 3 * A_HEADS * HEAD_DIM
B_Q = B_HEADS * HEAD_DIM
B_KV_COLS = 6 * B_KV * HEAD_DIM
B_GATE = 3 * B_HEADS
MERGE_GATE = 2 * D_MODEL
IN_COLS = A_QKV + B_Q + B_KV_COLS + B_GATE + MERGE_GATE

kernel_name = "hybrid_dilated_nsa_gated_block"


def rms_norm(x, g):
    xf = x.astype(jnp.float32)
    y = xf * lax.rsqrt(jnp.mean(xf * xf, axis=-1, keepdims=True) + EPS)
    return (y * g.astype(jnp.float32)).astype(x.dtype)


def rope_tables(seq):
    pos = jnp.arange(seq, dtype=jnp.float32)
    inv = ROPE_THETA ** (-jnp.arange(0, ROT_DIM, 2, dtype=jnp.float32) / ROT_DIM)
    ang = pos[:, None] * inv[None, :]
    return jnp.cos(ang), jnp.sin(ang)


def partial_rotary(x, cos, sin):
    half = ROT_DIM // 2
    c = cos[None, :, None, :].astype(x.dtype)
    s = sin[None, :, None, :].astype(x.dtype)
    x1 = x[..., :half]
    x2 = x[..., half:ROT_DIM]
    return jnp.concatenate([x1 * c - x2 * s, x2 * c + x1 * s, x[..., ROT_DIM:]], axis=-1)


def banded_attention(q, k, v, max_offset):
    n, L, H, hd = q.shape
    hk = k.shape[2]
    rep = H // hk
    n_prev = -(-max_offset // BLOCK)
    nb = -(-L // BLOCK)
    lp = nb * BLOCK
    pad_end = ((0, 0), (0, lp - L), (0, 0), (0, 0))
    q = jnp.pad(q, pad_end)
    front = ((0, 0), (n_prev * BLOCK, lp - L), (0, 0), (0, 0))
    kb = jnp.pad(k, front).reshape(n, nb + n_prev, BLOCK, hk, hd)
    vb = jnp.pad(v, front).reshape(n, nb + n_prev, BLOCK, hk, hd)
    kwin = jnp.concatenate([kb[:, o:o + nb] for o in range(n_prev + 1)], axis=2)
    vwin = jnp.concatenate([vb[:, o:o + nb] for o in range(n_prev + 1)], axis=2)
    qb = q.reshape(n, nb, BLOCK, hk, rep, hd)
    qpos = jnp.arange(nb)[:, None] * BLOCK + jnp.arange(BLOCK)[None, :]
    kpos = (jnp.arange(nb)[:, None] - n_prev) * BLOCK + jnp.arange((n_prev + 1) * BLOCK)[None, :]
    diff = qpos[:, :, None] - kpos[:, None, :]
    mask = (diff >= 0) & (diff <= max_offset) & (kpos[:, None, :] >= 0)
    s = jnp.einsum('bnqkrd,bnwkd->bnkrqw', qb, kwin,
                   preferred_element_type=jnp.float32) * (hd ** -0.5)
    s = jnp.where(mask[None, :, None, None], s, -1e30)
    m = jnp.max(s, axis=-1, keepdims=True)
    p = jnp.exp(s - m)
    den = jnp.sum(p, axis=-1, keepdims=True)
    o = jnp.einsum('bnkrqw,bnwkd->bnqkrd', (p / den).astype(v.dtype), vwin)
    lse = (m + jnp.log(den))[..., 0]
    o = o.reshape(n, lp, H, hd)[:, :L]
    lse = lse.transpose(0, 1, 4, 2, 3).reshape(n, lp, H)[:, :L]
    return o, lse


def dilated_attention(q, k, v):
    b, s_len, _, hd = q.shape
    outs, lses = [], []
    for g, (w, d) in enumerate(DIL_PAIRS):
        sl = slice(g * A_SLOTS, (g + 1) * A_SLOTS)

        def split(t):
            return t[:, :, sl].reshape(b, s_len // d, d, A_SLOTS, hd).transpose(0, 2, 1, 3, 4) \
                .reshape(b * d, s_len // d, A_SLOTS, hd)

        o, lse = banded_attention(split(q), split(k), split(v), w // d)
        outs.append(o.reshape(b, d, s_len // d, A_SLOTS, hd).transpose(0, 2, 1, 3, 4)
                    .reshape(b, s_len, A_SLOTS, hd))
        lses.append(lse.reshape(b, d, s_len // d, A_SLOTS).transpose(0, 2, 1, 3)
                    .reshape(b, s_len, A_SLOTS))
    wts = jax.nn.softmax(jnp.stack(lses, axis=0), axis=0)
    y = jnp.sum(wts[..., None] * jnp.stack(outs, axis=0).astype(jnp.float32), axis=0)
    return y.astype(q.dtype).reshape(b, s_len, A_SLOTS * hd)


def compress(t, pos_emb, w1, w2):
    b, s_len, g, hd = t.shape
    ch = t.reshape(b, s_len // CMP_STRIDE, CMP_STRIDE, g, hd)
    blocks = jnp.concatenate([ch[:, :-1], ch[:, 1:]], axis=2)
    blocks = blocks + pos_emb[None, None, :, None, :]
    nc = blocks.shape[1]
    flat = blocks.transpose(0, 1, 3, 2, 4).reshape(b, nc, g, CMP_LEN * hd)
    return jax.nn.gelu(flat @ w1) @ w2


def nsa_compressed_selected(q, kc, vc, ks, vs):
    b, s_len, hq, hd = q.shape
    g = kc.shape[2]
    rep = hq // g
    nc = kc.shape[1]
    ns = s_len // SEL_LEN
    n_sel = min(SEL_TOP, ns)
    nq = s_len // BLOCK
    scale = hd ** -0.5
    c_start = jnp.arange(nc) * CMP_STRIDE
    s_start = jnp.arange(ns) * SEL_LEN
    overlap = ((c_start[:, None] < s_start[None, :] + SEL_LEN) &
               (c_start[:, None] + CMP_LEN > s_start[None, :])).astype(jnp.float32)
    c_end = c_start + CMP_LEN - 1
    ks_b = ks.reshape(b, ns, SEL_LEN, g, hd).transpose(0, 3, 1, 2, 4)
    vs_b = vs.reshape(b, ns, SEL_LEN, g, hd).transpose(0, 3, 1, 2, 4)
    b_idx = jnp.arange(b)[:, None, None, None]
    g_idx = jnp.arange(g)[None, None, :, None]
    blk = jnp.arange(ns)
    tok = jnp.arange(SEL_LEN)
    qc = q.reshape(b, nq, BLOCK, g, rep, hd).transpose(1, 0, 2, 3, 4, 5)

    def chunk(args):
        q_c, c = args
        t = c * BLOCK + jnp.arange(BLOCK)
        s = jnp.einsum('bqgrd,bngd->bqgrn', q_c, kc, preferred_element_type=jnp.float32) * scale
        cmask = (c_end[None, :] <= t[:, None])[None, :, None, None, :]
        s = jnp.where(cmask, s, -1e30)
        p = jnp.where(cmask, jnp.exp(s - jnp.max(s, axis=-1, keepdims=True)), 0.0)
        p = p / jnp.maximum(jnp.sum(p, axis=-1, keepdims=True), 1e-30)
        o_cmp = jnp.einsum('bqgrn,bngd->bqgrd', p.astype(vc.dtype), vc)
        imp = jnp.einsum('bqgn,nm->bqgm', jnp.sum(p, axis=3), overlap)
        cur = (t // SEL_LEN)[:, None]
        forced = (blk[None, :] == 0) | (blk[None, :] == cur) | (blk[None, :] == cur - 1)
        imp = jnp.where(forced[None, :, None, :], FORCE, imp)
        imp = jnp.where((blk[None, :] > cur)[None, :, None, :], -FORCE, imp)
        _, idx = lax.top_k(imp, n_sel)
        ksel = ks_b[b_idx, g_idx, idx].reshape(b, BLOCK, g, n_sel * SEL_LEN, hd)
        vsel = vs_b[b_idx, g_idx, idx].reshape(b, BLOCK, g, n_sel * SEL_LEN, hd)
        kpos = (idx[..., None] * SEL_LEN + tok).reshape(b, BLOCK, g, n_sel * SEL_LEN)
        smask = (kpos <= t[None, :, None, None])[:, :, :, None, :]
        s2 = jnp.einsum('bqgrd,bqgkd->bqgrk', q_c, ksel, preferred_element_type=jnp.float32) * scale
        s2 = jnp.where(smask, s2, -1e30)
        p2 = jax.nn.softmax(s2, axis=-1)
        o_slc = jnp.einsum('bqgrk,bqgkd->bqgrd', p2.astype(vsel.dtype), vsel)
        return o_cmp, o_slc

    o_cmp, o_slc = lax.map(chunk, (qc, jnp.arange(nq)))
    o_cmp = o_cmp.transpose(1, 0, 2, 3, 4, 5).reshape(b, s_len, hq, hd)
    o_slc = o_slc.transpose(1, 0, 2, 3, 4, 5).reshape(b, s_len, hq, hd)
    return o_cmp, o_slc


def hybrid_layer(x, g_mix, w_in, cmp_pos_k, cmp_w1_k, cmp_w2_k, cmp_pos_v, cmp_w1_v, cmp_w2_v,
                 w_branch_a, w_branch_b, w_out, g_mlp, w_up, w_down, cos, sin):
    b, s_len, d = x.shape
    u = rms_norm(x, g_mix)
    proj = u @ w_in
    o1 = A_QKV
    o2 = o1 + B_Q
    o3 = o2 + B_KV_COLS
    o4 = o3 + B_GATE
    qkv_a, q_b, kv_b, gate_b, gate_m = jnp.split(proj, [o1, o2, o3, o4], axis=-1)
    qkv_a = qkv_a.reshape(b, s_len, 3, A_HEADS, HEAD_DIM)
    qa = partial_rotary(qkv_a[:, :, 0], cos, sin)
    ka = partial_rotary(qkv_a[:, :, 1], cos, sin)
    va = qkv_a[:, :, 2]
    q_b = partial_rotary(q_b.reshape(b, s_len, B_HEADS, HEAD_DIM), cos, sin)
    kv_b = kv_b.reshape(b, s_len, 6, B_KV, HEAD_DIM)
    k_cmp = partial_rotary(kv_b[:, :, 0], cos, sin)
    v_cmp = kv_b[:, :, 1]
    k_slc = partial_rotary(kv_b[:, :, 2], cos, sin)
    v_slc = kv_b[:, :, 3]
    k_win = partial_rotary(kv_b[:, :, 4], cos, sin)
    v_win = kv_b[:, :, 5]
    y_a = dilated_attention(qa, ka, va)
    kc = compress(k_cmp, cmp_pos_k, cmp_w1_k, cmp_w2_k)
    vc = compress(v_cmp, cmp_pos_v, cmp_w1_v, cmp_w2_v)
    o_cmp, o_slc = nsa_compressed_selected(q_b, kc, vc, k_slc, v_slc)
    o_win, _ = banded_attention(q_b, k_win, v_win, WIN_LEN - 1)
    gb = jax.nn.sigmoid(gate_b.reshape(b, s_len, B_HEADS, 3))
    y_b = (gb[..., 0:1] * o_cmp + gb[..., 1:2] * o_slc + gb[..., 2:3] * o_win).reshape(b, s_len, B_Q)
    gm = jax.nn.sigmoid(gate_m.reshape(b, s_len, 2, d))
    merged = gm[:, :, 0] * (y_a @ w_branch_a) + gm[:, :, 1] * (y_b @ w_branch_b)
    x = x + merged @ w_out
    h = rms_norm(x, g_mlp) @ w_up
    x = x + jnp.square(jax.nn.relu(h)) @ w_down
    return x


def setup_inputs(seed: int = 0) -> dict:
    key = jax.random.key(seed)
    ks = jax.random.split(key, 16)
    f = jnp.float32
    L = DEPTH

    def nrm(k, shape, fan_in):
        return jax.random.normal(k, shape, f) * (fan_in ** -0.5)

    hd = HEAD_DIM
    return {
        "x": jax.random.normal(ks[0], (BATCH, SEQ, D_MODEL), f),
        "norm_mix_g": 1.0 + 0.02 * jax.random.normal(ks[1], (L, D_MODEL), f),
        "w_in": nrm(ks[2], (L, D_MODEL, IN_COLS), D_MODEL),
        "cmp_pos_k": 0.02 * jax.random.normal(ks[3], (L, CMP_LEN, hd), f),
        "cmp_w1_k": nrm(ks[4], (L, CMP_LEN * hd, CMP_HIDDEN), CMP_LEN * hd),
        "cmp_w2_k": nrm(ks[5], (L, CMP_HIDDEN, hd), CMP_HIDDEN),
        "cmp_pos_v": 0.02 * jax.random.normal(ks[6], (L, CMP_LEN, hd), f),
        "cmp_w1_v": nrm(ks[7], (L, CMP_LEN * hd, CMP_HIDDEN), CMP_LEN * hd),
        "cmp_w2_v": nrm(ks[8], (L, CMP_HIDDEN, hd), CMP_HIDDEN),
        "w_branch_a": nrm(ks[9], (L, A_SLOTS * hd, D_MODEL), A_SLOTS * hd),
        "w_branch_b": nrm(ks[10], (L, B_Q, D_MODEL), B_Q),
        "w_out": nrm(ks[11], (L, D_MODEL, D_MODEL), D_MODEL),
        "norm_mlp_g": 1.0 + 0.02 * jax.random.normal(ks[12], (L, D_MODEL), f),
        "w_up": nrm(ks[13], (L, D_MODEL, D_FF), D_MODEL),
        "w_down": nrm(ks[14], (L, D_FF, D_MODEL), D_FF),
        "norm_final_g": 1.0 + 0.02 * jax.random.normal(ks[15], (D_MODEL,), f),
    }


def reference(x, norm_mix_g, w_in, cmp_pos_k, cmp_w1_k, cmp_w2_k, cmp_pos_v, cmp_w1_v, cmp_w2_v,
              w_branch_a, w_branch_b, w_out, norm_mlp_g, w_up, w_down, norm_final_g):
    cos, sin = rope_tables(x.shape[1])
    for l in range(DEPTH):
        x = hybrid_layer(x, norm_mix_g[l], w_in[l], cmp_pos_k[l], cmp_w1_k[l], cmp_w2_k[l],
                         cmp_pos_v[l], cmp_w1_v[l], cmp_w2_v[l], w_branch_a[l], w_branch_b[l],
                         w_out[l], norm_mlp_g[l], w_up[l], w_down[l], cos, sin)
    return rms_norm(x, norm_final_g)
```

```python
import functools

import jax
import jax.numpy as jnp
from jax import lax
from jax.experimental import pallas as pl
from jax.experimental.pallas import tpu as pltpu

F32 = jnp.float32
BF16 = jnp.bfloat16

D_MODEL = 1024
HEAD_DIM = 64
ROT_DIM = HEAD_DIM // 4
ROPE_THETA = 500000.0
EPS = 1e-6
DIL_PAIRS = ((128, 1), (512, 4), (2048, 16))
A_SLOTS = 8
A_HEADS = A_SLOTS * len(DIL_PAIRS)
B_HEADS = 8
B_KV = 2
CMP_LEN = 32
CMP_STRIDE = 16
CMP_HIDDEN = 4 * HEAD_DIM
SEL_LEN = 64
SEL_TOP = 16
WIN_LEN = 512
FORCE = 1e4
D_FF = 4 * D_MODEL
A_QKV = 3 * A_HEADS * HEAD_DIM
B_Q = B_HEADS * HEAD_DIM
B_KV_COLS = 6 * B_KV * HEAD_DIM
B_GATE = 3 * B_HEADS
MERGE_GATE = 2 * D_MODEL

LANES = 128
NEG = -1e30
SCALE = HEAD_DIM ** -0.5

A_COLS = 3 * A_SLOTS * HEAD_DIM
B_COLS = B_Q + B_KV_COLS
PROJ_CHUNK = 256
G_PAD = PROJ_CHUNK
G_COLS = MERGE_GATE + G_PAD
W_COLS = 3 * A_COLS + B_COLS + G_COLS

ROW_TILE = 512
VMEM_LIMIT = 56 * 1024 * 1024


def _cparams(n_axes, vmem=VMEM_LIMIT):
    return pltpu.CompilerParams(dimension_semantics=("arbitrary",) * n_axes, vmem_limit_bytes=vmem)


def _rms(x, g):
    return x * lax.rsqrt(jnp.mean(x * x, axis=-1, keepdims=True) + EPS) * g


def _dot(a, b):
    return jnp.dot(a, b, preferred_element_type=F32)


def _dot_nt(a, b):
    return lax.dot_general(a, b, (((1,), (1,)), ((), ())), preferred_element_type=F32)


def _split_bf16(x, parts):
    out = []
    for _ in range(parts - 1):
        hi = x.astype(BF16)
        out.append(hi)
        x = x - hi.astype(F32)
    out.append(x.astype(BF16))
    return out


_B_ROT = (True,) * (B_Q // LANES) + (True, False, True, False, True, False)


def _rotary(x, cos_t, sin_a, sin_b):
    half = ROT_DIM // 2
    return x * cos_t + pltpu.roll(x, LANES - half, 1) * sin_a + pltpu.roll(x, half, 1) * sin_b


def _in_proj_kernel(x_ref, g_ref, w_ref, t0_ref, t1_ref, t2_ref,
                    a0_ref, a1_ref, a2_ref, b_ref, gate_ref, u_scr):
    T = x_ref.shape[0]
    u = _rms(x_ref[...], g_ref[...])
    nslab = u_scr.shape[0]
    for k in range(nslab):
        u_scr[k] = u[:, k * LANES:(k + 1) * LANES]
    u_nat = u.astype(BF16)

    def dilated(d):
        rows = T // d
        return jnp.concatenate(
            [jnp.concatenate([u_scr[k, pl.ds(r, rows, stride=d), :] for k in range(nslab)], axis=1)
             for r in range(d)], axis=0).astype(BF16)

    def project(lhs, col0, ncols, rot, tables, store):
        per = PROJ_CHUNK // LANES
        for c in range(ncols // PROJ_CHUNK):
            res = _dot(lhs, w_ref[:, col0 + c * PROJ_CHUNK: col0 + (c + 1) * PROJ_CHUNK])
            for k in range(per):
                slab = res[:, k * LANES:(k + 1) * LANES]
                if rot[c * per + k]:
                    slab = _rotary(slab, tables[0], tables[1], tables[2])
                store((c * per + k) * LANES, slab)

    a_rot = (True,) * (2 * A_SLOTS * HEAD_DIM // LANES) + (False,) * (A_SLOTS * HEAD_DIM // LANES)

    tab0 = (t0_ref[0], t0_ref[1], t0_ref[2])

    def store_a0(c0, res):
        a0_ref[:, c0:c0 + LANES] = res.astype(BF16)

    project(u_nat, 0, A_COLS, a_rot, tab0, store_a0)

    for d, t_ref, a_ref, col0 in ((4, t1_ref, a1_ref, A_COLS), (16, t2_ref, a2_ref, 2 * A_COLS)):
        rows = T // d
        tabs = tuple(jnp.concatenate([t_ref[k, r] for r in range(d)], axis=0) for k in range(3))

        def store_ad(c0, res, a_ref=a_ref, d=d, rows=rows):
            for r in range(d):
                a_ref[r, :, c0:c0 + LANES] = res[r * rows:(r + 1) * rows].astype(BF16)

        project(dilated(d), col0, A_COLS, a_rot, tabs, store_ad)

    def store_b(c0, res):
        b_ref[:, c0:c0 + LANES] = res.astype(BF16)

    project(u_nat, 3 * A_COLS, B_COLS, _B_ROT, tab0, store_b)

    def store_g(c0, res):
        gate_ref[:, c0:c0 + LANES] = jax.nn.sigmoid(res)

    project(u_nat, 3 * A_COLS + B_COLS, G_COLS, (False,) * (G_COLS // LANES), tab0, store_g)


def _in_proj(x, g, w_all, tab0, tab1, tab2):
    B, S, D = x.shape
    T = ROW_TILE
    nt = S // T
    out_shape = (
        jax.ShapeDtypeStruct((B, S, A_COLS), BF16),
        jax.ShapeDtypeStruct((B, 4, S // 4, A_COLS), BF16),
        jax.ShapeDtypeStruct((B, 16, S // 16, A_COLS), BF16),
        jax.ShapeDtypeStruct((B, S, B_COLS), BF16),
        jax.ShapeDtypeStruct((B, S, G_COLS), F32),
    )
    in_specs = [
        pl.BlockSpec((None, T, D), lambda b, i: (b, i, 0)),
        pl.BlockSpec((1, D), lambda b, i: (0, 0)),
        pl.BlockSpec((D, W_COLS), lambda b, i: (0, 0), pipeline_mode=pl.Buffered(1)),
        pl.BlockSpec((3, T, LANES), lambda b, i: (0, i, 0)),
        pl.BlockSpec((3, 4, T // 4, LANES), lambda b, i: (0, 0, i, 0)),
        pl.BlockSpec((3, 16, T // 16, LANES), lambda b, i: (0, 0, i, 0)),
    ]
    out_specs = (
        pl.BlockSpec((None, T, A_COLS), lambda b, i: (b, i, 0)),
        pl.BlockSpec((None, 4, T // 4, A_COLS), lambda b, i: (b, 0, i, 0)),
        pl.BlockSpec((None, 16, T // 16, A_COLS), lambda b, i: (b, 0, i, 0)),
        pl.BlockSpec((None, T, B_COLS), lambda b, i: (b, i, 0)),
        pl.BlockSpec((None, T, G_COLS), lambda b, i: (b, i, 0)),
    )
    return pl.pallas_call(
        _in_proj_kernel, grid=(B, nt), in_specs=in_specs, out_specs=out_specs, out_shape=out_shape,
        scratch_shapes=[pltpu.VMEM((D // LANES, T, LANES), F32)], compiler_params=_cparams(2),
        name="in_proj",
    )(x, g, w_all, tab0, tab1, tab2)


def _gelu_tanh(x):
    return 0.5 * x * (1.0 + jnp.tanh(0.7978845608028654 * (x + 0.044715 * (x * x * x))))


def _compress_kernel(c_ref, pos_ref, w1_ref, w2_ref, o_ref):
    half = CMP_STRIDE * HEAD_DIM
    c = c_ref[...]
    nch = c.shape[0]
    p1 = _dot(c, w1_ref[:half, :])
    p2 = _dot(c, w1_ref[half:, :])
    pb = _dot(pos_ref[...], w1_ref[...])[0:1]
    p2_next = pltpu.roll(p2, nch - 1, 0)
    h = _gelu_tanh(p1 + p2_next + pb)
    o_ref[...] = _dot(h.astype(BF16), w2_ref[...]).astype(BF16)


def _compress(chunks, pos, w1, w2):
    _, B, G, nch, cw = chunks.shape
    return pl.pallas_call(
        _compress_kernel, grid=(2, B, G),
        in_specs=[
            pl.BlockSpec((None, None, None, nch, cw), lambda t, b, g: (t, b, g, 0, 0)),
            pl.BlockSpec((None, 8, 2 * cw), lambda t, b, g: (t, 0, 0)),
            pl.BlockSpec((None, 2 * cw, CMP_HIDDEN), lambda t, b, g: (t, 0, 0)),
            pl.BlockSpec((None, CMP_HIDDEN, HEAD_DIM), lambda t, b, g: (t, 0, 0)),
        ],
        out_specs=pl.BlockSpec((None, None, None, nch, HEAD_DIM), lambda t, b, g: (t, b, g, 0, 0)),
        out_shape=jax.ShapeDtypeStruct((2, B, G, nch, HEAD_DIM), BF16),
        compiler_params=_cparams(3), name="compress",
    )(chunks, pos, w1, w2)


CMP_TQ = 128


def _cmp_attn_kernel(q_ref, kc_ref, vc_ref, ov_ref, o_ref, sel_ref):
    i = pl.program_id(1)
    tq = q_ref.shape[0]
    nc = kc_ref.shape[1]
    ns = ov_ref.shape[1]
    rep = B_HEADS // B_KV
    t = i * tq + lax.broadcasted_iota(jnp.int32, (tq, 1), 0)
    n = lax.broadcasted_iota(jnp.int32, (1, nc), 1)
    cmask = (n * CMP_STRIDE + (CMP_LEN - 1)) <= t
    blk = lax.broadcasted_iota(jnp.int32, (1, ns), 1)
    cur = jnp.right_shift(t, SEL_LEN.bit_length() - 1)
    forced = (blk == 0) | (blk == cur) | (blk == cur - 1)
    future = blk > cur
    for g in range(B_KV):
        kc = kc_ref[g]
        vc = vc_ref[g]
        psum = jnp.zeros((tq, nc), F32)
        for r in range(rep):
            h = g * rep + r
            qh = q_ref[:, h * HEAD_DIM:(h + 1) * HEAD_DIM]
            s = _dot_nt(qh, kc) * SCALE
            s = jnp.where(cmask, s, NEG)
            p = jnp.where(cmask, jnp.exp(s - jnp.max(s, axis=-1, keepdims=True)), 0.0)
            p = p / jnp.maximum(jnp.sum(p, axis=-1, keepdims=True), 1e-30)
            o_ref[:, h * HEAD_DIM:(h + 1) * HEAD_DIM] = _dot(p.astype(BF16), vc).astype(BF16)
            psum = psum + p
        ov = ov_ref[...]
        imp = sum(_dot(piece, ov) for piece in _split_bf16(psum, 3))
        imp = jnp.where(forced, FORCE, imp)
        imp = jnp.where(future, -FORCE, imp)
        bias = jnp.full((tq, ns), NEG, F32)
        for _ in range(SEL_TOP):
            mx = jnp.max(imp, axis=-1, keepdims=True)
            first = jnp.min(jnp.where(imp == mx, blk, ns), axis=-1, keepdims=True)
            hit = blk == first
            bias = jnp.where(hit, 0.0, bias)
            imp = jnp.where(hit, -3e38, imp)
        sel_ref[:, g * ns:(g + 1) * ns] = bias.astype(BF16)


def _cmp_attn(nat_b, kc, vc, overlap):
    B, S, _ = nat_b.shape
    nc = kc.shape[2]
    ns = overlap.shape[1]
    tq = CMP_TQ
    return pl.pallas_call(
        _cmp_attn_kernel, grid=(B, S // tq),
        in_specs=[
            pl.BlockSpec((None, tq, B_Q), lambda b, i: (b, i, 0)),
            pl.BlockSpec((None, B_KV, nc, HEAD_DIM), lambda b, i: (b, 0, 0, 0)),
            pl.BlockSpec((None, B_KV, nc, HEAD_DIM), lambda b, i: (b, 0, 0, 0)),
            pl.BlockSpec((nc, ns), lambda b, i: (0, 0)),
        ],
        out_specs=(
            pl.BlockSpec((None, tq, B_Q), lambda b, i: (b, i, 0)),
            pl.BlockSpec((None, tq, B_KV * ns), lambda b, i: (b, i, 0)),
        ),
        out_shape=(
            jax.ShapeDtypeStruct((B, S, B_Q), BF16),
            jax.ShapeDtypeStruct((B, S, B_KV * ns), BF16),
        ),
        compiler_params=_cparams(2), name="cmp_attn",
    )(nat_b, kc, vc, overlap)


SLC_TQ = 256
SLC_TK = 256


def _slc_attn_kernel(q_ref, sel_ref, kt_ref, v_ref, o_ref, m_scr, l_scr, acc_scr):
    i = pl.program_id(2)
    tq = q_ref.shape[0]
    rep = B_HEADS // B_KV
    sel = sel_ref[...]
    qa = jnp.concatenate(
        [jnp.concatenate([q_ref[:, r * HEAD_DIM:(r + 1) * HEAD_DIM] * SCALE, sel], axis=1)
         for r in range(rep)], axis=0).astype(BF16)
    m_scr[...] = jnp.full(m_scr.shape, NEG, F32)
    l_scr[...] = jnp.zeros(l_scr.shape, F32)
    acc_scr[...] = jnp.zeros(acc_scr.shape, F32)
    q0 = i * tq
    row = lax.broadcasted_iota(jnp.int32, (tq, SLC_TK), 0)
    col = lax.broadcasted_iota(jnp.int32, (tq, SLC_TK), 1)
    rel = row - col

    def body(j, carry):
        k0 = pl.multiple_of(j * SLC_TK, SLC_TK)
        kt = kt_ref[:, pl.ds(k0, SLC_TK)]
        vj = v_ref[pl.ds(k0, SLC_TK), :]
        s_all = _dot(qa, kt)
        causal = rel >= (k0 - q0)
        for r in range(rep):
            rows = slice(r * tq, (r + 1) * tq)
            s = jnp.where(causal, s_all[rows], NEG)
            m_prev = m_scr[rows]
            m_new = jnp.maximum(m_prev, jnp.max(s, axis=-1, keepdims=True))
            alpha = jnp.exp(m_prev - m_new)
            p = jnp.exp(s - jnp.concatenate([m_new] * (SLC_TK // LANES), axis=1))
            l_scr[rows] = alpha * l_scr[rows] + jnp.sum(p, axis=-1, keepdims=True)
            acc_scr[rows] = alpha[:, :HEAD_DIM] * acc_scr[rows] + _dot(p.astype(BF16), vj)
            m_scr[rows] = m_new
        return carry

    lax.fori_loop(0, (q0 + tq) // SLC_TK, body, 0)
    for r in range(rep):
        rows = slice(r * tq, (r + 1) * tq)
        o_ref[:, r * HEAD_DIM:(r + 1) * HEAD_DIM] = (
            acc_scr[rows] / l_scr[rows][:, :HEAD_DIM]).astype(BF16)


def _slc_attn(nat_b, selb, kt_aug, v_slc):
    B, S, _ = nat_b.shape
    G = B_KV
    rep = B_HEADS // B_KV
    ns = selb.shape[2] // G
    tq = SLC_TQ
    gw = rep * HEAD_DIM
    return pl.pallas_call(
        _slc_attn_kernel, grid=(B, G, S // tq),
        in_specs=[
            pl.BlockSpec((None, tq, gw), lambda b, g, i: (b, i, g)),
            pl.BlockSpec((None, tq, ns), lambda b, g, i: (b, i, g)),
            pl.BlockSpec((None, None, HEAD_DIM + ns, S), lambda b, g, i: (b, g, 0, 0)),
            pl.BlockSpec((None, None, S, HEAD_DIM), lambda b, g, i: (b, g, 0, 0)),
        ],
        out_specs=pl.BlockSpec((None, tq, gw), lambda b, g, i: (b, i, g)),
        out_shape=jax.ShapeDtypeStruct((B, S, B_Q), BF16),
        scratch_shapes=[pltpu.VMEM((rep * tq, LANES), F32), pltpu.VMEM((rep * tq, LANES), F32),
                        pltpu.VMEM((rep * tq, HEAD_DIM), F32)],
        compiler_params=_cparams(3), name="slc_attn",
    )(nat_b, selb, kt_aug, v_slc)


SUB = 128


def _band_kernel(q_ref, kp_ref, kc_ref, vp_ref, vc_ref, o_ref, *lse_refs, hq, hk, prev, max_off):
    i = pl.program_id(1)
    tq = q_ref.shape[0]
    rep = hq // hk
    width = prev + SUB
    r_i = lax.broadcasted_iota(jnp.int32, (SUB, width), 0)
    c_i = lax.broadcasted_iota(jnp.int32, (SUB, width), 1)
    diff = r_i - c_i + prev
    in_band = (diff >= 0) & (diff <= max_off)

    def window(p_ref, c_ref, sub, g):
        cols = slice(g * HEAD_DIM, (g + 1) * HEAD_DIM)
        lo = sub * SUB
        parts = []
        if lo < prev:
            parts.append(p_ref[lo:prev, cols])
        parts.append(c_ref[max(0, lo - prev):lo + SUB, cols])
        return jnp.concatenate(parts, axis=0) if len(parts) > 1 else parts[0]

    for sub in range(tq // SUB):
        k_start = i * tq + sub * SUB - prev
        bias = jnp.where(in_band & (c_i + k_start >= 0), 0.0, NEG)
        for g in range(hk):
            kw = window(kp_ref, kc_ref, sub, g)
            vw = window(vp_ref, vc_ref, sub, g)
            for r in range(rep):
                h = g * rep + r
                cols = slice(h * HEAD_DIM, (h + 1) * HEAD_DIM)
                qh = q_ref[sub * SUB:(sub + 1) * SUB, cols] * SCALE
                s = _dot_nt(qh, kw) + bias
                m = jnp.max(s, axis=-1, keepdims=True)
                p = jnp.exp(s - m)
                den = jnp.sum(p, axis=-1, keepdims=True)
                o = _dot(p.astype(BF16), vw) / den
                o_ref[sub * SUB:(sub + 1) * SUB, cols] = o.astype(o_ref.dtype)
                if lse_refs:
                    lse = m + jnp.log(den)
                    lse_refs[0][sub * SUB:(sub + 1) * SUB, cols] = jnp.broadcast_to(
                        lse, (SUB, HEAD_DIM))


def _band_attn(q_arr, k_arr, v_arr, *, q_blk, k_blk, v_blk, hq, hk, max_off, tq, with_lse):
    N, L, _ = q_arr.shape
    prev = -(-max_off // SUB) * SUB
    m = tq // prev
    qw, kw = hq * HEAD_DIM, hk * HEAD_DIM
    cur = lambda blk: (lambda n, i: (n, i, blk))
    prv = lambda blk: (lambda n, i: (n, jnp.maximum(i * m - 1, 0), blk))
    out_shape = [jax.ShapeDtypeStruct((N, L, qw), BF16)]
    out_specs = [pl.BlockSpec((None, tq, qw), cur(0))]
    if with_lse:
        out_shape.append(jax.ShapeDtypeStruct((N, L, qw), F32))
        out_specs.append(pl.BlockSpec((None, tq, qw), cur(0)))
    return pl.pallas_call(
        functools.partial(_band_kernel, hq=hq, hk=hk, prev=prev, max_off=max_off),
        grid=(N, L // tq),
        in_specs=[
            pl.BlockSpec((None, tq, qw), cur(q_blk)),
            pl.BlockSpec((None, prev, kw), prv(k_blk)),
            pl.BlockSpec((None, tq, kw), cur(k_blk)),
            pl.BlockSpec((None, prev, kw), prv(v_blk)),
            pl.BlockSpec((None, tq, kw), cur(v_blk)),
        ],
        out_specs=tuple(out_specs), out_shape=tuple(out_shape),
        compiler_params=_cparams(2), name="band_attn",
    )(q_arr, k_arr, k_arr, v_arr, v_arr)


def _merge_kernel(x_ref, o0_ref, l0_ref, o1_ref, l1_ref, o2_ref, l2_ref, oc_ref, os_ref, ow_ref,
                  gate_ref, ex_ref, wa_ref, wb_ref, wo_ref, out_ref, s_o1, s_l1, s_o2, s_l2):
    T = x_ref.shape[0]
    for d, o_ref, l_ref, s_o, s_l in ((4, o1_ref, l1_ref, s_o1, s_l1), (16, o2_ref, l2_ref, s_o2, s_l2)):
        rows = T // d
        for r in range(d):
            o_r = o_ref[r].astype(F32)
            l_r = l_ref[r]
            for k in range(s_o.shape[0]):
                lanes = slice(k * LANES, (k + 1) * LANES)
                s_o[k, pl.ds(r, rows, stride=d), :] = o_r[:, lanes]
                s_l[k, pl.ds(r, rows, stride=d), :] = l_r[:, lanes]
    gather = lambda s: jnp.concatenate([s[k] for k in range(s.shape[0])], axis=1)
    l0, l1, l2 = l0_ref[...], gather(s_l1), gather(s_l2)
    mx = jnp.maximum(jnp.maximum(l0, l1), l2)
    e0, e1, e2 = jnp.exp(l0 - mx), jnp.exp(l1 - mx), jnp.exp(l2 - mx)
    y_a = (e0 * o0_ref[...].astype(F32) + e1 * gather(s_o1) + e2 * gather(s_o2)) / (e0 + e1 + e2)

    gb = gate_ref[:, MERGE_GATE:]
    ex = ex_ref[...]
    gbx = sum(_dot(piece, ex) for piece in _split_bf16(gb, 2))
    y_b = (gbx[:, :B_Q] * oc_ref[...].astype(F32)
           + gbx[:, B_Q:2 * B_Q] * os_ref[...].astype(F32)
           + gbx[:, 2 * B_Q:] * ow_ref[...].astype(F32))
    merged = (gate_ref[:, :D_MODEL] * _dot(y_a.astype(BF16), wa_ref[...])
              + gate_ref[:, D_MODEL:MERGE_GATE] * _dot(y_b.astype(BF16), wb_ref[...]))
    out_ref[...] = x_ref[...] + _dot(merged.astype(BF16), wo_ref[...])


def _merge(x, o0, l0, o1, l1, o2, l2, o_cmp, o_slc, o_win, gates, ex, wa, wb, wo):
    B, S, D = x.shape
    T = ROW_TILE
    qw = A_SLOTS * HEAD_DIM
    nat = lambda w: pl.BlockSpec((None, T, w), lambda b, i: (b, i, 0))
    dil = lambda d: pl.BlockSpec((None, d, T // d, qw), lambda b, i: (b, 0, i, 0))
    full = lambda a: pl.BlockSpec(a.shape, lambda b, i: (0,) * a.ndim)
    return pl.pallas_call(
        _merge_kernel, grid=(B, S // T),
        in_specs=[nat(D), nat(qw), nat(qw), dil(4), dil(4), dil(16), dil(16), nat(B_Q), nat(B_Q),
                  nat(B_Q), nat(G_COLS), full(ex), full(wa), full(wb), full(wo)],
        out_specs=nat(D), out_shape=jax.ShapeDtypeStruct((B, S, D), F32),
        scratch_shapes=[pltpu.VMEM((qw // LANES, T, LANES), F32)] * 4,
        compiler_params=_cparams(2), name="merge",
    )(x, o0, l0, o1, l1, o2, l2, o_cmp, o_slc, o_win, gates, ex, wa, wb, wo)


FF_CHUNK = 512


def _ffn_kernel(x_ref, g_ref, wu_ref, wd_ref, gf_ref, out_ref, *, final_norm):
    x = x_ref[...]
    u = _rms(x, g_ref[...]).astype(BF16)
    acc = x
    for c in range(D_FF // FF_CHUNK):
        cols = slice(c * FF_CHUNK, (c + 1) * FF_CHUNK)
        h = jnp.maximum(_dot(u, wu_ref[:, cols]), 0.0)
        acc = acc + _dot((h * h).astype(BF16), wd_ref[cols, :])
    out_ref[...] = _rms(acc, gf_ref[...]) if final_norm else acc


def _ffn(x, g_mlp, w_up, w_down, g_final, final_norm):
    B, S, D = x.shape
    T = ROW_TILE
    row = pl.BlockSpec((None, T, D), lambda b, i: (b, i, 0))
    vec = pl.BlockSpec((1, D), lambda b, i: (0, 0))
    return pl.pallas_call(
        functools.partial(_ffn_kernel, final_norm=final_norm), grid=(B, S // T),
        in_specs=[row, vec,
                  pl.BlockSpec((D, D_FF), lambda b, i: (0, 0), pipeline_mode=pl.Buffered(1)),
                  pl.BlockSpec((D_FF, D), lambda b, i: (0, 0), pipeline_mode=pl.Buffered(1)),
                  vec],
        out_specs=row, out_shape=jax.ShapeDtypeStruct((B, S, D), F32),
        compiler_params=_cparams(2), name="ffn",
    )(x, g_mlp, w_up, w_down, g_final)


def _rope_tables(seq):
    pos = jnp.arange(seq, dtype=F32)
    inv = ROPE_THETA ** (-jnp.arange(0, ROT_DIM, 2, dtype=F32) / ROT_DIM)
    ang = pos[:, None] * inv[None, :]
    cos, sin = jnp.cos(ang), jnp.sin(ang)
    half = ROT_DIM // 2
    z = lambda w: jnp.zeros((seq, w), F32)
    per_head = (
        jnp.concatenate([cos, cos, jnp.ones((seq, HEAD_DIM - ROT_DIM), F32)], axis=1),
        jnp.concatenate([-sin, z(HEAD_DIM - half)], axis=1),
        jnp.concatenate([z(half), sin, z(HEAD_DIM - ROT_DIM)], axis=1),
    )
    tab = jnp.stack([jnp.tile(t, (1, LANES // HEAD_DIM)) for t in per_head])
    dil = lambda d: tab.reshape(3, seq // d, d, LANES).transpose(0, 2, 1, 3)
    return tab, dil(4), dil(16)


def _layer_weights(w_in):
    hw = A_SLOTS * HEAD_DIM
    na = A_HEADS * HEAD_DIM
    groups = []
    for g in range(len(DIL_PAIRS)):
        groups += [w_in[:, t * na + g * hw: t * na + (g + 1) * hw] for t in range(3)]
    o3 = A_QKV + B_Q + B_KV_COLS
    o4 = o3 + B_GATE
    groups.append(w_in[:, A_QKV:o3])
    groups.append(w_in[:, o4:])
    groups.append(jnp.pad(w_in[:, o3:o4], ((0, 0), (0, G_PAD - B_GATE))))
    return jnp.concatenate(groups, axis=1).astype(BF16)


def _overlap_matrix(nc_pad, ns):
    c_start = jnp.arange(nc_pad) * CMP_STRIDE
    s_start = jnp.arange(ns) * SEL_LEN
    ov = (c_start[:, None] < s_start[None, :] + SEL_LEN) & (c_start[:, None] + CMP_LEN > s_start[None, :])
    return ov.astype(BF16)


def _gate_expand():
    rows = jnp.arange(G_PAD)[:, None]
    cols = jnp.arange(3 * B_Q)[None, :]
    br, h = cols // B_Q, (cols % B_Q) // HEAD_DIM
    return (rows == h * 3 + br).astype(BF16)


def _layer(x, g_mix, w_in, cmp_pos_k, cmp_w1_k, cmp_w2_k, cmp_pos_v, cmp_w1_v, cmp_w2_v,
           w_branch_a, w_branch_b, w_out, tabs):
    B, S, D = x.shape
    G = B_KV
    a0, a1, a2, nat_b, gates = _in_proj(x, g_mix.reshape(1, D), _layer_weights(w_in), *tabs)

    a_out = []
    for arr, (w, d) in zip((a0, a1, a2), DIL_PAIRS):
        arr = arr.reshape(B * d, S // d, A_COLS)
        a_out.append(_band_attn(arr, arr, arr, q_blk=0, k_blk=1, v_blk=2, hq=A_SLOTS, hk=A_SLOTS,
                                max_off=w // d, tq=256, with_lse=True))
    (o0, l0), (o1, l1), (o2, l2) = a_out
    qw = A_SLOTS * HEAD_DIM
    o1, l1 = o1.reshape(B, 4, S // 4, qw), l1.reshape(B, 4, S // 4, qw)
    o2, l2 = o2.reshape(B, 16, S // 16, qw), l2.reshape(B, 16, S // 16, qw)

    nch = S // CMP_STRIDE
    kv = nat_b[:, :, B_Q:].reshape(B, S, 6, G, HEAD_DIM)
    chunks = jnp.stack([kv[:, :, 0], kv[:, :, 1]]).transpose(0, 1, 3, 2, 4).reshape(
        2, B, G, nch, CMP_STRIDE * HEAD_DIM)
    pos = jnp.stack([cmp_pos_k, cmp_pos_v]).reshape(2, 1, CMP_LEN * HEAD_DIM)
    pos = jnp.broadcast_to(pos, (2, 8, CMP_LEN * HEAD_DIM)).astype(BF16)
    kcvc = _compress(chunks, pos, jnp.stack([cmp_w1_k, cmp_w1_v]).astype(BF16),
                     jnp.stack([cmp_w2_k, cmp_w2_v]).astype(BF16))
    ns = S // SEL_LEN
    o_cmp, selb = _cmp_attn(nat_b, kcvc[0], kcvc[1], _overlap_matrix(nch, ns))

    k_slc_t = kv[:, :, 2].transpose(0, 2, 3, 1)
    onehot = (jnp.arange(ns)[:, None] == (jnp.arange(S) // SEL_LEN)[None, :]).astype(BF16)
    kt_aug = jnp.concatenate([k_slc_t, jnp.broadcast_to(onehot, (B, G, ns, S))], axis=2)
    v_slc = kv[:, :, 3].transpose(0, 2, 1, 3)
    o_slc = _slc_attn(nat_b, selb, kt_aug, v_slc)

    kw_blk = (B_Q + 4 * G * HEAD_DIM) // (G * HEAD_DIM)
    (o_win,) = _band_attn(nat_b, nat_b, nat_b, q_blk=0, k_blk=kw_blk, v_blk=kw_blk + 1,
                          hq=B_HEADS, hk=B_KV, max_off=WIN_LEN - 1, tq=512, with_lse=False)

    return _merge(x, o0, l0, o1, l1, o2, l2, o_cmp, o_slc, o_win, gates, _gate_expand(),
                  w_branch_a.astype(BF16), w_branch_b.astype(BF16), w_out.astype(BF16))


def kernel(x, norm_mix_g, w_in, cmp_pos_k, cmp_w1_k, cmp_w2_k, cmp_pos_v, cmp_w1_v, cmp_w2_v,
           w_branch_a, w_branch_b, w_out, norm_mlp_g, w_up, w_down, norm_final_g):
    B, S, D = x.shape
    depth = w_in.shape[0]
    tabs = _rope_tables(S)
    for l in range(depth):
        x1 = _layer(x, norm_mix_g[l], w_in[l], cmp_pos_k[l], cmp_w1_k[l], cmp_w2_k[l],
                    cmp_pos_v[l], cmp_w1_v[l], cmp_w2_v[l], w_branch_a[l], w_branch_b[l],
                    w_out[l], tabs)
        x = _ffn(x1, norm_mlp_g[l].reshape(1, D), w_up[l].astype(BF16), w_down[l].astype(BF16),
                 norm_final_g.reshape(1, D), final_norm=(l == depth - 1))
    return x
```

```python
import functools

import jax
import jax.numpy as jnp
from jax import lax
from jax.experimental import pallas as pl
from jax.experimental.pallas import tpu as pltpu

F32 = jnp.float32
BF16 = jnp.bfloat16

D_MODEL = 1024
HEAD_DIM = 64
ROT_DIM = HEAD_DIM // 4
ROPE_THETA = 500000.0
EPS = 1e-6
DIL_PAIRS = ((128, 1), (512, 4), (2048, 16))
A_SLOTS = 8
A_HEADS = A_SLOTS * len(DIL_PAIRS)
B_HEADS = 8
B_KV = 2
CMP_LEN = 32
CMP_STRIDE = 16
CMP_HIDDEN = 4 * HEAD_DIM
SEL_LEN = 64
SEL_TOP = 16
WIN_LEN = 512
FORCE = 1e4
D_FF = 4 * D_MODEL
A_QKV = 3 * A_HEADS * HEAD_DIM
B_Q = B_HEADS * HEAD_DIM
B_KV_COLS = 6 * B_KV * HEAD_DIM
B_GATE = 3 * B_HEADS
MERGE_GATE = 2 * D_MODEL

LANES = 128
NEG = -1e30
QSCALE = HEAD_DIM ** -0.5 * 1.4426950408889634

A_COLS = 3 * A_SLOTS * HEAD_DIM
B_COLS = B_Q + B_KV_COLS
PROJ_CHUNK = 256
G_PAD = PROJ_CHUNK
G_COLS = MERGE_GATE + G_PAD
W_COLS = 3 * A_COLS + B_COLS + G_COLS

ROW_TILE = 512
VMEM_LIMIT = 56 * 1024 * 1024


def _cparams(n_axes, vmem=VMEM_LIMIT):
    return pltpu.CompilerParams(dimension_semantics=("arbitrary",) * n_axes, vmem_limit_bytes=vmem)


def _rms(x, g):
    return x * lax.rsqrt(jnp.mean(x * x, axis=-1, keepdims=True) + EPS) * g


def _dot(a, b):
    return jnp.dot(a, b, preferred_element_type=F32)


def _dot_nt(a, b):
    return lax.dot_general(a, b, (((1,), (1,)), ((), ())), preferred_element_type=F32)


def _split_bf16(x, parts):
    out = []
    for _ in range(parts - 1):
        hi = x.astype(BF16)
        out.append(hi)
        x = x - hi.astype(F32)
    out.append(x.astype(BF16))
    return out


_B_ROT = (True,) * (B_Q // LANES) + (True, False, True, False, True, False)


def _rotary(x, cos_t, sin_a, sin_b):
    half = ROT_DIM // 2
    return x * cos_t + pltpu.roll(x, LANES - half, 1) * sin_a + pltpu.roll(x, half, 1) * sin_b


def _in_proj_kernel(x_ref, g_ref, w_ref, t0_ref, t1_ref, t2_ref,
                    a0_ref, a1_ref, a2_ref, b_ref, gate_ref, u_scr):
    T = x_ref.shape[0]
    u = _rms(x_ref[...], g_ref[...])
    nslab = u_scr.shape[0]
    for k in range(nslab):
        u_scr[k] = u[:, k * LANES:(k + 1) * LANES]
    u_nat = u.astype(BF16)

    def dilated(d):
        rows = T // d
        return jnp.concatenate(
            [jnp.concatenate([u_scr[k, pl.ds(r, rows, stride=d), :] for k in range(nslab)], axis=1)
             for r in range(d)], axis=0).astype(BF16)

    q_slabs = A_SLOTS * HEAD_DIM // LANES

    def project(lhs, col0, ncols, rot, tables, store, n_q=0):
        per = PROJ_CHUNK // LANES
        for c in range(ncols // PROJ_CHUNK):
            res = _dot(lhs, w_ref[:, col0 + c * PROJ_CHUNK: col0 + (c + 1) * PROJ_CHUNK])
            for k in range(per):
                slab = res[:, k * LANES:(k + 1) * LANES]
                if rot[c * per + k]:
                    slab = _rotary(slab, tables[0], tables[1], tables[2])
                if c * per + k < n_q:
                    slab = slab * QSCALE
                store((c * per + k) * LANES, slab)

    a_rot = (True,) * (2 * q_slabs) + (False,) * q_slabs

    tab0 = (t0_ref[0], t0_ref[1], t0_ref[2])

    def store_a0(c0, res):
        a0_ref[:, c0:c0 + LANES] = res.astype(BF16)

    project(u_nat, 0, A_COLS, a_rot, tab0, store_a0, n_q=q_slabs)

    for d, t_ref, a_ref, col0 in ((4, t1_ref, a1_ref, A_COLS), (16, t2_ref, a2_ref, 2 * A_COLS)):
        rows = T // d
        tabs = tuple(jnp.concatenate([t_ref[k, r] for r in range(d)], axis=0) for k in range(3))

        def store_ad(c0, res, a_ref=a_ref, d=d, rows=rows):
            for r in range(d):
                a_ref[r, :, c0:c0 + LANES] = res[r * rows:(r + 1) * rows].astype(BF16)

        project(dilated(d), col0, A_COLS, a_rot, tabs, store_ad, n_q=q_slabs)

    def store_b(c0, res):
        b_ref[:, c0:c0 + LANES] = res.astype(BF16)

    project(u_nat, 3 * A_COLS, B_COLS, _B_ROT, tab0, store_b, n_q=q_slabs)

    def store_g(c0, res):
        gate_ref[:, c0:c0 + LANES] = jax.nn.sigmoid(res)

    project(u_nat, 3 * A_COLS + B_COLS, G_COLS, (False,) * (G_COLS // LANES), tab0, store_g)


def _in_proj(x, g, w_all, tab0, tab1, tab2):
    B, S, D = x.shape
    T = ROW_TILE
    nt = S // T
    out_shape = (
        jax.ShapeDtypeStruct((B, S, A_COLS), BF16),
        jax.ShapeDtypeStruct((B * 4, S // 4, A_COLS), BF16),
        jax.ShapeDtypeStruct((B * 16, S // 16, A_COLS), BF16),
        jax.ShapeDtypeStruct((B, S, B_COLS), BF16),
        jax.ShapeDtypeStruct((B, S, G_COLS), F32),
    )
    in_specs = [
        pl.BlockSpec((None, T, D), lambda b, i: (b, i, 0)),
        pl.BlockSpec((1, D), lambda b, i: (0, 0)),
        pl.BlockSpec((D, W_COLS), lambda b, i: (0, 0), pipeline_mode=pl.Buffered(1)),
        pl.BlockSpec((3, T, LANES), lambda b, i: (0, i, 0)),
        pl.BlockSpec((3, 4, T // 4, LANES), lambda b, i: (0, 0, i, 0)),
        pl.BlockSpec((3, 16, T // 16, LANES), lambda b, i: (0, 0, i, 0)),
    ]
    out_specs = (
        pl.BlockSpec((None, T, A_COLS), lambda b, i: (b, i, 0)),
        pl.BlockSpec((4, T // 4, A_COLS), lambda b, i: (b, i, 0)),
        pl.BlockSpec((16, T // 16, A_COLS), lambda b, i: (b, i, 0)),
        pl.BlockSpec((None, T, B_COLS), lambda b, i: (b, i, 0)),
        pl.BlockSpec((None, T, G_COLS), lambda b, i: (b, i, 0)),
    )
    return pl.pallas_call(
        _in_proj_kernel, grid=(B, nt), in_specs=in_specs, out_specs=out_specs, out_shape=out_shape,
        scratch_shapes=[pltpu.VMEM((D // LANES, T, LANES), F32)], compiler_params=_cparams(2),
        name="in_proj",
    )(x, g, w_all, tab0, tab1, tab2)


def _gelu_tanh(x):
    return 0.5 * x * (1.0 + jnp.tanh(0.7978845608028654 * (x + 0.044715 * (x * x * x))))


def _compress_kernel(c_ref, pos_ref, w1_ref, w2_ref, o_ref):
    half = CMP_STRIDE * HEAD_DIM
    c = c_ref[...]
    nch = c.shape[0]
    p1 = _dot(c, w1_ref[:half, :])
    p2 = _dot(c, w1_ref[half:, :])
    pb = _dot(pos_ref[...], w1_ref[...])[0:1]
    p2_next = pltpu.roll(p2, nch - 1, 0)
    h = _gelu_tanh(p1 + p2_next + pb)
    o_ref[...] = _dot(h.astype(BF16), w2_ref[...]).astype(BF16)


def _compress(chunks, pos, w1, w2):
    _, B, G, nch, cw = chunks.shape
    return pl.pallas_call(
        _compress_kernel, grid=(2, B, G),
        in_specs=[
            pl.BlockSpec((None, None, None, nch, cw), lambda t, b, g: (t, b, g, 0, 0)),
            pl.BlockSpec((None, 8, 2 * cw), lambda t, b, g: (t, 0, 0)),
            pl.BlockSpec((None, 2 * cw, CMP_HIDDEN), lambda t, b, g: (t, 0, 0)),
            pl.BlockSpec((None, CMP_HIDDEN, HEAD_DIM), lambda t, b, g: (t, 0, 0)),
        ],
        out_specs=pl.BlockSpec((None, None, None, nch, HEAD_DIM), lambda t, b, g: (t, b, g, 0, 0)),
        out_shape=jax.ShapeDtypeStruct((2, B, G, nch, HEAD_DIM), BF16),
        compiler_params=_cparams(3), name="compress",
    )(chunks, pos, w1, w2)


CMP_TQ = 256


def _cmp_attn_kernel(q_ref, kc_ref, vct_ref, ovt_ref, o_ref, sel_ref):
    i = pl.program_id(1)
    tq = q_ref.shape[0]
    nc = kc_ref.shape[1]
    ns = ovt_ref.shape[0]
    rep = B_HEADS // B_KV
    t = i * tq + lax.broadcasted_iota(jnp.int32, (1, tq), 1)
    n = lax.broadcasted_iota(jnp.int32, (nc, 1), 0)
    cmask = (n * CMP_STRIDE + (CMP_LEN - 1)) <= t
    blk = lax.broadcasted_iota(jnp.int32, (ns, 1), 0)
    blk_f = blk.astype(F32)
    cur = jnp.right_shift(t, SEL_LEN.bit_length() - 1)
    forced = (blk == 0) | (blk == cur) | (blk == cur - 1)
    future = blk > cur
    o_t = []
    for g in range(B_KV):
        kc = kc_ref[g]
        vct = vct_ref[g]
        psum = jnp.zeros((nc, tq), F32)
        for r in range(rep):
            h = g * rep + r
            qh = q_ref[:, h * HEAD_DIM:(h + 1) * HEAD_DIM]
            s = jnp.where(cmask, _dot_nt(kc, qh), NEG)
            p = jnp.where(cmask, jnp.exp2(s - jnp.max(s, axis=0, keepdims=True)), 0.0)
            p = p / jnp.maximum(jnp.sum(p, axis=0, keepdims=True), 1e-30)
            o_t.append(_dot(vct, p.astype(BF16)))
            psum = psum + p
        ovt = ovt_ref[...]
        imp = sum(_dot(ovt, piece) for piece in _split_bf16(psum, 3))
        imp = jnp.where(forced, FORCE, imp)
        imp = jnp.where(future, -FORCE, imp)
        bias = jnp.full((ns, tq), NEG, F32)
        for _ in range(SEL_TOP):
            mx = jnp.max(imp, axis=0, keepdims=True)
            first = jnp.min(jnp.where(imp == mx, blk_f, float(ns)), axis=0, keepdims=True)
            hit = blk_f == first
            bias = jnp.where(hit, 0.0, bias)
            imp = jnp.where(hit, -3e38, imp)
        sel_ref[:, g * ns:(g + 1) * ns] = jnp.transpose(bias).astype(BF16)
    o_ref[...] = jnp.transpose(jnp.concatenate(o_t, axis=0)).astype(BF16)


def _cmp_attn(nat_b, kc, vct, overlap_t):
    B, S, _ = nat_b.shape
    nc = kc.shape[2]
    ns = overlap_t.shape[0]
    tq = CMP_TQ
    return pl.pallas_call(
        _cmp_attn_kernel, grid=(B, S // tq),
        in_specs=[
            pl.BlockSpec((None, tq, B_Q), lambda b, i: (b, i, 0)),
            pl.BlockSpec((None, B_KV, nc, HEAD_DIM), lambda b, i: (b, 0, 0, 0)),
            pl.BlockSpec((None, B_KV, HEAD_DIM, nc), lambda b, i: (b, 0, 0, 0)),
            pl.BlockSpec((ns, nc), lambda b, i: (0, 0)),
        ],
        out_specs=(
            pl.BlockSpec((None, tq, B_Q), lambda b, i: (b, i, 0)),
            pl.BlockSpec((None, tq, B_KV * ns), lambda b, i: (b, i, 0)),
        ),
        out_shape=(
            jax.ShapeDtypeStruct((B, S, B_Q), BF16),
            jax.ShapeDtypeStruct((B, S, B_KV * ns), BF16),
        ),
        compiler_params=_cparams(2), name="cmp_attn",
    )(nat_b, kc, vct, overlap_t)


SLC_TQ = 256
SLC_TK = 256


def _slc_attn_kernel(q_ref, sel_ref, kt_ref, v_ref, o_ref, qa_scr, m_scr, acc_scr, s_a, s_b):
    i = pl.program_id(2)
    tq = q_ref.shape[0]
    rep = B_HEADS // B_KV
    sel = sel_ref[...]
    for r in range(rep):
        qa_scr[r * tq:(r + 1) * tq] = jnp.concatenate(
            [q_ref[:, r * HEAD_DIM:(r + 1) * HEAD_DIM], sel], axis=1)
    m_scr[...] = jnp.full(m_scr.shape, NEG, F32)
    acc_scr[...] = jnp.zeros(acc_scr.shape, F32)
    row = lax.broadcasted_iota(jnp.int32, (tq, SLC_TK), 0)
    col = lax.broadcasted_iota(jnp.int32, (tq, SLC_TK), 1)
    on_or_below_diag = row >= col

    def scores(j, s_buf):
        k0 = pl.multiple_of(j * SLC_TK, SLC_TK)
        s_buf[...] = _dot(qa_scr[...], kt_ref[:, pl.ds(k0, SLC_TK)])

    def update(j, s_buf, diagonal):
        k0 = pl.multiple_of(j * SLC_TK, SLC_TK)
        vj = v_ref[pl.ds(k0, SLC_TK), :]
        for r in range(rep):
            rows = slice(r * tq, (r + 1) * tq)
            s = s_buf[rows]
            if diagonal:
                s = jnp.where(on_or_below_diag, s, NEG)
            m_prev = m_scr[rows]
            m_new = jnp.maximum(m_prev, jnp.max(s, axis=-1, keepdims=True))
            alpha = jnp.exp2(m_prev - m_new)
            p = jnp.exp2(s - jnp.concatenate([m_new] * (SLC_TK // LANES), axis=1))
            acc_scr[rows] = alpha * acc_scr[rows] + _dot(p.astype(BF16), vj)
            m_scr[rows] = m_new

    def tile_pair(jj, carry):
        a = 2 * jj
        scores(a + 1, s_b)
        update(a, s_a, False)
        scores(a + 2, s_a)
        update(a + 1, s_b, False)
        return carry

    scores(0, s_a)
    lax.fori_loop(0, i // 2, tile_pair, 0)

    @pl.when(i % 2 == 0)
    def _():
        update(i, s_a, True)

    @pl.when(i % 2 == 1)
    def _():
        scores(i, s_b)
        update(i - 1, s_a, False)
        update(i, s_b, True)

    for r in range(rep):
        acc = acc_scr[r * tq:(r + 1) * tq]
        o_ref[:, r * HEAD_DIM:(r + 1) * HEAD_DIM] = (
            acc[:, :HEAD_DIM] / acc[:, HEAD_DIM:HEAD_DIM + 1]).astype(BF16)


def _slc_attn(nat_b, selb, kt_aug, v_slc):
    B, S, _ = nat_b.shape
    G = B_KV
    rep = B_HEADS // B_KV
    ns = selb.shape[2] // G
    tq = SLC_TQ
    gw = rep * HEAD_DIM
    return pl.pallas_call(
        _slc_attn_kernel, grid=(B, G, S // tq),
        in_specs=[
            pl.BlockSpec((None, tq, gw), lambda b, g, i: (b, i, g)),
            pl.BlockSpec((None, tq, ns), lambda b, g, i: (b, i, g)),
            pl.BlockSpec((None, None, HEAD_DIM + ns, S), lambda b, g, i: (b, g, 0, 0)),
            pl.BlockSpec((None, None, S, LANES), lambda b, g, i: (b, g, 0, 0)),
        ],
        out_specs=pl.BlockSpec((None, tq, gw), lambda b, g, i: (b, i, g)),
        out_shape=jax.ShapeDtypeStruct((B, S, B_Q), BF16),
        scratch_shapes=[pltpu.VMEM((rep * tq, HEAD_DIM + ns), BF16),
                        pltpu.VMEM((rep * tq, LANES), F32), pltpu.VMEM((rep * tq, LANES), F32),
                        pltpu.VMEM((rep * tq, SLC_TK), F32), pltpu.VMEM((rep * tq, SLC_TK), F32)],
        compiler_params=_cparams(3), name="slc_attn",
    )(nat_b, selb, kt_aug, v_slc)


SUB = 128


def _band_kernel(q_ref, kp_ref, kc_ref, vp_ref, vc_ref, o_ref, *lse_refs, hq, hk, prev, max_off):
    i = pl.program_id(1)
    tq = q_ref.shape[0]
    rep = hq // hk
    width = prev + SUB
    r_i = lax.broadcasted_iota(jnp.int32, (SUB, width), 0)
    c_i = lax.broadcasted_iota(jnp.int32, (SUB, width), 1)
    diff = r_i - c_i + prev
    in_band = (diff >= 0) & (diff <= max_off)

    def window(p_ref, c_ref, sub, g):
        cols = slice(g * HEAD_DIM, (g + 1) * HEAD_DIM)
        lo = sub * SUB
        parts = []
        if lo < prev:
            parts.append(p_ref[lo:prev, cols])
        parts.append(c_ref[max(0, lo - prev):lo + SUB, cols])
        return jnp.concatenate(parts, axis=0) if len(parts) > 1 else parts[0]

    for sub in range(tq // SUB):
        k_start = i * tq + sub * SUB - prev
        bias = jnp.where(in_band & (c_i + k_start >= 0), 0.0, NEG)
        for g in range(hk):
            kw = window(kp_ref, kc_ref, sub, g)
            vw = window(vp_ref, vc_ref, sub, g)
            for r in range(rep):
                h = g * rep + r
                cols = slice(h * HEAD_DIM, (h + 1) * HEAD_DIM)
                qh = q_ref[sub * SUB:(sub + 1) * SUB, cols]
                s = _dot_nt(qh, kw) + bias
                m = jnp.max(s, axis=-1, keepdims=True)
                p = jnp.exp2(s - m)
                den = jnp.sum(p, axis=-1, keepdims=True)
                o = _dot(p.astype(BF16), vw) / den
                o_ref[sub * SUB:(sub + 1) * SUB, cols] = o.astype(o_ref.dtype)
                if lse_refs:
                    lse = m + jnp.log2(den)
                    lse_refs[0][sub * SUB:(sub + 1) * SUB, cols] = jnp.broadcast_to(
                        lse, (SUB, HEAD_DIM))


def _band_attn(q_arr, k_arr, v_arr, *, q_blk, k_blk, v_blk, hq, hk, max_off, tq, with_lse):
    N, L, _ = q_arr.shape
    prev = -(-max_off // SUB) * SUB
    m = tq // prev
    qw, kw = hq * HEAD_DIM, hk * HEAD_DIM
    cur = lambda blk: (lambda n, i: (n, i, blk))
    prv = lambda blk: (lambda n, i: (n, jnp.maximum(i * m - 1, 0), blk))
    out_shape = [jax.ShapeDtypeStruct((N, L, qw), BF16)]
    out_specs = [pl.BlockSpec((None, tq, qw), cur(0))]
    if with_lse:
        out_shape.append(jax.ShapeDtypeStruct((N, L, qw), F32))
        out_specs.append(pl.BlockSpec((None, tq, qw), cur(0)))
    return pl.pallas_call(
        functools.partial(_band_kernel, hq=hq, hk=hk, prev=prev, max_off=max_off),
        grid=(N, L // tq),
        in_specs=[
            pl.BlockSpec((None, tq, qw), cur(q_blk)),
            pl.BlockSpec((None, prev, kw), prv(k_blk)),
            pl.BlockSpec((None, tq, kw), cur(k_blk)),
            pl.BlockSpec((None, prev, kw), prv(v_blk)),
            pl.BlockSpec((None, tq, kw), cur(v_blk)),
        ],
        out_specs=tuple(out_specs), out_shape=tuple(out_shape),
        compiler_params=_cparams(2), name="band_attn",
    )(q_arr, k_arr, k_arr, v_arr, v_arr)


def _merge_kernel(x_ref, o0_ref, l0_ref, o1_ref, l1_ref, o2_ref, l2_ref, oc_ref, os_ref, ow_ref,
                  gate_ref, ex_ref, wa_ref, wb_ref, wo_ref, out_ref, s_o1, s_l1, s_o2, s_l2):
    T = x_ref.shape[0]
    for d, o_ref, l_ref, s_o, s_l in ((4, o1_ref, l1_ref, s_o1, s_l1), (16, o2_ref, l2_ref, s_o2, s_l2)):
        rows = T // d
        for r in range(d):
            o_r = o_ref[r].astype(F32)
            l_r = l_ref[r]
            for k in range(s_o.shape[0]):
                lanes = slice(k * LANES, (k + 1) * LANES)
                s_o[k, pl.ds(r, rows, stride=d), :] = o_r[:, lanes]
                s_l[k, pl.ds(r, rows, stride=d), :] = l_r[:, lanes]
    gather = lambda s: jnp.concatenate([s[k] for k in range(s.shape[0])], axis=1)
    l0, l1, l2 = l0_ref[...], gather(s_l1), gather(s_l2)
    mx = jnp.maximum(jnp.maximum(l0, l1), l2)
    e0, e1, e2 = jnp.exp2(l0 - mx), jnp.exp2(l1 - mx), jnp.exp2(l2 - mx)
    y_a = (e0 * o0_ref[...].astype(F32) + e1 * gather(s_o1) + e2 * gather(s_o2)) / (e0 + e1 + e2)

    gb = gate_ref[:, MERGE_GATE:]
    ex = ex_ref[...]
    gbx = sum(_dot(piece, ex) for piece in _split_bf16(gb, 2))
    y_b = (gbx[:, :B_Q] * oc_ref[...].astype(F32)
           + gbx[:, B_Q:2 * B_Q] * os_ref[...].astype(F32)
           + gbx[:, 2 * B_Q:] * ow_ref[...].astype(F32))
    merged = (gate_ref[:, :D_MODEL] * _dot(y_a.astype(BF16), wa_ref[...])
              + gate_ref[:, D_MODEL:MERGE_GATE] * _dot(y_b.astype(BF16), wb_ref[...]))
    out_ref[...] = x_ref[...] + _dot(merged.astype(BF16), wo_ref[...])


def _merge(x, o0, l0, o1, l1, o2, l2, o_cmp, o_slc, o_win, gates, ex, wa, wb, wo):
    B, S, D = x.shape
    T = ROW_TILE
    qw = A_SLOTS * HEAD_DIM
    nat = lambda w: pl.BlockSpec((None, T, w), lambda b, i: (b, i, 0))
    dil = lambda d: pl.BlockSpec((d, T // d, qw), lambda b, i: (b, i, 0))
    full = lambda a: pl.BlockSpec(a.shape, lambda b, i: (0,) * a.ndim)
    return pl.pallas_call(
        _merge_kernel, grid=(B, S // T),
        in_specs=[nat(D), nat(qw), nat(qw), dil(4), dil(4), dil(16), dil(16), nat(B_Q), nat(B_Q),
                  nat(B_Q), nat(G_COLS), full(ex), full(wa), full(wb), full(wo)],
        out_specs=nat(D), out_shape=jax.ShapeDtypeStruct((B, S, D), F32),
        scratch_shapes=[pltpu.VMEM((qw // LANES, T, LANES), F32)] * 4,
        compiler_params=_cparams(2), name="merge",
    )(x, o0, l0, o1, l1, o2, l2, o_cmp, o_slc, o_win, gates, ex, wa, wb, wo)


FF_CHUNK = 512


def _ffn_kernel(x_ref, g_ref, wu_ref, wd_ref, gf_ref, out_ref, *, final_norm):
    x = x_ref[...]
    u = _rms(x, g_ref[...]).astype(BF16)
    acc = x
    for c in range(D_FF // FF_CHUNK):
        cols = slice(c * FF_CHUNK, (c + 1) * FF_CHUNK)
        h = jnp.maximum(_dot(u, wu_ref[:, cols]), 0.0)
        acc = acc + _dot((h * h).astype(BF16), wd_ref[cols, :])
    out_ref[...] = _rms(acc, gf_ref[...]) if final_norm else acc


def _ffn(x, g_mlp, w_up, w_down, g_final, final_norm):
    B, S, D = x.shape
    T = ROW_TILE
    row = pl.BlockSpec((None, T, D), lambda b, i: (b, i, 0))
    vec = pl.BlockSpec((1, D), lambda b, i: (0, 0))
    return pl.pallas_call(
        functools.partial(_ffn_kernel, final_norm=final_norm), grid=(B, S // T),
        in_specs=[row, vec,
                  pl.BlockSpec((D, D_FF), lambda b, i: (0, 0), pipeline_mode=pl.Buffered(1)),
                  pl.BlockSpec((D_FF, D), lambda b, i: (0, 0), pipeline_mode=pl.Buffered(1)),
                  vec],
        out_specs=row, out_shape=jax.ShapeDtypeStruct((B, S, D), F32),
        compiler_params=_cparams(2), name="ffn",
    )(x, g_mlp, w_up, w_down, g_final)


def _rope_tables(seq):
    pos = jnp.arange(seq, dtype=F32)
    inv = ROPE_THETA ** (-jnp.arange(0, ROT_DIM, 2, dtype=F32) / ROT_DIM)
    ang = pos[:, None] * inv[None, :]
    cos, sin = jnp.cos(ang), jnp.sin(ang)
    half = ROT_DIM // 2
    z = lambda w: jnp.zeros((seq, w), F32)
    per_head = (
        jnp.concatenate([cos, cos, jnp.ones((seq, HEAD_DIM - ROT_DIM), F32)], axis=1),
        jnp.concatenate([-sin, z(HEAD_DIM - half)], axis=1),
        jnp.concatenate([z(half), sin, z(HEAD_DIM - ROT_DIM)], axis=1),
    )
    tab = jnp.stack([jnp.tile(t, (1, LANES // HEAD_DIM)) for t in per_head])
    dil = lambda d: tab.reshape(3, seq // d, d, LANES).transpose(0, 2, 1, 3)
    return tab, dil(4), dil(16)


def _layer_weights(w_in):
    hw = A_SLOTS * HEAD_DIM
    na = A_HEADS * HEAD_DIM
    groups = []
    for g in range(len(DIL_PAIRS)):
        groups += [w_in[:, t * na + g * hw: t * na + (g + 1) * hw] for t in range(3)]
    o3 = A_QKV + B_Q + B_KV_COLS
    o4 = o3 + B_GATE
    groups.append(w_in[:, A_QKV:o3])
    groups.append(w_in[:, o4:])
    groups.append(jnp.pad(w_in[:, o3:o4], ((0, 0), (0, G_PAD - B_GATE))))
    return jnp.concatenate(groups, axis=1).astype(BF16)


def _overlap_matrix(nc_pad, ns):
    c_start = jnp.arange(nc_pad) * CMP_STRIDE
    s_start = jnp.arange(ns) * SEL_LEN
    ov = (c_start[:, None] < s_start[None, :] + SEL_LEN) & (c_start[:, None] + CMP_LEN > s_start[None, :])
    return ov.astype(BF16)


def _gate_expand():
    rows = jnp.arange(G_PAD)[:, None]
    cols = jnp.arange(3 * B_Q)[None, :]
    br, h = cols // B_Q, (cols % B_Q) // HEAD_DIM
    return (rows == h * 3 + br).astype(BF16)


def _layer(x, g_mix, w_in, cmp_pos_k, cmp_w1_k, cmp_w2_k, cmp_pos_v, cmp_w1_v, cmp_w2_v,
           w_branch_a, w_branch_b, w_out, tabs):
    B, S, D = x.shape
    G = B_KV
    a0, a1, a2, nat_b, gates = _in_proj(x, g_mix.reshape(1, D), _layer_weights(w_in), *tabs)

    a_out = []
    for arr, (w, d) in zip((a0, a1, a2), DIL_PAIRS):
        a_out.append(_band_attn(arr, arr, arr, q_blk=0, k_blk=1, v_blk=2, hq=A_SLOTS, hk=A_SLOTS,
                                max_off=w // d, tq=256, with_lse=True))
    (o0, l0), (o1, l1), (o2, l2) = a_out

    nch = S // CMP_STRIDE
    kv = nat_b[:, :, B_Q:].reshape(B, S, 6, G, HEAD_DIM)
    chunks = jnp.stack([kv[:, :, 0], kv[:, :, 1]]).transpose(0, 1, 3, 2, 4).reshape(
        2, B, G, nch, CMP_STRIDE * HEAD_DIM)
    pos = jnp.stack([cmp_pos_k, cmp_pos_v]).reshape(2, 1, CMP_LEN * HEAD_DIM)
    pos = jnp.broadcast_to(pos, (2, 8, CMP_LEN * HEAD_DIM)).astype(BF16)
    kcvc = _compress(chunks, pos, jnp.stack([cmp_w1_k, cmp_w1_v]).astype(BF16),
                     jnp.stack([cmp_w2_k, cmp_w2_v]).astype(BF16))
    ns = S // SEL_LEN
    o_cmp, selb = _cmp_attn(nat_b, kcvc[0], kcvc[1].transpose(0, 1, 3, 2),
                            _overlap_matrix(nch, ns).T)

    k_slc_t = kv[:, :, 2].transpose(0, 2, 3, 1)
    onehot = (jnp.arange(ns)[:, None] == (jnp.arange(S) // SEL_LEN)[None, :]).astype(BF16)
    kt_aug = jnp.concatenate([k_slc_t, jnp.broadcast_to(onehot, (B, G, ns, S))], axis=2)
    v_slc = kv[:, :, 3].transpose(0, 2, 1, 3)
    ones_col = (jnp.arange(LANES - HEAD_DIM) == 0).astype(BF16)
    v_aug = jnp.concatenate(
        [v_slc, jnp.broadcast_to(ones_col, (B, G, S, LANES - HEAD_DIM))], axis=3)
    o_slc = _slc_attn(nat_b, selb, kt_aug, v_aug)

    kw_blk = (B_Q + 4 * G * HEAD_DIM) // (G * HEAD_DIM)
    (o_win,) = _band_attn(nat_b, nat_b, nat_b, q_blk=0, k_blk=kw_blk, v_blk=kw_blk + 1,
                          hq=B_HEADS, hk=B_KV, max_off=WIN_LEN - 1, tq=512, with_lse=False)

    return _merge(x, o0, l0, o1, l1, o2, l2, o_cmp, o_slc, o_win, gates, _gate_expand(),
                  w_branch_a.astype(BF16), w_branch_b.astype(BF16), w_out.astype(BF16))


def kernel(x, norm_mix_g, w_in, cmp_pos_k, cmp_w1_k, cmp_w2_k, cmp_pos_v, cmp_w1_v, cmp_w2_v,
           w_branch_a, w_branch_b, w_out, norm_mlp_g, w_up, w_down, norm_final_g):
    B, S, D = x.shape
    depth = w_in.shape[0]
    tabs = _rope_tables(S)
    for l in range(depth):
        x1 = _layer(x, norm_mix_g[l], w_in[l], cmp_pos_k[l], cmp_w1_k[l], cmp_w2_k[l],
                    cmp_pos_v[l], cmp_w1_v[l], cmp_w2_v[l], w_branch_a[l], w_branch_b[l],
                    w_out[l], tabs)
        x = _ffn(x1, norm_mlp_g[l].reshape(1, D), w_up[l].astype(BF16), w_down[l].astype(BF16),
                 norm_final_g.reshape(1, D), final_norm=(l == depth - 1))
    return x
```

```python
import functools

import jax
import jax.numpy as jnp
from jax import lax
from jax.experimental import pallas as pl
from jax.experimental.pallas import tpu as pltpu

F32 = jnp.float32
BF16 = jnp.bfloat16

D_MODEL = 1024
HEAD_DIM = 64
ROT_DIM = HEAD_DIM // 4
ROPE_THETA = 500000.0
EPS = 1e-6
DIL_PAIRS = ((128, 1), (512, 4), (2048, 16))
A_SLOTS = 8
A_HEADS = A_SLOTS * len(DIL_PAIRS)
B_HEADS = 8
B_KV = 2
CMP_LEN = 32
CMP_STRIDE = 16
CMP_HIDDEN = 4 * HEAD_DIM
SEL_LEN = 64
SEL_TOP = 16
WIN_LEN = 512
FORCE = 1e4
D_FF = 4 * D_MODEL
A_QKV = 3 * A_HEADS * HEAD_DIM
B_Q = B_HEADS * HEAD_DIM
B_KV_COLS = 6 * B_KV * HEAD_DIM
B_GATE = 3 * B_HEADS
MERGE_GATE = 2 * D_MODEL

LANES = 128
NEG = -1e30
QSCALE = HEAD_DIM ** -0.5 * 1.4426950408889634

A_COLS = 3 * A_SLOTS * HEAD_DIM
B_COLS = B_Q + B_KV_COLS
PROJ_CHUNK = 256
G_PAD = PROJ_CHUNK
G_COLS = MERGE_GATE + G_PAD
W_COLS = 3 * A_COLS + B_COLS + G_COLS

ROW_TILE = 512
VMEM_LIMIT = 56 * 1024 * 1024


def _cparams(n_axes, vmem=VMEM_LIMIT):
    return pltpu.CompilerParams(dimension_semantics=("arbitrary",) * n_axes, vmem_limit_bytes=vmem)


def _rms(x, g):
    return x * lax.rsqrt(jnp.mean(x * x, axis=-1, keepdims=True) + EPS) * g


def _dot(a, b):
    return jnp.dot(a, b, preferred_element_type=F32)


def _dot_nt(a, b):
    return lax.dot_general(a, b, (((1,), (1,)), ((), ())), preferred_element_type=F32)


def _split_bf16(x, parts):
    out = []
    for _ in range(parts - 1):
        hi = x.astype(BF16)
        out.append(hi)
        x = x - hi.astype(F32)
    out.append(x.astype(BF16))
    return out


_B_ROT = (True,) * (B_Q // LANES) + (True, False, True, False, True, False)


def _rotary(x, cos_t, sin_a, sin_b):
    half = ROT_DIM // 2
    return x * cos_t + pltpu.roll(x, LANES - half, 1) * sin_a + pltpu.roll(x, half, 1) * sin_b


def _in_proj_kernel(x_ref, g_ref, w_ref, t0_ref, t1_ref, t2_ref,
                    a0_ref, a1_ref, a2_ref, b_ref, cmp_ref, gate_ref, u_scr):
    T = x_ref.shape[0]
    u = _rms(x_ref[...], g_ref[...])
    nslab = u_scr.shape[0]
    for k in range(nslab):
        u_scr[k] = u[:, k * LANES:(k + 1) * LANES]
    u_nat = u.astype(BF16)

    def dilated(d):
        rows = T // d
        return jnp.concatenate(
            [jnp.concatenate([u_scr[k, pl.ds(r, rows, stride=d), :] for k in range(nslab)], axis=1)
             for r in range(d)], axis=0).astype(BF16)

    q_slabs = A_SLOTS * HEAD_DIM // LANES

    def project(lhs, col0, ncols, rot, tables, store, n_q=0):
        per = PROJ_CHUNK // LANES
        for c in range(ncols // PROJ_CHUNK):
            res = _dot(lhs, w_ref[:, col0 + c * PROJ_CHUNK: col0 + (c + 1) * PROJ_CHUNK])
            for k in range(per):
                slab = res[:, k * LANES:(k + 1) * LANES]
                if rot[c * per + k]:
                    slab = _rotary(slab, tables[0], tables[1], tables[2])
                if c * per + k < n_q:
                    slab = slab * QSCALE
                store((c * per + k) * LANES, slab)

    a_rot = (True,) * (2 * q_slabs) + (False,) * q_slabs

    tab0 = (t0_ref[0], t0_ref[1], t0_ref[2])

    def store_a0(c0, res):
        a0_ref[:, c0:c0 + LANES] = res.astype(BF16)

    project(u_nat, 0, A_COLS, a_rot, tab0, store_a0, n_q=q_slabs)

    for d, t_ref, a_ref, col0 in ((4, t1_ref, a1_ref, A_COLS), (16, t2_ref, a2_ref, 2 * A_COLS)):
        rows = T // d
        tabs = tuple(jnp.concatenate([t_ref[k, r] for r in range(d)], axis=0) for k in range(3))

        def store_ad(c0, res, a_ref=a_ref, d=d, rows=rows):
            for r in range(d):
                a_ref[r, :, c0:c0 + LANES] = res[r * rows:(r + 1) * rows].astype(BF16)

        project(dilated(d), col0, A_COLS, a_rot, tabs, store_ad, n_q=q_slabs)

    def store_b(c0, res):
        b_ref[:, c0:c0 + LANES] = res.astype(BF16)
        if B_Q <= c0 < B_Q + 2 * LANES:
            cmp_ref[:, c0 - B_Q:c0 - B_Q + LANES] = res

    project(u_nat, 3 * A_COLS, B_COLS, _B_ROT, tab0, store_b, n_q=q_slabs)

    def store_g(c0, res):
        gate_ref[:, c0:c0 + LANES] = jax.nn.sigmoid(res)

    project(u_nat, 3 * A_COLS + B_COLS, G_COLS, (False,) * (G_COLS // LANES), tab0, store_g)


def _in_proj(x, g, w_all, tab0, tab1, tab2):
    B, S, D = x.shape
    T = ROW_TILE
    nt = S // T
    out_shape = (
        jax.ShapeDtypeStruct((B, S, A_COLS), BF16),
        jax.ShapeDtypeStruct((B * 4, S // 4, A_COLS), BF16),
        jax.ShapeDtypeStruct((B * 16, S // 16, A_COLS), BF16),
        jax.ShapeDtypeStruct((B, S, B_COLS), BF16),
        jax.ShapeDtypeStruct((B, S, 2 * LANES), F32),
        jax.ShapeDtypeStruct((B, S, G_COLS), F32),
    )
    in_specs = [
        pl.BlockSpec((None, T, D), lambda b, i: (b, i, 0)),
        pl.BlockSpec((1, D), lambda b, i: (0, 0)),
        pl.BlockSpec((D, W_COLS), lambda b, i: (0, 0), pipeline_mode=pl.Buffered(1)),
        pl.BlockSpec((3, T, LANES), lambda b, i: (0, i, 0)),
        pl.BlockSpec((3, 4, T // 4, LANES), lambda b, i: (0, 0, i, 0)),
        pl.BlockSpec((3, 16, T // 16, LANES), lambda b, i: (0, 0, i, 0)),
    ]
    out_specs = (
        pl.BlockSpec((None, T, A_COLS), lambda b, i: (b, i, 0)),
        pl.BlockSpec((4, T // 4, A_COLS), lambda b, i: (b, i, 0)),
        pl.BlockSpec((16, T // 16, A_COLS), lambda b, i: (b, i, 0)),
        pl.BlockSpec((None, T, B_COLS), lambda b, i: (b, i, 0)),
        pl.BlockSpec((None, T, 2 * LANES), lambda b, i: (b, i, 0)),
        pl.BlockSpec((None, T, G_COLS), lambda b, i: (b, i, 0)),
    )
    return pl.pallas_call(
        _in_proj_kernel, grid=(B, nt), in_specs=in_specs, out_specs=out_specs, out_shape=out_shape,
        scratch_shapes=[pltpu.VMEM((D // LANES, T, LANES), F32)], compiler_params=_cparams(2),
        name="in_proj",
    )(x, g, w_all, tab0, tab1, tab2)


def _gelu_tanh(x):
    return 0.5 * x * (1.0 + jnp.tanh(0.7978845608028654 * (x + 0.044715 * (x * x * x))))


def _compress_kernel(x_ref, pos_ref, w1_ref, w2_ref, o_ref):
    nch = x_ref.shape[0] // CMP_STRIDE
    half = CMP_STRIDE * LANES
    chunks = jnp.concatenate(
        [x_ref[pl.ds(j, nch, stride=CMP_STRIDE), :] for j in range(CMP_STRIDE)], axis=1).astype(BF16)
    lane_group = (lax.broadcasted_iota(jnp.int32, (1, half), 1) % LANES) // HEAD_DIM
    pb = _dot(pos_ref[...], w1_ref[...])[0:1]
    for g in range(B_KV):
        c = jnp.where(lane_group == g, chunks, jnp.zeros((), BF16))
        p1 = _dot(c, w1_ref[:half, :])
        p2 = _dot(c, w1_ref[half:, :])
        p2_next = pltpu.roll(p2, nch - 1, 0)
        h = _gelu_tanh(p1 + p2_next + pb)
        o_ref[g] = _dot(h.astype(BF16), w2_ref[...]).astype(BF16)


def _compress(cmp_in, pos, w1, w2):
    B, S, _ = cmp_in.shape
    nch = S // CMP_STRIDE
    kx = 2 * CMP_STRIDE * LANES
    return pl.pallas_call(
        _compress_kernel, grid=(2, B),
        in_specs=[
            pl.BlockSpec((None, S, LANES), lambda t, b: (b, 0, t)),
            pl.BlockSpec((None, 8, kx), lambda t, b: (t, 0, 0)),
            pl.BlockSpec((None, kx, CMP_HIDDEN), lambda t, b: (t, 0, 0)),
            pl.BlockSpec((None, CMP_HIDDEN, HEAD_DIM), lambda t, b: (t, 0, 0)),
        ],
        out_specs=pl.BlockSpec((None, None, B_KV, nch, HEAD_DIM), lambda t, b: (t, b, 0, 0, 0)),
        out_shape=jax.ShapeDtypeStruct((2, B, B_KV, nch, HEAD_DIM), BF16),
        compiler_params=_cparams(2), name="compress",
    )(cmp_in, pos, w1, w2)


CMP_TQ = 256


def _cmp_attn_kernel(q_ref, kc_ref, vct_ref, ovt_ref, o_ref, sel_ref):
    i = pl.program_id(1)
    tq = q_ref.shape[0]
    nc = kc_ref.shape[1]
    ns = ovt_ref.shape[0]
    rep = B_HEADS // B_KV
    t = i * tq + lax.broadcasted_iota(jnp.int32, (1, tq), 1)
    n = lax.broadcasted_iota(jnp.int32, (nc, 1), 0)
    cmask = (n * CMP_STRIDE + (CMP_LEN - 1)) <= t
    blk = lax.broadcasted_iota(jnp.int32, (ns, 1), 0)
    blk_f = blk.astype(F32)
    cur = jnp.right_shift(t, SEL_LEN.bit_length() - 1)
    forced = (blk == 0) | (blk == cur) | (blk == cur - 1)
    future = blk > cur
    o_t = []
    for g in range(B_KV):
        kc = kc_ref[g]
        vct = vct_ref[g]
        psum = jnp.zeros((nc, tq), F32)
        for r in range(rep):
            h = g * rep + r
            qh = q_ref[:, h * HEAD_DIM:(h + 1) * HEAD_DIM]
            s = jnp.where(cmask, _dot_nt(kc, qh), NEG)
            p = jnp.where(cmask, jnp.exp2(s - jnp.max(s, axis=0, keepdims=True)), 0.0)
            p = p / jnp.maximum(jnp.sum(p, axis=0, keepdims=True), 1e-30)
            o_t.append(_dot(vct, p.astype(BF16)))
            psum = psum + p
        ovt = ovt_ref[...]
        imp = sum(_dot(ovt, piece) for piece in _split_bf16(psum, 3))
        imp = jnp.where(forced, FORCE, imp)
        imp = jnp.where(future, -FORCE, imp)
        bias = jnp.full((ns, tq), NEG, F32)
        for _ in range(SEL_TOP):
            mx = jnp.max(imp, axis=0, keepdims=True)
            first = jnp.min(jnp.where(imp == mx, blk_f, float(ns)), axis=0, keepdims=True)
            hit = blk_f == first
            bias = jnp.where(hit, 0.0, bias)
            imp = jnp.where(hit, -3e38, imp)
        sel_ref[:, g * ns:(g + 1) * ns] = jnp.transpose(bias).astype(BF16)
    o_ref[...] = jnp.transpose(jnp.concatenate(o_t, axis=0)).astype(BF16)


def _cmp_attn(nat_b, kc, vct, overlap_t):
    B, S, _ = nat_b.shape
    nc = kc.shape[2]
    ns = overlap_t.shape[0]
    tq = CMP_TQ
    return pl.pallas_call(
        _cmp_attn_kernel, grid=(B, S // tq),
        in_specs=[
            pl.BlockSpec((None, tq, B_Q), lambda b, i: (b, i, 0)),
            pl.BlockSpec((None, B_KV, nc, HEAD_DIM), lambda b, i: (b, 0, 0, 0)),
            pl.BlockSpec((None, B_KV, HEAD_DIM, nc), lambda b, i: (b, 0, 0, 0)),
            pl.BlockSpec((ns, nc), lambda b, i: (0, 0)),
        ],
        out_specs=(
            pl.BlockSpec((None, tq, B_Q), lambda b, i: (b, i, 0)),
            pl.BlockSpec((None, tq, B_KV * ns), lambda b, i: (b, i, 0)),
        ),
        out_shape=(
            jax.ShapeDtypeStruct((B, S, B_Q), BF16),
            jax.ShapeDtypeStruct((B, S, B_KV * ns), BF16),
        ),
        compiler_params=_cparams(2), name="cmp_attn",
    )(nat_b, kc, vct, overlap_t)


SLC_TQ = 256
SLC_TK = 256


def _slc_attn_kernel(q_ref, sel_ref, kt_ref, v_ref, o_ref, qa_scr, m_scr, acc_scr, s_a, s_b):
    i = pl.program_id(2)
    tq = q_ref.shape[0]
    rep = B_HEADS // B_KV
    sel = sel_ref[...]
    for r in range(rep):
        qa_scr[r * tq:(r + 1) * tq] = jnp.concatenate(
            [q_ref[:, r * HEAD_DIM:(r + 1) * HEAD_DIM], sel], axis=1)
    m_scr[...] = jnp.full(m_scr.shape, NEG, F32)
    acc_scr[...] = jnp.zeros(acc_scr.shape, F32)
    row = lax.broadcasted_iota(jnp.int32, (tq, SLC_TK), 0)
    col = lax.broadcasted_iota(jnp.int32, (tq, SLC_TK), 1)
    on_or_below_diag = row >= col

    def scores(j, s_buf):
        k0 = pl.multiple_of(j * SLC_TK, SLC_TK)
        s_buf[...] = _dot(qa_scr[...], kt_ref[:, pl.ds(k0, SLC_TK)])

    def update(j, s_buf, diagonal):
        k0 = pl.multiple_of(j * SLC_TK, SLC_TK)
        vj = v_ref[pl.ds(k0, SLC_TK), :]
        for r in range(rep):
            rows = slice(r * tq, (r + 1) * tq)
            s = s_buf[rows]
            if diagonal:
                s = jnp.where(on_or_below_diag, s, NEG)
            m_prev = m_scr[rows]
            m_new = jnp.maximum(m_prev, jnp.max(s, axis=-1, keepdims=True))
            alpha = jnp.exp2(m_prev - m_new)
            p = jnp.exp2(s - jnp.concatenate([m_new] * (SLC_TK // LANES), axis=1))
            acc_scr[rows] = alpha * acc_scr[rows] + _dot(p.astype(BF16), vj)
            m_scr[rows] = m_new

    def tile_pair(jj, carry):
        a = 2 * jj
        scores(a + 1, s_b)
        update(a, s_a, False)
        scores(a + 2, s_a)
        update(a + 1, s_b, False)
        return carry

    scores(0, s_a)
    lax.fori_loop(0, i // 2, tile_pair, 0)

    @pl.when(i % 2 == 0)
    def _():
        update(i, s_a, True)

    @pl.when(i % 2 == 1)
    def _():
        scores(i, s_b)
        update(i - 1, s_a, False)
        update(i, s_b, True)

    for r in range(rep):
        acc = acc_scr[r * tq:(r + 1) * tq]
        o_ref[:, r * HEAD_DIM:(r + 1) * HEAD_DIM] = (
            acc[:, :HEAD_DIM] / acc[:, HEAD_DIM:HEAD_DIM + 1]).astype(BF16)


def _slc_attn(nat_b, selb, kt_aug, v_slc):
    B, S, _ = nat_b.shape
    G = B_KV
    rep = B_HEADS // B_KV
    ns = selb.shape[2] // G
    tq = SLC_TQ
    gw = rep * HEAD_DIM
    return pl.pallas_call(
        _slc_attn_kernel, grid=(B, G, S // tq),
        in_specs=[
            pl.BlockSpec((None, tq, gw), lambda b, g, i: (b, i, g)),
            pl.BlockSpec((None, tq, ns), lambda b, g, i: (b, i, g)),
            pl.BlockSpec((None, None, HEAD_DIM + ns, S), lambda b, g, i: (b, g, 0, 0)),
            pl.BlockSpec((None, None, S, LANES), lambda b, g, i: (b, g, 0, 0)),
        ],
        out_specs=pl.BlockSpec((None, tq, gw), lambda b, g, i: (b, i, g)),
        out_shape=jax.ShapeDtypeStruct((B, S, B_Q), BF16),
        scratch_shapes=[pltpu.VMEM((rep * tq, HEAD_DIM + ns), BF16),
                        pltpu.VMEM((rep * tq, LANES), F32), pltpu.VMEM((rep * tq, LANES), F32),
                        pltpu.VMEM((rep * tq, SLC_TK), F32), pltpu.VMEM((rep * tq, SLC_TK), F32)],
        compiler_params=_cparams(3), name="slc_attn",
    )(nat_b, selb, kt_aug, v_slc)


SUB = 128


def _band_kernel(q_ref, kp_ref, kc_ref, vp_ref, vc_ref, o_ref, *lse_refs, shared_kv, prev, max_off):
    i = pl.program_id(1)
    tq = q_ref.shape[0]
    n_slab = q_ref.shape[1] // LANES
    width = prev + SUB
    r_i = lax.broadcasted_iota(jnp.int32, (SUB, width), 0)
    c_i = lax.broadcasted_iota(jnp.int32, (SUB, width), 1)
    diff = r_i - c_i + prev
    in_band = (diff >= 0) & (diff <= max_off)
    lo_half = lax.broadcasted_iota(jnp.int32, (1, LANES), 1) < HEAD_DIM
    half_mask = {"lo": lo_half, "hi": jnp.logical_not(lo_half)}

    def window(p_ref, c_ref, sub, slab):
        cols = slice(slab * LANES, (slab + 1) * LANES)
        lo = sub * SUB
        parts = []
        if lo < prev:
            parts.append(p_ref[lo:prev, cols])
        parts.append(c_ref[max(0, lo - prev):lo + SUB, cols])
        return jnp.concatenate(parts, axis=0) if len(parts) > 1 else parts[0]

    def swap_halves(x):
        return jnp.concatenate([x[:, HEAD_DIM:], x[:, :HEAD_DIM]], axis=1)

    for sub in range(tq // SUB):
        rows = slice(sub * SUB, (sub + 1) * SUB)
        k_start = i * tq + sub * SUB - prev
        bias = jnp.where(in_band & (c_i + k_start >= 0), 0.0, NEG)
        if shared_kv:
            k2 = window(kp_ref, kc_ref, sub, 0)
            v2 = window(vp_ref, vc_ref, sub, 0)
            groups = [
                (((0, "lo"), (1, "lo"), (2, "hi"), (3, "hi")), k2, v2),
                (((0, "hi"), (1, "hi"), (2, "lo"), (3, "lo")), swap_halves(k2), swap_halves(v2)),
            ]
        else:
            groups = [(((p, "lo"), (p, "hi")), window(kp_ref, kc_ref, sub, p),
                       window(vp_ref, vc_ref, sub, p)) for p in range(n_slab)]
        outs = {}
        for members, k2, v2 in groups:
            nm = len(members)
            zero = jnp.zeros((), BF16)
            lhs = jnp.concatenate(
                [jnp.where(half_mask[half], q_ref[rows, slab * LANES:(slab + 1) * LANES], zero)
                 for slab, half in members], axis=0)
            s = _dot_nt(lhs, k2)
            s = (s.reshape(nm, SUB, width) + bias[None]).reshape(nm * SUB, width)
            m = jnp.max(s, axis=-1, keepdims=True)
            p = jnp.exp2(s - m)
            den = jnp.sum(p, axis=-1, keepdims=True)
            pv = _dot(p.astype(BF16), v2) / den
            lse = m + jnp.log2(den)
            for idx, member in enumerate(members):
                outs[member] = (pv[idx * SUB:(idx + 1) * SUB], lse[idx * SUB:(idx + 1) * SUB])
        for slab in range(n_slab):
            cols = slice(slab * LANES, (slab + 1) * LANES)
            (o_lo, l_lo), (o_hi, l_hi) = outs[(slab, "lo")], outs[(slab, "hi")]
            o_ref[rows, cols] = jnp.where(lo_half, o_lo, o_hi).astype(o_ref.dtype)
            if lse_refs:
                lse_refs[0][rows, cols] = jnp.where(lo_half, l_lo, l_hi)


def _band_attn(q_arr, k_arr, v_arr, *, q_blk, k_blk, v_blk, hq, hk, max_off, tq, with_lse):
    N, L, _ = q_arr.shape
    assert hq == hk or (hq, hk) == (8, 2), (hq, hk)
    prev = -(-max_off // SUB) * SUB
    m = tq // prev
    qw, kw = hq * HEAD_DIM, hk * HEAD_DIM
    cur = lambda blk: (lambda n, i: (n, i, blk))
    prv = lambda blk: (lambda n, i: (n, jnp.maximum(i * m - 1, 0), blk))
    out_shape = [jax.ShapeDtypeStruct((N, L, qw), BF16)]
    out_specs = [pl.BlockSpec((None, tq, qw), cur(0))]
    if with_lse:
        out_shape.append(jax.ShapeDtypeStruct((N, L, qw), F32))
        out_specs.append(pl.BlockSpec((None, tq, qw), cur(0)))
    return pl.pallas_call(
        functools.partial(_band_kernel, shared_kv=hq != hk, prev=prev, max_off=max_off),
        grid=(N, L // tq),
        in_specs=[
            pl.BlockSpec((None, tq, qw), cur(q_blk)),
            pl.BlockSpec((None, prev, kw), prv(k_blk)),
            pl.BlockSpec((None, tq, kw), cur(k_blk)),
            pl.BlockSpec((None, prev, kw), prv(v_blk)),
            pl.BlockSpec((None, tq, kw), cur(v_blk)),
        ],
        out_specs=tuple(out_specs), out_shape=tuple(out_shape),
        compiler_params=_cparams(2), name="band_attn",
    )(q_arr, k_arr, k_arr, v_arr, v_arr)


def _merge_kernel(x_ref, o0_ref, l0_ref, o1_ref, l1_ref, o2_ref, l2_ref, oc_ref, os_ref, ow_ref,
                  gate_ref, ex_ref, wa_ref, wb_ref, wo_ref, out_ref, s_o1, s_l1, s_o2, s_l2):
    T = x_ref.shape[0]
    for d, o_ref, l_ref, s_o, s_l in ((4, o1_ref, l1_ref, s_o1, s_l1), (16, o2_ref, l2_ref, s_o2, s_l2)):
        rows = T // d
        for r in range(d):
            o_r = o_ref[r].astype(F32)
            l_r = l_ref[r]
            for k in range(s_o.shape[0]):
                lanes = slice(k * LANES, (k + 1) * LANES)
                s_o[k, pl.ds(r, rows, stride=d), :] = o_r[:, lanes]
                s_l[k, pl.ds(r, rows, stride=d), :] = l_r[:, lanes]
    gather = lambda s: jnp.concatenate([s[k] for k in range(s.shape[0])], axis=1)
    l0, l1, l2 = l0_ref[...], gather(s_l1), gather(s_l2)
    mx = jnp.maximum(jnp.maximum(l0, l1), l2)
    e0, e1, e2 = jnp.exp2(l0 - mx), jnp.exp2(l1 - mx), jnp.exp2(l2 - mx)
    y_a = (e0 * o0_ref[...].astype(F32) + e1 * gather(s_o1) + e2 * gather(s_o2)) / (e0 + e1 + e2)

    gb = gate_ref[:, MERGE_GATE:]
    ex = ex_ref[...]
    gbx = sum(_dot(piece, ex) for piece in _split_bf16(gb, 2))
    y_b = (gbx[:, :B_Q] * oc_ref[...].astype(F32)
           + gbx[:, B_Q:2 * B_Q] * os_ref[...].astype(F32)
           + gbx[:, 2 * B_Q:] * ow_ref[...].astype(F32))
    merged = (gate_ref[:, :D_MODEL] * _dot(y_a.astype(BF16), wa_ref[...])
              + gate_ref[:, D_MODEL:MERGE_GATE] * _dot(y_b.astype(BF16), wb_ref[...]))
    out_ref[...] = x_ref[...] + _dot(merged.astype(BF16), wo_ref[...])


def _merge(x, o0, l0, o1, l1, o2, l2, o_cmp, o_slc, o_win, gates, ex, wa, wb, wo):
    B, S, D = x.shape
    T = ROW_TILE
    qw = A_SLOTS * HEAD_DIM
    nat = lambda w: pl.BlockSpec((None, T, w), lambda b, i: (b, i, 0))
    dil = lambda d: pl.BlockSpec((d, T // d, qw), lambda b, i: (b, i, 0))
    full = lambda a: pl.BlockSpec(a.shape, lambda b, i: (0,) * a.ndim)
    return pl.pallas_call(
        _merge_kernel, grid=(B, S // T),
        in_specs=[nat(D), nat(qw), nat(qw), dil(4), dil(4), dil(16), dil(16), nat(B_Q), nat(B_Q),
                  nat(B_Q), nat(G_COLS), full(ex), full(wa), full(wb), full(wo)],
        out_specs=nat(D), out_shape=jax.ShapeDtypeStruct((B, S, D), F32),
        scratch_shapes=[pltpu.VMEM((qw // LANES, T, LANES), F32)] * 4,
        compiler_params=_cparams(2), name="merge",
    )(x, o0, l0, o1, l1, o2, l2, o_cmp, o_slc, o_win, gates, ex, wa, wb, wo)


FF_CHUNK = 512


def _ffn_kernel(x_ref, g_ref, wu_ref, wd_ref, gf_ref, out_ref, *, final_norm):
    x = x_ref[...]
    u = _rms(x, g_ref[...]).astype(BF16)
    acc = x
    for c in range(D_FF // FF_CHUNK):
        cols = slice(c * FF_CHUNK, (c + 1) * FF_CHUNK)
        h = jnp.maximum(_dot(u, wu_ref[:, cols]), 0.0)
        acc = acc + _dot((h * h).astype(BF16), wd_ref[cols, :])
    out_ref[...] = _rms(acc, gf_ref[...]) if final_norm else acc


def _ffn(x, g_mlp, w_up, w_down, g_final, final_norm):
    B, S, D = x.shape
    T = ROW_TILE
    row = pl.BlockSpec((None, T, D), lambda b, i: (b, i, 0))
    vec = pl.BlockSpec((1, D), lambda b, i: (0, 0))
    return pl.pallas_call(
        functools.partial(_ffn_kernel, final_norm=final_norm), grid=(B, S // T),
        in_specs=[row, vec,
                  pl.BlockSpec((D, D_FF), lambda b, i: (0, 0), pipeline_mode=pl.Buffered(1)),
                  pl.BlockSpec((D_FF, D), lambda b, i: (0, 0), pipeline_mode=pl.Buffered(1)),
                  vec],
        out_specs=row, out_shape=jax.ShapeDtypeStruct((B, S, D), F32),
        compiler_params=_cparams(2), name="ffn",
    )(x, g_mlp, w_up, w_down, g_final)


def _rope_table(seq, d):
    half = ROT_DIM // 2
    pos = (jnp.arange(seq // d, dtype=jnp.int32)[None, :] * d
           + jnp.arange(d, dtype=jnp.int32)[:, None]).astype(F32)[:, :, None]
    lane = jnp.arange(LANES, dtype=jnp.int32) % HEAD_DIM
    inv = ROPE_THETA ** (-(2 * (lane % half)).astype(F32) / ROT_DIM)
    ang = pos * inv[None, None, :]
    cos, sin = jnp.cos(ang), jnp.sin(ang)
    return jnp.stack([
        jnp.where(lane < ROT_DIM, cos, 1.0),
        jnp.where(lane < half, -sin, 0.0),
        jnp.where((lane >= half) & (lane < ROT_DIM), sin, 0.0),
    ])


def _rope_tables(seq):
    return _rope_table(seq, 1)[:, 0], _rope_table(seq, 4), _rope_table(seq, 16)


def _layer_weights(w_in):
    hw = A_SLOTS * HEAD_DIM
    na = A_HEADS * HEAD_DIM
    groups = []
    for g in range(len(DIL_PAIRS)):
        groups += [w_in[:, t * na + g * hw: t * na + (g + 1) * hw] for t in range(3)]
    o3 = A_QKV + B_Q + B_KV_COLS
    o4 = o3 + B_GATE
    groups.append(w_in[:, A_QKV:o3])
    groups.append(w_in[:, o4:])
    groups.append(jnp.pad(w_in[:, o3:o4], ((0, 0), (0, G_PAD - B_GATE))))
    return jnp.concatenate(groups, axis=1).astype(BF16)


def _overlap_matrix(nc_pad, ns):
    c_start = jnp.arange(nc_pad) * CMP_STRIDE
    s_start = jnp.arange(ns) * SEL_LEN
    ov = (c_start[:, None] < s_start[None, :] + SEL_LEN) & (c_start[:, None] + CMP_LEN > s_start[None, :])
    return ov.astype(BF16)


def _gate_expand():
    rows = jnp.arange(G_PAD)[:, None]
    cols = jnp.arange(3 * B_Q)[None, :]
    br, h = cols // B_Q, (cols % B_Q) // HEAD_DIM
    return (rows == h * 3 + br).astype(BF16)


def _layer(x, g_mix, w_in, cmp_pos_k, cmp_w1_k, cmp_w2_k, cmp_pos_v, cmp_w1_v, cmp_w2_v,
           w_branch_a, w_branch_b, w_out, tabs):
    B, S, D = x.shape
    G = B_KV
    a0, a1, a2, nat_b, cmp_in, gates = _in_proj(x, g_mix.reshape(1, D), _layer_weights(w_in), *tabs)

    a_out = []
    for arr, (w, d) in zip((a0, a1, a2), DIL_PAIRS):
        a_out.append(_band_attn(arr, arr, arr, q_blk=0, k_blk=1, v_blk=2, hq=A_SLOTS, hk=A_SLOTS,
                                max_off=w // d, tq=256, with_lse=True))
    (o0, l0), (o1, l1), (o2, l2) = a_out

    nch = S // CMP_STRIDE
    kv = nat_b[:, :, B_Q:].reshape(B, S, 6, G, HEAD_DIM)
    pos = jnp.stack([cmp_pos_k, cmp_pos_v]).astype(BF16)[:, :, None, :]
    pos = jnp.pad(pos, ((0, 0), (0, 0), (0, G - 1), (0, 0))).reshape(2, 1, CMP_LEN * G * HEAD_DIM)
    pos = jnp.broadcast_to(pos, (2, 8, CMP_LEN * G * HEAD_DIM))
    w1 = jnp.stack([cmp_w1_k, cmp_w1_v]).astype(BF16).reshape(2, CMP_LEN, 1, HEAD_DIM, CMP_HIDDEN)
    w1 = jnp.broadcast_to(w1, (2, CMP_LEN, G, HEAD_DIM, CMP_HIDDEN)).reshape(
        2, CMP_LEN * G * HEAD_DIM, CMP_HIDDEN)
    kcvc = _compress(cmp_in, pos, w1, jnp.stack([cmp_w2_k, cmp_w2_v]).astype(BF16))
    ns = S // SEL_LEN
    o_cmp, selb = _cmp_attn(nat_b, kcvc[0], kcvc[1].transpose(0, 1, 3, 2),
                            _overlap_matrix(nch, ns).T)

    k_slc_t = kv[:, :, 2].transpose(0, 2, 3, 1)
    onehot = (jnp.arange(ns)[:, None] == (jnp.arange(S) // SEL_LEN)[None, :]).astype(BF16)
    kt_aug = jnp.concatenate([k_slc_t, jnp.broadcast_to(onehot, (B, G, ns, S))], axis=2)
    v_slc = kv[:, :, 3].transpose(0, 2, 1, 3)
    ones_col = (jnp.arange(LANES - HEAD_DIM) == 0).astype(BF16)
    v_aug = jnp.concatenate(
        [v_slc, jnp.broadcast_to(ones_col, (B, G, S, LANES - HEAD_DIM))], axis=3)
    o_slc = _slc_attn(nat_b, selb, kt_aug, v_aug)

    kw_blk = (B_Q + 4 * G * HEAD_DIM) // (G * HEAD_DIM)
    (o_win,) = _band_attn(nat_b, nat_b, nat_b, q_blk=0, k_blk=kw_blk, v_blk=kw_blk + 1,
                          hq=B_HEADS, hk=B_KV, max_off=WIN_LEN - 1, tq=512, with_lse=False)

    return _merge(x, o0, l0, o1, l1, o2, l2, o_cmp, o_slc, o_win, gates, _gate_expand(),
                  w_branch_a.astype(BF16), w_branch_b.astype(BF16), w_out.astype(BF16))


def kernel(x, norm_mix_g, w_in, cmp_pos_k, cmp_w1_k, cmp_w2_k, cmp_pos_v, cmp_w1_v, cmp_w2_v,
           w_branch_a, w_branch_b, w_out, norm_mlp_g, w_up, w_down, norm_final_g):
    B, S, D = x.shape
    depth = w_in.shape[0]
    tabs = _rope_tables(S)
    for l in range(depth):
        x1 = _layer(x, norm_mix_g[l], w_in[l], cmp_pos_k[l], cmp_w1_k[l], cmp_w2_k[l],
                    cmp_pos_v[l], cmp_w1_v[l], cmp_w2_v[l], w_branch_a[l], w_branch_b[l],
                    w_out[l], tabs)
        x = _ffn(x1, norm_mlp_g[l].reshape(1, D), w_up[l].astype(BF16), w_down[l].astype(BF16),
                 norm_final_g.reshape(1, D), final_norm=(l == depth - 1))
    return x
```

```python
import functools

import jax
import jax.numpy as jnp
from jax import lax
from jax.experimental import pallas as pl
from jax.experimental.pallas import tpu as pltpu

F32 = jnp.float32
BF16 = jnp.bfloat16

D_MODEL = 1024
HEAD_DIM = 64
ROT_DIM = HEAD_DIM // 4
ROPE_THETA = 500000.0
EPS = 1e-6
DIL_PAIRS = ((128, 1), (512, 4), (2048, 16))
A_SLOTS = 8
A_HEADS = A_SLOTS * len(DIL_PAIRS)
B_HEADS = 8
B_KV = 2
CMP_LEN = 32
CMP_STRIDE = 16
CMP_HIDDEN = 4 * HEAD_DIM
SEL_LEN = 64
SEL_TOP = 16
WIN_LEN = 512
FORCE = 1e4
D_FF = 4 * D_MODEL
A_QKV = 3 * A_HEADS * HEAD_DIM
B_Q = B_HEADS * HEAD_DIM
B_KV_COLS = 6 * B_KV * HEAD_DIM
B_GATE = 3 * B_HEADS
MERGE_GATE = 2 * D_MODEL

LANES = 128
NEG = -1e30
QSCALE = HEAD_DIM ** -0.5 * 1.4426950408889634

A_COLS = 3 * A_SLOTS * HEAD_DIM
B_COLS = B_Q + B_KV_COLS
PROJ_CHUNK = 256
G_PAD = PROJ_CHUNK
G_COLS = MERGE_GATE + G_PAD
W_COLS = 3 * A_COLS + B_COLS + G_COLS

ROW_TILE = 512
VMEM_LIMIT = 56 * 1024 * 1024


def _cparams(n_axes, vmem=VMEM_LIMIT):
    return pltpu.CompilerParams(dimension_semantics=("arbitrary",) * n_axes, vmem_limit_bytes=vmem)


def _rms(x, g):
    return x * lax.rsqrt(jnp.mean(x * x, axis=-1, keepdims=True) + EPS) * g


def _dot(a, b):
    return jnp.dot(a, b, preferred_element_type=F32)


def _dot_nt(a, b):
    return lax.dot_general(a, b, (((1,), (1,)), ((), ())), preferred_element_type=F32)


def _split_bf16(x, parts):
    out = []
    for _ in range(parts - 1):
        hi = x.astype(BF16)
        out.append(hi)
        x = x - hi.astype(F32)
    out.append(x.astype(BF16))
    return out


_B_ROT = (True,) * (B_Q // LANES) + (True, False, True, False, True, False)


def _rotary(x, cos_t, sin_a, sin_b):
    half = ROT_DIM // 2
    return x * cos_t + pltpu.roll(x, LANES - half, 1) * sin_a + pltpu.roll(x, half, 1) * sin_b


def _in_proj_kernel(x_ref, g_ref, w_ref, wg_ref, tab_ref,
                    a0_ref, a1_ref, a2_ref, b_ref, cmp_ref, gate_ref, u_scr):
    T = x_ref.shape[0]
    u = _rms(x_ref[...], g_ref[...])
    nslab = u_scr.shape[0]
    for k in range(nslab):
        u_scr[k] = u[:, k * LANES:(k + 1) * LANES]
    u_nat = u.astype(BF16)

    def dilated(load, d):
        rows = T // d
        return jnp.concatenate([load(pl.ds(r, rows, stride=d)) for r in range(d)], axis=0)

    def u_rows(rows):
        return jnp.concatenate([u_scr[k, rows, :] for k in range(nslab)], axis=1).astype(BF16)

    q_slabs = A_SLOTS * HEAD_DIM // LANES
    per = PROJ_CHUNK // LANES

    def project(lhs, weights, chunk_cols, rot, tables, store, n_q=0):
        for c, w_col in enumerate(chunk_cols):
            res = _dot(lhs, weights[:, w_col:w_col + PROJ_CHUNK])
            for k in range(per):
                slab = res[:, k * LANES:(k + 1) * LANES]
                if rot[c * per + k]:
                    slab = _rotary(slab, tables[0], tables[1], tables[2])
                if c * per + k < n_q:
                    slab = slab * QSCALE
                store((c * per + k) * LANES, slab)

    a_rot = (True,) * (2 * q_slabs) + (False,) * q_slabs
    group_w = A_SLOTS * HEAD_DIM

    def group_cols(g):
        return [t * A_HEADS * HEAD_DIM + g * group_w + c
                for t in range(3) for c in range(0, group_w, PROJ_CHUNK)]

    tab0 = tuple(tab_ref[k] for k in range(3))

    def store_a0(c0, res):
        a0_ref[:, c0:c0 + LANES] = res.astype(BF16)

    project(u_nat, w_ref, group_cols(0), a_rot, tab0, store_a0, n_q=q_slabs)

    for g, (d, a_ref) in enumerate(((4, a1_ref), (16, a2_ref)), start=1):
        rows = T // d
        tabs = tuple(dilated(lambda rs, k=k: tab_ref[k, rs, :], d) for k in range(3))

        def store_ad(c0, res, a_ref=a_ref, d=d, rows=rows):
            for r in range(d):
                a_ref[r, :, c0:c0 + LANES] = res[r * rows:(r + 1) * rows].astype(BF16)

        project(dilated(u_rows, d), w_ref, group_cols(g), a_rot, tabs, store_ad, n_q=q_slabs)

    def store_b(c0, res):
        b_ref[:, c0:c0 + LANES] = res.astype(BF16)
        if B_Q <= c0 < B_Q + 2 * LANES:
            cmp_ref[:, c0 - B_Q:c0 - B_Q + LANES] = res

    project(u_nat, w_ref, range(A_QKV, A_QKV + B_COLS, PROJ_CHUNK), _B_ROT, tab0, store_b,
            n_q=q_slabs)

    def store_g(c0, res):
        gate_ref[:, c0:c0 + LANES] = jax.nn.sigmoid(res).astype(gate_ref.dtype)

    project(u_nat, wg_ref, range(0, G_COLS, PROJ_CHUNK), (False,) * (G_COLS // LANES), tab0, store_g)


def _in_proj(x, g, w_main, w_gate, tab):
    B, S, D = x.shape
    T = ROW_TILE
    nt = S // T
    out_shape = (
        jax.ShapeDtypeStruct((B, S, A_COLS), BF16),
        jax.ShapeDtypeStruct((B * 4, S // 4, A_COLS), BF16),
        jax.ShapeDtypeStruct((B * 16, S // 16, A_COLS), BF16),
        jax.ShapeDtypeStruct((B, S, B_COLS), BF16),
        jax.ShapeDtypeStruct((B, S, 2 * LANES), F32),
        jax.ShapeDtypeStruct((B, S, G_COLS), BF16),
    )
    in_specs = [
        pl.BlockSpec((None, T, D), lambda b, i: (b, i, 0)),
        pl.BlockSpec((1, D), lambda b, i: (0, 0)),
        pl.BlockSpec(w_main.shape, lambda b, i: (0, 0), pipeline_mode=pl.Buffered(1)),
        pl.BlockSpec(w_gate.shape, lambda b, i: (0, 0), pipeline_mode=pl.Buffered(1)),
        pl.BlockSpec((3, T, LANES), lambda b, i: (0, i, 0)),
    ]
    out_specs = (
        pl.BlockSpec((None, T, A_COLS), lambda b, i: (b, i, 0)),
        pl.BlockSpec((4, T // 4, A_COLS), lambda b, i: (b, i, 0)),
        pl.BlockSpec((16, T // 16, A_COLS), lambda b, i: (b, i, 0)),
        pl.BlockSpec((None, T, B_COLS), lambda b, i: (b, i, 0)),
        pl.BlockSpec((None, T, 2 * LANES), lambda b, i: (b, i, 0)),
        pl.BlockSpec((None, T, G_COLS), lambda b, i: (b, i, 0)),
    )
    return pl.pallas_call(
        _in_proj_kernel, grid=(B, nt), in_specs=in_specs, out_specs=out_specs, out_shape=out_shape,
        scratch_shapes=[pltpu.VMEM((D // LANES, T, LANES), F32)], compiler_params=_cparams(2),
        name="in_proj",
    )(x, g, w_main, w_gate, tab)


def _gelu_tanh(x):
    return 0.5 * x * (1.0 + jnp.tanh(0.7978845608028654 * (x + 0.044715 * (x * x * x))))


def _compress_kernel(x_ref, pos_ref, w1_ref, w2_ref, o_ref):
    nch = x_ref.shape[0] // CMP_STRIDE
    half = CMP_STRIDE * LANES
    chunks = jnp.concatenate(
        [x_ref[pl.ds(j, nch, stride=CMP_STRIDE), :] for j in range(CMP_STRIDE)], axis=1).astype(BF16)
    lane_group = (lax.broadcasted_iota(jnp.int32, (1, half), 1) % LANES) // HEAD_DIM
    pb = _dot(pos_ref[...], w1_ref[...])[0:1]
    for g in range(B_KV):
        c = jnp.where(lane_group == g, chunks, jnp.zeros((), BF16))
        p1 = _dot(c, w1_ref[:half, :])
        p2 = _dot(c, w1_ref[half:, :])
        p2_next = pltpu.roll(p2, nch - 1, 0)
        h = _gelu_tanh(p1 + p2_next + pb)
        o_ref[g] = _dot(h.astype(BF16), w2_ref[...]).astype(BF16)


def _compress(cmp_in, pos, w1, w2):
    B, S, _ = cmp_in.shape
    nch = S // CMP_STRIDE
    kx = 2 * CMP_STRIDE * LANES
    return pl.pallas_call(
        _compress_kernel, grid=(2, B),
        in_specs=[
            pl.BlockSpec((None, S, LANES), lambda t, b: (b, 0, t)),
            pl.BlockSpec((None, 8, kx), lambda t, b: (t, 0, 0)),
            pl.BlockSpec((None, kx, CMP_HIDDEN), lambda t, b: (t, 0, 0)),
            pl.BlockSpec((None, CMP_HIDDEN, HEAD_DIM), lambda t, b: (t, 0, 0)),
        ],
        out_specs=pl.BlockSpec((None, None, B_KV, nch, HEAD_DIM), lambda t, b: (t, b, 0, 0, 0)),
        out_shape=jax.ShapeDtypeStruct((2, B, B_KV, nch, HEAD_DIM), BF16),
        compiler_params=_cparams(2), name="compress",
    )(cmp_in, pos, w1, w2)


CMP_TQ = 256


def _cmp_attn_kernel(q_ref, kc_ref, vct_ref, ovt_ref, o_ref, sel_ref):
    i = pl.program_id(1)
    tq = q_ref.shape[0]
    nc = kc_ref.shape[1]
    ns = ovt_ref.shape[0]
    rep = B_HEADS // B_KV
    t = i * tq + lax.broadcasted_iota(jnp.int32, (1, tq), 1)
    n = lax.broadcasted_iota(jnp.int32, (nc, 1), 0)
    cmask = (n * CMP_STRIDE + (CMP_LEN - 1)) <= t
    blk = lax.broadcasted_iota(jnp.int32, (ns, 1), 0)
    blk_f = blk.astype(F32)
    cur = jnp.right_shift(t, SEL_LEN.bit_length() - 1)
    forced = (blk == 0) | (blk == cur) | (blk == cur - 1)
    future = blk > cur
    o_t = []
    for g in range(B_KV):
        kc = kc_ref[g]
        vct = vct_ref[g]
        psum = jnp.zeros((nc, tq), F32)
        for r in range(rep):
            h = g * rep + r
            qh = q_ref[:, h * HEAD_DIM:(h + 1) * HEAD_DIM]
            s = jnp.where(cmask, _dot_nt(kc, qh), NEG)
            p = jnp.where(cmask, jnp.exp2(s - jnp.max(s, axis=0, keepdims=True)), 0.0)
            p = p / jnp.maximum(jnp.sum(p, axis=0, keepdims=True), 1e-30)
            o_t.append(_dot(vct, p.astype(BF16)))
            psum = psum + p
        ovt = ovt_ref[...]
        imp = sum(_dot(ovt, piece) for piece in _split_bf16(psum, 3))
        imp = jnp.where(forced, FORCE, imp)
        imp = jnp.where(future, -FORCE, imp)
        bias = jnp.full((ns, tq), NEG, F32)
        for _ in range(SEL_TOP):
            mx = jnp.max(imp, axis=0, keepdims=True)
            first = jnp.min(jnp.where(imp == mx, blk_f, float(ns)), axis=0, keepdims=True)
            hit = blk_f == first
            bias = jnp.where(hit, 0.0, bias)
            imp = jnp.where(hit, -3e38, imp)
        sel_ref[:, g * ns:(g + 1) * ns] = jnp.transpose(bias).astype(BF16)
    o_ref[...] = jnp.transpose(jnp.concatenate(o_t, axis=0)).astype(BF16)


def _cmp_attn(nat_b, kc, vct, overlap_t):
    B, S, _ = nat_b.shape
    nc = kc.shape[2]
    ns = overlap_t.shape[0]
    tq = CMP_TQ
    return pl.pallas_call(
        _cmp_attn_kernel, grid=(B, S // tq),
        in_specs=[
            pl.BlockSpec((None, tq, B_Q), lambda b, i: (b, i, 0)),
            pl.BlockSpec((None, B_KV, nc, HEAD_DIM), lambda b, i: (b, 0, 0, 0)),
            pl.BlockSpec((None, B_KV, HEAD_DIM, nc), lambda b, i: (b, 0, 0, 0)),
            pl.BlockSpec((ns, nc), lambda b, i: (0, 0)),
        ],
        out_specs=(
            pl.BlockSpec((None, tq, B_Q), lambda b, i: (b, i, 0)),
            pl.BlockSpec((None, tq, B_KV * ns), lambda b, i: (b, i, 0)),
        ),
        out_shape=(
            jax.ShapeDtypeStruct((B, S, B_Q), BF16),
            jax.ShapeDtypeStruct((B, S, B_KV * ns), BF16),
        ),
        compiler_params=_cparams(2), name="cmp_attn",
    )(nat_b, kc, vct, overlap_t)


SLC_TK = 256
SLC_TQ = 2 * SLC_TK


def _slc_attn_kernel(q_ref, sel_ref, kt_ref, v_ref, o_ref, qa_scr, m_scr, acc_scr, s_a, s_b):
    i = pl.program_id(2)
    tq = q_ref.shape[0]
    rep = B_HEADS // B_KV
    sel = sel_ref[...]
    for r in range(rep):
        qa_scr[r * tq:(r + 1) * tq] = jnp.concatenate(
            [q_ref[:, r * HEAD_DIM:(r + 1) * HEAD_DIM], sel], axis=1)
    m_scr[...] = jnp.full(m_scr.shape, NEG, F32)
    acc_scr[...] = jnp.zeros(acc_scr.shape, F32)
    rel = (lax.broadcasted_iota(jnp.int32, (tq, SLC_TK), 0)
           - lax.broadcasted_iota(jnp.int32, (tq, SLC_TK), 1))

    def scores(j, s_buf):
        k0 = pl.multiple_of(j * SLC_TK, SLC_TK)
        s_buf[...] = _dot(qa_scr[...], kt_ref[:, pl.ds(k0, SLC_TK)])

    def update(j, s_buf, diag_offset=None):
        k0 = pl.multiple_of(j * SLC_TK, SLC_TK)
        vj = v_ref[pl.ds(k0, SLC_TK), :]
        for r in range(rep):
            rows = slice(r * tq, (r + 1) * tq)
            s = s_buf[rows]
            if diag_offset is not None:
                s = jnp.where(rel >= diag_offset, s, NEG)
            m_prev = m_scr[rows]
            m_new = jnp.maximum(m_prev, jnp.max(s, axis=-1, keepdims=True))
            alpha = jnp.exp2(m_prev - m_new)
            p = jnp.exp2(s - jnp.concatenate([m_new] * (SLC_TK // LANES), axis=1))
            acc_scr[rows] = alpha * acc_scr[rows] + _dot(p.astype(BF16), vj)
            m_scr[rows] = m_new

    def tile_pair(jj, carry):
        a = 2 * jj
        scores(a + 1, s_b)
        update(a, s_a)
        scores(a + 2, s_a)
        update(a + 1, s_b)
        return carry

    scores(0, s_a)
    lax.fori_loop(0, i, tile_pair, 0)
    scores(2 * i + 1, s_b)
    update(2 * i, s_a, diag_offset=0)
    update(2 * i + 1, s_b, diag_offset=SLC_TK)

    for r in range(rep):
        acc = acc_scr[r * tq:(r + 1) * tq]
        o_ref[:, r * HEAD_DIM:(r + 1) * HEAD_DIM] = (
            acc[:, :HEAD_DIM] / acc[:, HEAD_DIM:HEAD_DIM + 1]).astype(BF16)


def _slc_attn(nat_b, selb, kt_aug, v_slc):
    B, S, _ = nat_b.shape
    G = B_KV
    rep = B_HEADS // B_KV
    ns = selb.shape[2] // G
    tq = SLC_TQ
    gw = rep * HEAD_DIM
    return pl.pallas_call(
        _slc_attn_kernel, grid=(B, G, S // tq),
        in_specs=[
            pl.BlockSpec((None, tq, gw), lambda b, g, i: (b, i, g)),
            pl.BlockSpec((None, tq, ns), lambda b, g, i: (b, i, g)),
            pl.BlockSpec((None, None, HEAD_DIM + ns, S), lambda b, g, i: (b, g, 0, 0)),
            pl.BlockSpec((None, None, S, LANES), lambda b, g, i: (b, g, 0, 0)),
        ],
        out_specs=pl.BlockSpec((None, tq, gw), lambda b, g, i: (b, i, g)),
        out_shape=jax.ShapeDtypeStruct((B, S, B_Q), BF16),
        scratch_shapes=[pltpu.VMEM((rep * tq, HEAD_DIM + ns), BF16),
                        pltpu.VMEM((rep * tq, LANES), F32), pltpu.VMEM((rep * tq, LANES), F32),
                        pltpu.VMEM((rep * tq, SLC_TK), F32), pltpu.VMEM((rep * tq, SLC_TK), F32)],
        compiler_params=_cparams(3), name="slc_attn",
    )(nat_b, selb, kt_aug, v_slc)


SUB = 128


def _band_kernel(q_ref, kp_ref, kc_ref, vp_ref, vc_ref, o_ref, *lse_refs, shared_kv, prev, max_off):
    i = pl.program_id(1)
    tq = q_ref.shape[0]
    n_slab = q_ref.shape[1] // LANES
    width = prev + SUB
    r_i = lax.broadcasted_iota(jnp.int32, (SUB, width), 0)
    c_i = lax.broadcasted_iota(jnp.int32, (SUB, width), 1)
    diff = r_i - c_i + prev
    in_band = (diff >= 0) & (diff <= max_off)
    lane = lax.broadcasted_iota(jnp.int32, (1, LANES), 1)
    lo_half = lane < HEAD_DIM
    half_mask = {"lo": lo_half, "hi": jnp.logical_not(lo_half)}

    def window(p_ref, c_ref, sub, slab):
        cols = slice(slab * LANES, (slab + 1) * LANES)
        lo = sub * SUB
        parts = []
        if lo < prev:
            parts.append(p_ref[lo:prev, cols])
        parts.append(c_ref[max(0, lo - prev):lo + SUB, cols])
        return jnp.concatenate(parts, axis=0) if len(parts) > 1 else parts[0]

    def swap_halves(x):
        return jnp.concatenate([x[:, HEAD_DIM:], x[:, :HEAD_DIM]], axis=1)

    for sub in range(tq // SUB):
        rows = slice(sub * SUB, (sub + 1) * SUB)
        k_start = i * tq + sub * SUB - prev
        bias = jnp.where(in_band & (c_i + k_start >= 0), 0.0, NEG)
        if shared_kv:
            k2 = window(kp_ref, kc_ref, sub, 0)
            v2 = window(vp_ref, vc_ref, sub, 0)
            groups = [
                (((0, "lo"), (1, "lo"), (2, "hi"), (3, "hi")), k2, v2),
                (((0, "hi"), (1, "hi"), (2, "lo"), (3, "lo")), swap_halves(k2), swap_halves(v2)),
            ]
        else:
            groups = [(((p, "lo"), (p, "hi")), window(kp_ref, kc_ref, sub, p),
                       window(vp_ref, vc_ref, sub, p)) for p in range(n_slab)]
        outs = {}
        for members, k2, v2 in groups:
            nm = len(members)
            zero = jnp.zeros((), BF16)
            lhs = jnp.concatenate(
                [jnp.where(half_mask[half], q_ref[rows, slab * LANES:(slab + 1) * LANES], zero)
                 for slab, half in members], axis=0)
            s = _dot_nt(lhs, k2)
            s = (s.reshape(nm, SUB, width) + bias[None]).reshape(nm * SUB, width)
            m = jnp.max(s, axis=-1, keepdims=True)
            p = jnp.exp2(s - m)
            den = jnp.sum(p, axis=-1, keepdims=True)
            pv = _dot(p.astype(BF16), v2) / den
            lse = m + jnp.log2(den)
            for idx, member in enumerate(members):
                outs[member] = (pv[idx * SUB:(idx + 1) * SUB], lse[idx * SUB:(idx + 1) * SUB])
        lse_tile = jnp.zeros((SUB, LANES), F32)
        for slab in range(n_slab):
            cols = slice(slab * LANES, (slab + 1) * LANES)
            (o_lo, l_lo), (o_hi, l_hi) = outs[(slab, "lo")], outs[(slab, "hi")]
            o_ref[rows, cols] = jnp.where(lo_half, o_lo, o_hi).astype(o_ref.dtype)
            lse_tile = jnp.where(lane == 2 * slab, l_lo,
                                 jnp.where(lane == 2 * slab + 1, l_hi, lse_tile))
        if lse_refs:
            lse_refs[0][rows, :] = lse_tile


def _band_attn(q_arr, k_arr, v_arr, *, q_blk, k_blk, v_blk, hq, hk, max_off, tq, with_lse):
    N, L, _ = q_arr.shape
    assert hq == hk or (hq, hk) == (8, 2), (hq, hk)
    prev = -(-max_off // SUB) * SUB
    m = tq // prev
    qw, kw = hq * HEAD_DIM, hk * HEAD_DIM
    cur = lambda blk: (lambda n, i: (n, i, blk))
    prv = lambda blk: (lambda n, i: (n, jnp.maximum(i * m - 1, 0), blk))
    out_shape = [jax.ShapeDtypeStruct((N, L, qw), BF16)]
    out_specs = [pl.BlockSpec((None, tq, qw), cur(0))]
    if with_lse:
        out_shape.append(jax.ShapeDtypeStruct((N, L, LANES), F32))
        out_specs.append(pl.BlockSpec((None, tq, LANES), cur(0)))
    return pl.pallas_call(
        functools.partial(_band_kernel, shared_kv=hq != hk, prev=prev, max_off=max_off),
        grid=(N, L // tq),
        in_specs=[
            pl.BlockSpec((None, tq, qw), cur(q_blk)),
            pl.BlockSpec((None, prev, kw), prv(k_blk)),
            pl.BlockSpec((None, tq, kw), cur(k_blk)),
            pl.BlockSpec((None, prev, kw), prv(v_blk)),
            pl.BlockSpec((None, tq, kw), cur(v_blk)),
        ],
        out_specs=tuple(out_specs), out_shape=tuple(out_shape),
        compiler_params=_cparams(2), name="band_attn",
    )(q_arr, k_arr, k_arr, v_arr, v_arr)


def _merge_kernel(x_ref, o0_ref, l0_ref, o1_ref, l1_ref, o2_ref, l2_ref, oc_ref, os_ref, ow_ref,
                  gate_ref, ex_ref, wa_ref, wb_ref, wo_ref, out_ref, s_o1, s_l1, s_o2, s_l2):
    T = x_ref.shape[0]
    for d, o_ref, l_ref, s_o, s_l in ((4, o1_ref, l1_ref, s_o1, s_l1), (16, o2_ref, l2_ref, s_o2, s_l2)):
        rows = T // d
        for r in range(d):
            o_r = o_ref[r].astype(F32)
            s_l[pl.ds(r, rows, stride=d), :] = l_ref[r]
            for k in range(s_o.shape[0]):
                s_o[k, pl.ds(r, rows, stride=d), :] = o_r[:, k * LANES:(k + 1) * LANES]
    gather = lambda s: jnp.concatenate([s[k] for k in range(s.shape[0])], axis=1)

    l0, l1, l2 = l0_ref[...], s_l1[...], s_l2[...]
    mx = jnp.maximum(jnp.maximum(l0, l1), l2)
    e0, e1, e2 = jnp.exp2(l0 - mx), jnp.exp2(l1 - mx), jnp.exp2(l2 - mx)
    inv = 1.0 / (e0 + e1 + e2)
    lo_half = lax.broadcasted_iota(jnp.int32, (1, LANES), 1) < HEAD_DIM

    def spread(w):
        col = lambda h: jnp.broadcast_to(w[:, h:h + 1], (T, LANES))
        return jnp.concatenate([jnp.where(lo_half, col(2 * p), col(2 * p + 1))
                                for p in range(A_SLOTS // 2)], axis=1)

    y_a = (spread(e0 * inv) * o0_ref[...].astype(F32) + spread(e1 * inv) * gather(s_o1)
           + spread(e2 * inv) * gather(s_o2))

    gbx = _dot(gate_ref[:, MERGE_GATE:], ex_ref[...])
    y_b = (gbx[:, :B_Q] * oc_ref[...].astype(F32)
           + gbx[:, B_Q:2 * B_Q] * os_ref[...].astype(F32)
           + gbx[:, 2 * B_Q:] * ow_ref[...].astype(F32))
    merged = (gate_ref[:, :D_MODEL].astype(F32) * _dot(y_a.astype(BF16), wa_ref[...])
              + gate_ref[:, D_MODEL:MERGE_GATE].astype(F32) * _dot(y_b.astype(BF16), wb_ref[...]))
    out_ref[...] = x_ref[...] + _dot(merged.astype(BF16), wo_ref[...])


def _merge(x, o0, l0, o1, l1, o2, l2, o_cmp, o_slc, o_win, gates, ex, wa, wb, wo):
    B, S, D = x.shape
    T = ROW_TILE
    qw = A_SLOTS * HEAD_DIM
    nat = lambda w: pl.BlockSpec((None, T, w), lambda b, i: (b, i, 0))
    dil = lambda d, w: pl.BlockSpec((d, T // d, w), lambda b, i: (b, i, 0))
    full = lambda a: pl.BlockSpec(a.shape, lambda b, i: (0,) * a.ndim)
    o_scr, l_scr = pltpu.VMEM((qw // LANES, T, LANES), F32), pltpu.VMEM((T, LANES), F32)
    return pl.pallas_call(
        _merge_kernel, grid=(B, S // T),
        in_specs=[nat(D), nat(qw), nat(LANES), dil(4, qw), dil(4, LANES), dil(16, qw),
                  dil(16, LANES), nat(B_Q), nat(B_Q), nat(B_Q), nat(G_COLS), full(ex), full(wa),
                  full(wb), full(wo)],
        out_specs=nat(D), out_shape=jax.ShapeDtypeStruct((B, S, D), F32),
        scratch_shapes=[o_scr, l_scr, o_scr, l_scr],
        compiler_params=_cparams(2), name="merge",
    )(x, o0, l0, o1, l1, o2, l2, o_cmp, o_slc, o_win, gates, ex, wa, wb, wo)


FF_CHUNK = 512


def _ffn_kernel(x_ref, g_ref, wu_ref, wd_ref, gf_ref, out_ref, *, final_norm):
    x = x_ref[...]
    u = _rms(x, g_ref[...]).astype(BF16)
    acc = x
    for c in range(D_FF // FF_CHUNK):
        cols = slice(c * FF_CHUNK, (c + 1) * FF_CHUNK)
        h = jnp.maximum(_dot(u, wu_ref[:, cols]), 0.0)
        acc = acc + _dot((h * h).astype(BF16), wd_ref[cols, :])
    out_ref[...] = _rms(acc, gf_ref[...]) if final_norm else acc


def _ffn(x, g_mlp, w_up, w_down, g_final, final_norm):
    B, S, D = x.shape
    T = ROW_TILE
    row = pl.BlockSpec((None, T, D), lambda b, i: (b, i, 0))
    vec = pl.BlockSpec((1, D), lambda b, i: (0, 0))
    return pl.pallas_call(
        functools.partial(_ffn_kernel, final_norm=final_norm), grid=(B, S // T),
        in_specs=[row, vec,
                  pl.BlockSpec((D, D_FF), lambda b, i: (0, 0), pipeline_mode=pl.Buffered(1)),
                  pl.BlockSpec((D_FF, D), lambda b, i: (0, 0), pipeline_mode=pl.Buffered(1)),
                  vec],
        out_specs=row, out_shape=jax.ShapeDtypeStruct((B, S, D), F32),
        compiler_params=_cparams(2), name="ffn",
    )(x, g_mlp, w_up, w_down, g_final)


def _rope_table(seq):
    half = ROT_DIM // 2
    pos = jnp.arange(seq, dtype=F32)[:, None]
    lane = jnp.arange(LANES, dtype=jnp.int32) % HEAD_DIM
    inv = ROPE_THETA ** (-(2 * (lane % half)).astype(F32) / ROT_DIM)
    ang = pos * inv[None, :]
    cos, sin = jnp.cos(ang), jnp.sin(ang)
    return jnp.stack([
        jnp.where(lane < ROT_DIM, cos, 1.0),
        jnp.where(lane < half, -sin, 0.0),
        jnp.where((lane >= half) & (lane < ROT_DIM), sin, 0.0),
    ])


def _layer_weights(w_in):
    o3 = A_QKV + B_COLS
    o4 = o3 + B_GATE
    w_main = w_in[:, :o3].astype(BF16)
    w_gate = jnp.concatenate(
        [w_in[:, o4:], jnp.pad(w_in[:, o3:o4], ((0, 0), (0, G_PAD - B_GATE)))], axis=1).astype(BF16)
    return w_main, w_gate


def _overlap_matrix(nc_pad, ns):
    c_start = jnp.arange(nc_pad) * CMP_STRIDE
    s_start = jnp.arange(ns) * SEL_LEN
    ov = (c_start[:, None] < s_start[None, :] + SEL_LEN) & (c_start[:, None] + CMP_LEN > s_start[None, :])
    return ov.astype(BF16)


def _gate_expand():
    rows = jnp.arange(G_PAD)[:, None]
    cols = jnp.arange(3 * B_Q)[None, :]
    br, h = cols // B_Q, (cols % B_Q) // HEAD_DIM
    return (rows == h * 3 + br).astype(BF16)


def _layer(x, g_mix, w_in, cmp_pos_k, cmp_w1_k, cmp_w2_k, cmp_pos_v, cmp_w1_v, cmp_w2_v,
           w_branch_a, w_branch_b, w_out, tab):
    B, S, D = x.shape
    G = B_KV
    a0, a1, a2, nat_b, cmp_in, gates = _in_proj(x, g_mix.reshape(1, D), *_layer_weights(w_in), tab)

    a_out = []
    for arr, (w, d) in zip((a0, a1, a2), DIL_PAIRS):
        a_out.append(_band_attn(arr, arr, arr, q_blk=0, k_blk=1, v_blk=2, hq=A_SLOTS, hk=A_SLOTS,
                                max_off=w // d, tq=256, with_lse=True))
    (o0, l0), (o1, l1), (o2, l2) = a_out

    nch = S // CMP_STRIDE
    kv_w = G * HEAD_DIM
    k_slc = nat_b[:, :, B_Q + 2 * kv_w:B_Q + 3 * kv_w].reshape(B, S, G, HEAD_DIM)
    v_slc = nat_b[:, :, B_Q + 3 * kv_w:B_Q + 4 * kv_w].reshape(B, S, G, HEAD_DIM)
    pos = jnp.stack([cmp_pos_k, cmp_pos_v]).astype(BF16)[:, :, None, :]
    pos = jnp.pad(pos, ((0, 0), (0, 0), (0, G - 1), (0, 0))).reshape(2, 1, CMP_LEN * G * HEAD_DIM)
    pos = jnp.broadcast_to(pos, (2, 8, CMP_LEN * G * HEAD_DIM))
    w1 = jnp.stack([cmp_w1_k, cmp_w1_v]).astype(BF16).reshape(2, CMP_LEN, 1, HEAD_DIM, CMP_HIDDEN)
    w1 = jnp.broadcast_to(w1, (2, CMP_LEN, G, HEAD_DIM, CMP_HIDDEN)).reshape(
        2, CMP_LEN * G * HEAD_DIM, CMP_HIDDEN)
    kcvc = _compress(cmp_in, pos, w1, jnp.stack([cmp_w2_k, cmp_w2_v]).astype(BF16))
    ns = S // SEL_LEN
    o_cmp, selb = _cmp_attn(nat_b, kcvc[0], kcvc[1].transpose(0, 1, 3, 2),
                            _overlap_matrix(nch, ns).T)

    onehot = (jnp.arange(ns)[:, None] == (jnp.arange(S) // SEL_LEN)[None, :]).astype(BF16)
    kt_aug = jnp.concatenate([k_slc.transpose(0, 2, 3, 1),
                              jnp.broadcast_to(onehot, (B, G, ns, S))], axis=2)
    ones_col = (jnp.arange(LANES - HEAD_DIM) == 0).astype(BF16)
    v_aug = jnp.concatenate(
        [v_slc.transpose(0, 2, 1, 3),
         jnp.broadcast_to(ones_col, (B, G, S, LANES - HEAD_DIM))], axis=3)
    o_slc = _slc_attn(nat_b, selb, kt_aug, v_aug)

    kw_blk = (B_Q + 4 * G * HEAD_DIM) // (G * HEAD_DIM)
    (o_win,) = _band_attn(nat_b, nat_b, nat_b, q_blk=0, k_blk=kw_blk, v_blk=kw_blk + 1,
                          hq=B_HEADS, hk=B_KV, max_off=WIN_LEN - 1, tq=512, with_lse=False)

    return _merge(x, o0, l0, o1, l1, o2, l2, o_cmp, o_slc, o_win, gates, _gate_expand(),
                  w_branch_a.astype(BF16), w_branch_b.astype(BF16), w_out.astype(BF16))


def kernel(x, norm_mix_g, w_in, cmp_pos_k, cmp_w1_k, cmp_w2_k, cmp_pos_v, cmp_w1_v, cmp_w2_v,
           w_branch_a, w_branch_b, w_out, norm_mlp_g, w_up, w_down, norm_final_g):
    B, S, D = x.shape
    depth = w_in.shape[0]
    tab = _rope_table(S)
    for l in range(depth):
        x1 = _layer(x, norm_mix_g[l], w_in[l], cmp_pos_k[l], cmp_w1_k[l], cmp_w2_k[l],
                    cmp_pos_v[l], cmp_w1_v[l], cmp_w2_v[l], w_branch_a[l], w_branch_b[l],
                    w_out[l], tab)
        x = _ffn(x1, norm_mlp_g[l].reshape(1, D), w_up[l].astype(BF16), w_down[l].astype(BF16),
                 norm_final_g.reshape(1, D), final_norm=(l == depth - 1))
    return x
```

```python
import functools

import jax
import jax.numpy as jnp
from jax import lax
from jax.experimental import pallas as pl
from jax.experimental.pallas import tpu as pltpu

F32 = jnp.float32
BF16 = jnp.bfloat16

D_MODEL = 1024
HEAD_DIM = 64
ROT_DIM = HEAD_DIM // 4
ROPE_THETA = 500000.0
EPS = 1e-6
DIL_PAIRS = ((128, 1), (512, 4), (2048, 16))
A_SLOTS = 8
A_HEADS = A_SLOTS * len(DIL_PAIRS)
B_HEADS = 8
B_KV = 2
CMP_LEN = 32
CMP_STRIDE = 16
CMP_HIDDEN = 4 * HEAD_DIM
SEL_LEN = 64
SEL_TOP = 16
WIN_LEN = 512
FORCE = 1e4
D_FF = 4 * D_MODEL
A_QKV = 3 * A_HEADS * HEAD_DIM
B_Q = B_HEADS * HEAD_DIM
B_KV_COLS = 6 * B_KV * HEAD_DIM
B_GATE = 3 * B_HEADS
MERGE_GATE = 2 * D_MODEL

LANES = 128
NEG = -1e30
QSCALE = HEAD_DIM ** -0.5 * 1.4426950408889634

A_COLS = 3 * A_SLOTS * HEAD_DIM
B_COLS = B_Q + B_KV_COLS
PROJ_CHUNK = 256
G_PAD = PROJ_CHUNK
G_COLS = MERGE_GATE + G_PAD
W_COLS = 3 * A_COLS + B_COLS + G_COLS

ROW_TILE = 512
VMEM_LIMIT = 56 * 1024 * 1024


def _cparams(n_axes, vmem=VMEM_LIMIT):
    return pltpu.CompilerParams(dimension_semantics=("arbitrary",) * n_axes, vmem_limit_bytes=vmem)


def _rms(x, g):
    return x * lax.rsqrt(jnp.mean(x * x, axis=-1, keepdims=True) + EPS) * g


def _dot(a, b):
    return jnp.dot(a, b, preferred_element_type=F32)


def _dot_nt(a, b):
    return lax.dot_general(a, b, (((1,), (1,)), ((), ())), preferred_element_type=F32)


def _split_bf16(x, parts):
    out = []
    for _ in range(parts - 1):
        hi = x.astype(BF16)
        out.append(hi)
        x = x - hi.astype(F32)
    out.append(x.astype(BF16))
    return out


_B_ROT = (True,) * (B_Q // LANES) + (True, False, True, False, True, False)


def _rotary(x, cos_t, sin_a, sin_b):
    half = ROT_DIM // 2
    return x * cos_t + pltpu.roll(x, LANES - half, 1) * sin_a + pltpu.roll(x, half, 1) * sin_b


def _in_proj_kernel(x_ref, g_ref, w_ref, wg_ref, tab_ref,
                    a0_ref, a1_ref, a2_ref, b_ref, cmp_ref, gate_ref, u_scr):
    T = x_ref.shape[0]
    u = _rms(x_ref[...], g_ref[...])
    nslab = u_scr.shape[0]
    for k in range(nslab):
        u_scr[k] = u[:, k * LANES:(k + 1) * LANES]
    u_nat = u.astype(BF16)

    def dilated(load, d):
        rows = T // d
        return jnp.concatenate([load(pl.ds(r, rows, stride=d)) for r in range(d)], axis=0)

    def u_rows(rows):
        return jnp.concatenate([u_scr[k, rows, :] for k in range(nslab)], axis=1).astype(BF16)

    q_slabs = A_SLOTS * HEAD_DIM // LANES
    per = PROJ_CHUNK // LANES

    def project(lhs, weights, chunk_cols, rot, tables, store, n_q=0):
        for c, w_col in enumerate(chunk_cols):
            res = _dot(lhs, weights[:, w_col:w_col + PROJ_CHUNK])
            for k in range(per):
                slab = res[:, k * LANES:(k + 1) * LANES]
                if rot[c * per + k]:
                    slab = _rotary(slab, tables[0], tables[1], tables[2])
                if c * per + k < n_q:
                    slab = slab * QSCALE
                store((c * per + k) * LANES, slab)

    a_rot = (True,) * (2 * q_slabs) + (False,) * q_slabs
    group_w = A_SLOTS * HEAD_DIM

    def group_cols(g):
        return [t * A_HEADS * HEAD_DIM + g * group_w + c
                for t in range(3) for c in range(0, group_w, PROJ_CHUNK)]

    tab0 = tuple(tab_ref[k] for k in range(3))

    def store_a0(c0, res):
        a0_ref[:, c0:c0 + LANES] = res.astype(BF16)

    project(u_nat, w_ref, group_cols(0), a_rot, tab0, store_a0, n_q=q_slabs)

    for g, (d, a_ref) in enumerate(((4, a1_ref), (16, a2_ref)), start=1):
        rows = T // d
        tabs = tuple(dilated(lambda rs, k=k: tab_ref[k, rs, :], d) for k in range(3))

        def store_ad(c0, res, a_ref=a_ref, d=d, rows=rows):
            for r in range(d):
                a_ref[r, :, c0:c0 + LANES] = res[r * rows:(r + 1) * rows].astype(BF16)

        project(dilated(u_rows, d), w_ref, group_cols(g), a_rot, tabs, store_ad, n_q=q_slabs)

    def store_b(c0, res):
        b_ref[:, c0:c0 + LANES] = res.astype(BF16)
        if B_Q <= c0 < B_Q + 2 * LANES:
            cmp_ref[:, c0 - B_Q:c0 - B_Q + LANES] = res

    project(u_nat, w_ref, range(A_QKV, A_QKV + B_COLS, PROJ_CHUNK), _B_ROT, tab0, store_b,
            n_q=q_slabs)

    def store_g(c0, res):
        gate_ref[:, c0:c0 + LANES] = jax.nn.sigmoid(res).astype(gate_ref.dtype)

    project(u_nat, wg_ref, range(0, G_COLS, PROJ_CHUNK), (False,) * (G_COLS // LANES), tab0, store_g)


def _in_proj(x, g, w_main, w_gate, tab):
    B, S, D = x.shape
    T = ROW_TILE
    nt = S // T
    out_shape = (
        jax.ShapeDtypeStruct((B, S, A_COLS), BF16),
        jax.ShapeDtypeStruct((B * 4, S // 4, A_COLS), BF16),
        jax.ShapeDtypeStruct((B * 16, S // 16, A_COLS), BF16),
        jax.ShapeDtypeStruct((B, S, B_COLS), BF16),
        jax.ShapeDtypeStruct((B, S, 2 * LANES), F32),
        jax.ShapeDtypeStruct((B, S, G_COLS), BF16),
    )
    in_specs = [
        pl.BlockSpec((None, T, D), lambda b, i: (b, i, 0)),
        pl.BlockSpec((1, D), lambda b, i: (0, 0)),
        pl.BlockSpec(w_main.shape, lambda b, i: (0, 0), pipeline_mode=pl.Buffered(1)),
        pl.BlockSpec(w_gate.shape, lambda b, i: (0, 0), pipeline_mode=pl.Buffered(1)),
        pl.BlockSpec((3, T, LANES), lambda b, i: (0, i, 0)),
    ]
    out_specs = (
        pl.BlockSpec((None, T, A_COLS), lambda b, i: (b, i, 0)),
        pl.BlockSpec((4, T // 4, A_COLS), lambda b, i: (b, i, 0)),
        pl.BlockSpec((16, T // 16, A_COLS), lambda b, i: (b, i, 0)),
        pl.BlockSpec((None, T, B_COLS), lambda b, i: (b, i, 0)),
        pl.BlockSpec((None, T, 2 * LANES), lambda b, i: (b, i, 0)),
        pl.BlockSpec((None, T, G_COLS), lambda b, i: (b, i, 0)),
    )
    return pl.pallas_call(
        _in_proj_kernel, grid=(B, nt), in_specs=in_specs, out_specs=out_specs, out_shape=out_shape,
        scratch_shapes=[pltpu.VMEM((D // LANES, T, LANES), F32)], compiler_params=_cparams(2),
        name="in_proj",
    )(x, g, w_main, w_gate, tab)


def _gelu_tanh(x):
    return 0.5 * x * (1.0 + jnp.tanh(0.7978845608028654 * (x + 0.044715 * (x * x * x))))


def _compress_kernel(x_ref, pos_ref, w1_ref, w2_ref, o_ref):
    nch = x_ref.shape[0] // CMP_STRIDE
    half = CMP_STRIDE * LANES
    chunks = jnp.concatenate(
        [x_ref[pl.ds(j, nch, stride=CMP_STRIDE), :] for j in range(CMP_STRIDE)], axis=1).astype(BF16)
    lane_group = (lax.broadcasted_iota(jnp.int32, (1, half), 1) % LANES) // HEAD_DIM
    pb = _dot(pos_ref[...], w1_ref[...])[0:1]
    for g in range(B_KV):
        c = jnp.where(lane_group == g, chunks, jnp.zeros((), BF16))
        p1 = _dot(c, w1_ref[:half, :])
        p2 = _dot(c, w1_ref[half:, :])
        p2_next = pltpu.roll(p2, nch - 1, 0)
        h = _gelu_tanh(p1 + p2_next + pb)
        o_ref[g] = _dot(h.astype(BF16), w2_ref[...]).astype(BF16)


def _compress(cmp_in, pos, w1, w2):
    B, S, _ = cmp_in.shape
    nch = S // CMP_STRIDE
    kx = 2 * CMP_STRIDE * LANES
    return pl.pallas_call(
        _compress_kernel, grid=(2, B),
        in_specs=[
            pl.BlockSpec((None, S, LANES), lambda t, b: (b, 0, t)),
            pl.BlockSpec((None, 8, kx), lambda t, b: (t, 0, 0)),
            pl.BlockSpec((None, kx, CMP_HIDDEN), lambda t, b: (t, 0, 0)),
            pl.BlockSpec((None, CMP_HIDDEN, HEAD_DIM), lambda t, b: (t, 0, 0)),
        ],
        out_specs=pl.BlockSpec((None, None, B_KV, nch, HEAD_DIM), lambda t, b: (t, b, 0, 0, 0)),
        out_shape=jax.ShapeDtypeStruct((2, B, B_KV, nch, HEAD_DIM), BF16),
        compiler_params=_cparams(2), name="compress",
    )(cmp_in, pos, w1, w2)


CMP_TQ = 256


def _cmp_attn_kernel(q_ref, kc_ref, vct_ref, ovt_ref, o_ref, sel_ref):
    i = pl.program_id(1)
    tq = q_ref.shape[0]
    nc = kc_ref.shape[1]
    ns = ovt_ref.shape[0]
    rep = B_HEADS // B_KV
    t = i * tq + lax.broadcasted_iota(jnp.int32, (1, tq), 1)
    n = lax.broadcasted_iota(jnp.int32, (nc, 1), 0)
    hidden = jnp.where((n * CMP_STRIDE + (CMP_LEN - 1)) <= t, 0.0, NEG)
    live = jnp.where(t >= CMP_LEN - 1, 1.0, 0.0)
    blk = lax.broadcasted_iota(jnp.int32, (ns, 1), 0)
    blk_f = blk.astype(F32)
    cur = jnp.right_shift(t, SEL_LEN.bit_length() - 1)
    forced = (blk == 0) | (blk == cur) | (blk == cur - 1)
    keep = jnp.where(forced | (blk > cur), 0.0, 1.0)
    pinned = jnp.where(forced, FORCE, jnp.where(blk > cur, -FORCE, 0.0))
    removed = -3e38
    o_t, imps = [], []
    for g in range(B_KV):
        kc = kc_ref[g]
        vct = vct_ref[g]
        psum = jnp.zeros((nc, tq), F32)
        for r in range(rep):
            h = g * rep + r
            qh = q_ref[:, h * HEAD_DIM:(h + 1) * HEAD_DIM]
            s = _dot_nt(kc, qh) + hidden
            p = jnp.exp2(s - jnp.max(s, axis=0, keepdims=True))
            p = p * (live / jnp.sum(p, axis=0, keepdims=True))
            o_t.append(_dot(vct, p.astype(BF16)))
            psum = psum + p
        ovt = ovt_ref[...]
        imp = sum(_dot(ovt, piece) for piece in _split_bf16(psum, 3))
        imps.append(imp * keep + pinned)
    imp = jnp.concatenate(imps, axis=1)
    for _ in range(SEL_TOP):
        mx = jnp.max(imp, axis=0, keepdims=True)
        first = jnp.min(jnp.where(imp == mx, blk_f, float(ns)), axis=0, keepdims=True)
        imp = jnp.where(blk_f == first, removed, imp)
    for g in range(B_KV):
        picked = imp[:, g * tq:(g + 1) * tq] < 0.5 * removed
        sel_ref[g] = jnp.where(picked, 0.0, NEG).astype(BF16)
    o_ref[...] = jnp.transpose(jnp.concatenate(o_t, axis=0)).astype(BF16)


def _cmp_attn(nat_b, kc, vct, overlap_t):
    B, S, _ = nat_b.shape
    nc = kc.shape[2]
    ns = overlap_t.shape[0]
    tq = CMP_TQ
    return pl.pallas_call(
        _cmp_attn_kernel, grid=(B, S // tq),
        in_specs=[
            pl.BlockSpec((None, tq, B_Q), lambda b, i: (b, i, 0)),
            pl.BlockSpec((None, B_KV, nc, HEAD_DIM), lambda b, i: (b, 0, 0, 0)),
            pl.BlockSpec((None, B_KV, HEAD_DIM, nc), lambda b, i: (b, 0, 0, 0)),
            pl.BlockSpec((ns, nc), lambda b, i: (0, 0)),
        ],
        out_specs=(
            pl.BlockSpec((None, tq, B_Q), lambda b, i: (b, i, 0)),
            pl.BlockSpec((None, B_KV, ns, tq), lambda b, i: (b, 0, 0, i)),
        ),
        out_shape=(
            jax.ShapeDtypeStruct((B, S, B_Q), BF16),
            jax.ShapeDtypeStruct((B, B_KV, ns, S), BF16),
        ),
        compiler_params=_cparams(2), name="cmp_attn",
    )(nat_b, kc, vct, overlap_t)


SLC_TK = 256
SLC_TQ = 2 * SLC_TK


SLC_VROWS = 80


def _slc_attn_kernel(q_ref, selt_ref, k_ref, vt_ref, o_ref, qat_scr, m_scr, acc_scr, s_a, s_b):
    i = pl.program_id(2)
    tq = q_ref.shape[0]
    rep = B_HEADS // B_KV
    selt = selt_ref[...]
    for r in range(rep):
        cols = slice(r * tq, (r + 1) * tq)
        q_r = q_ref[:, r * HEAD_DIM:(r + 1) * HEAD_DIM].astype(F32)
        qat_scr[:HEAD_DIM, cols] = jnp.transpose(q_r).astype(BF16)
        qat_scr[HEAD_DIM:, cols] = selt
    m_scr[...] = jnp.full(m_scr.shape, NEG, F32)
    acc_scr[...] = jnp.zeros(acc_scr.shape, F32)
    rel = (lax.broadcasted_iota(jnp.int32, (SLC_TK, tq), 1)
           - lax.broadcasted_iota(jnp.int32, (SLC_TK, tq), 0))

    def scores(j, s_buf):
        k0 = pl.multiple_of(j * SLC_TK, SLC_TK)
        s_buf[...] = _dot(k_ref[pl.ds(k0, SLC_TK), :], qat_scr[...])

    def update(j, s_buf, diag_offset=None):
        k0 = pl.multiple_of(j * SLC_TK, SLC_TK)
        vt = vt_ref[:, pl.ds(k0, SLC_TK)]
        for r in range(rep):
            cols = slice(r * tq, (r + 1) * tq)
            s = s_buf[:, cols]
            if diag_offset is not None:
                s = jnp.where(rel >= diag_offset, s, NEG)
            m_prev = m_scr[:, cols]
            m_new = jnp.maximum(m_prev, jnp.max(s, axis=0, keepdims=True))
            alpha = jnp.exp2(m_prev - m_new)
            p = jnp.exp2(s - m_new).astype(BF16)
            acc_scr[:, cols] = alpha * acc_scr[:, cols] + _dot(vt, p)
            m_scr[:, cols] = m_new

    def tile_pair(jj, carry):
        a = 2 * jj
        scores(a + 1, s_b)
        update(a, s_a)
        scores(a + 2, s_a)
        update(a + 1, s_b)
        return carry

    scores(0, s_a)
    lax.fori_loop(0, i, tile_pair, 0)
    scores(2 * i + 1, s_b)
    update(2 * i, s_a, diag_offset=0)
    update(2 * i + 1, s_b, diag_offset=SLC_TK)

    for r in range(rep):
        acc = acc_scr[:, r * tq:(r + 1) * tq]
        o_t = acc[:HEAD_DIM] / acc[HEAD_DIM:HEAD_DIM + 1]
        o_ref[:, r * HEAD_DIM:(r + 1) * HEAD_DIM] = jnp.transpose(o_t).astype(BF16)


def _slc_attn(nat_b, selt, k_aug, vt_aug):
    B, S, _ = nat_b.shape
    G = B_KV
    rep = B_HEADS // B_KV
    ns = selt.shape[2]
    tq = SLC_TQ
    gw = rep * HEAD_DIM
    return pl.pallas_call(
        _slc_attn_kernel, grid=(B, G, S // tq),
        in_specs=[
            pl.BlockSpec((None, tq, gw), lambda b, g, i: (b, i, g)),
            pl.BlockSpec((None, None, ns, tq), lambda b, g, i: (b, g, 0, i)),
            pl.BlockSpec((None, None, S, HEAD_DIM + ns), lambda b, g, i: (b, g, 0, 0)),
            pl.BlockSpec((None, None, SLC_VROWS, S), lambda b, g, i: (b, g, 0, 0)),
        ],
        out_specs=pl.BlockSpec((None, tq, gw), lambda b, g, i: (b, i, g)),
        out_shape=jax.ShapeDtypeStruct((B, S, B_Q), BF16),
        scratch_shapes=[pltpu.VMEM((HEAD_DIM + ns, rep * tq), BF16),
                        pltpu.VMEM((1, rep * tq), F32), pltpu.VMEM((SLC_VROWS, rep * tq), F32),
                        pltpu.VMEM((SLC_TK, rep * tq), F32), pltpu.VMEM((SLC_TK, rep * tq), F32)],
        compiler_params=_cparams(3), name="slc_attn",
    )(nat_b, selt, k_aug, vt_aug)


SUB = 128


def _band_kernel(q_ref, kp_ref, kc_ref, vp_ref, vc_ref, o_ref, *lse_refs, shared_kv, prev, max_off):
    i = pl.program_id(1)
    tq = q_ref.shape[0]
    n_slab = q_ref.shape[1] // LANES
    width = prev + SUB
    r_i = lax.broadcasted_iota(jnp.int32, (SUB, width), 0)
    c_i = lax.broadcasted_iota(jnp.int32, (SUB, width), 1)
    diff = r_i - c_i + prev
    in_band = (diff >= 0) & (diff <= max_off)
    lane = lax.broadcasted_iota(jnp.int32, (1, LANES), 1)
    lo_half = lane < HEAD_DIM
    half_mask = {"lo": lo_half, "hi": jnp.logical_not(lo_half)}

    def window(p_ref, c_ref, sub, slab):
        cols = slice(slab * LANES, (slab + 1) * LANES)
        lo = sub * SUB
        parts = []
        if lo < prev:
            parts.append(p_ref[lo:prev, cols])
        parts.append(c_ref[max(0, lo - prev):lo + SUB, cols])
        return jnp.concatenate(parts, axis=0) if len(parts) > 1 else parts[0]

    def swap_halves(x):
        return jnp.concatenate([x[:, HEAD_DIM:], x[:, :HEAD_DIM]], axis=1)

    for sub in range(tq // SUB):
        rows = slice(sub * SUB, (sub + 1) * SUB)
        k_start = i * tq + sub * SUB - prev
        bias = jnp.where(in_band & (c_i + k_start >= 0), 0.0, NEG)
        if shared_kv:
            k2 = window(kp_ref, kc_ref, sub, 0)
            v2 = window(vp_ref, vc_ref, sub, 0)
            groups = [
                (((0, "lo"), (1, "lo"), (2, "hi"), (3, "hi")), k2, v2),
                (((0, "hi"), (1, "hi"), (2, "lo"), (3, "lo")), swap_halves(k2), swap_halves(v2)),
            ]
        else:
            groups = [(((p, "lo"), (p, "hi")), window(kp_ref, kc_ref, sub, p),
                       window(vp_ref, vc_ref, sub, p)) for p in range(n_slab)]
        outs = {}
        for members, k2, v2 in groups:
            nm = len(members)
            zero = jnp.zeros((), BF16)
            lhs = jnp.concatenate(
                [jnp.where(half_mask[half], q_ref[rows, slab * LANES:(slab + 1) * LANES], zero)
                 for slab, half in members], axis=0)
            s = _dot_nt(lhs, k2)
            s = (s.reshape(nm, SUB, width) + bias[None]).reshape(nm * SUB, width)
            m = jnp.max(s, axis=-1, keepdims=True)
            p = jnp.exp2(s - m)
            den = jnp.sum(p, axis=-1, keepdims=True)
            pv = _dot(p.astype(BF16), v2) / den
            lse = m + jnp.log2(den)
            for idx, member in enumerate(members):
                outs[member] = (pv[idx * SUB:(idx + 1) * SUB], lse[idx * SUB:(idx + 1) * SUB])
        lse_tile = jnp.zeros((SUB, LANES), F32)
        for slab in range(n_slab):
            cols = slice(slab * LANES, (slab + 1) * LANES)
            (o_lo, l_lo), (o_hi, l_hi) = outs[(slab, "lo")], outs[(slab, "hi")]
            o_ref[rows, cols] = jnp.where(lo_half, o_lo, o_hi).astype(o_ref.dtype)
            lse_tile = jnp.where(lane == 2 * slab, l_lo,
                                 jnp.where(lane == 2 * slab + 1, l_hi, lse_tile))
        if lse_refs:
            lse_refs[0][rows, :] = lse_tile


def _band_attn(q_arr, k_arr, v_arr, *, q_blk, k_blk, v_blk, hq, hk, max_off, tq, with_lse):
    N, L, _ = q_arr.shape
    assert hq == hk or (hq, hk) == (8, 2), (hq, hk)
    prev = -(-max_off // SUB) * SUB
    m = tq // prev
    qw, kw = hq * HEAD_DIM, hk * HEAD_DIM
    cur = lambda blk: (lambda n, i: (n, i, blk))
    prv = lambda blk: (lambda n, i: (n, jnp.maximum(i * m - 1, 0), blk))
    out_shape = [jax.ShapeDtypeStruct((N, L, qw), BF16)]
    out_specs = [pl.BlockSpec((None, tq, qw), cur(0))]
    if with_lse:
        out_shape.append(jax.ShapeDtypeStruct((N, L, LANES), F32))
        out_specs.append(pl.BlockSpec((None, tq, LANES), cur(0)))
    return pl.pallas_call(
        functools.partial(_band_kernel, shared_kv=hq != hk, prev=prev, max_off=max_off),
        grid=(N, L // tq),
        in_specs=[
            pl.BlockSpec((None, tq, qw), cur(q_blk)),
            pl.BlockSpec((None, prev, kw), prv(k_blk)),
            pl.BlockSpec((None, tq, kw), cur(k_blk)),
            pl.BlockSpec((None, prev, kw), prv(v_blk)),
            pl.BlockSpec((None, tq, kw), cur(v_blk)),
        ],
        out_specs=tuple(out_specs), out_shape=tuple(out_shape),
        compiler_params=_cparams(2), name="band_attn",
    )(q_arr, k_arr, k_arr, v_arr, v_arr)


FF_CHUNK = 512

def _merge_ffn_kernel(x_ref, o0_ref, l0_ref, o1_ref, l1_ref, o2_ref, l2_ref, oc_ref, os_ref, ow_ref,
                      gate_ref, ex_ref, wa_ref, wb_ref, wo_ref, gm_ref, wu_ref, wd_ref, gf_ref,
                      out_ref, s_o1, s_l1, s_o2, s_l2, *, final_norm):
    T = x_ref.shape[0]
    for d, o_ref, l_ref, s_o, s_l in ((4, o1_ref, l1_ref, s_o1, s_l1), (16, o2_ref, l2_ref, s_o2, s_l2)):
        rows = T // d
        for r in range(d):
            o_r = o_ref[r].astype(F32)
            s_l[pl.ds(r, rows, stride=d), :] = l_ref[r]
            for k in range(s_o.shape[0]):
                s_o[k, pl.ds(r, rows, stride=d), :] = o_r[:, k * LANES:(k + 1) * LANES]
    gather = lambda s: jnp.concatenate([s[k] for k in range(s.shape[0])], axis=1)

    l0, l1, l2 = l0_ref[...], s_l1[...], s_l2[...]
    mx = jnp.maximum(jnp.maximum(l0, l1), l2)
    e0, e1, e2 = jnp.exp2(l0 - mx), jnp.exp2(l1 - mx), jnp.exp2(l2 - mx)
    inv = 1.0 / (e0 + e1 + e2)
    lo_half = lax.broadcasted_iota(jnp.int32, (1, LANES), 1) < HEAD_DIM

    def spread(w):
        col = lambda h: jnp.broadcast_to(w[:, h:h + 1], (T, LANES))
        return jnp.concatenate([jnp.where(lo_half, col(2 * p), col(2 * p + 1))
                                for p in range(A_SLOTS // 2)], axis=1)

    y_a = (spread(e0 * inv) * o0_ref[...].astype(F32) + spread(e1 * inv) * gather(s_o1)
           + spread(e2 * inv) * gather(s_o2))

    gbx = _dot(gate_ref[:, MERGE_GATE:], ex_ref[...])
    y_b = (gbx[:, :B_Q] * oc_ref[...].astype(F32)
           + gbx[:, B_Q:2 * B_Q] * os_ref[...].astype(F32)
           + gbx[:, 2 * B_Q:] * ow_ref[...].astype(F32))
    merged = (gate_ref[:, :D_MODEL].astype(F32) * _dot(y_a.astype(BF16), wa_ref[...])
              + gate_ref[:, D_MODEL:MERGE_GATE].astype(F32) * _dot(y_b.astype(BF16), wb_ref[...]))
    x1 = x_ref[...] + _dot(merged.astype(BF16), wo_ref[...])

    u = _rms(x1, gm_ref[...]).astype(BF16)
    acc = x1
    for c in range(D_FF // FF_CHUNK):
        cols = slice(c * FF_CHUNK, (c + 1) * FF_CHUNK)
        h = jnp.maximum(_dot(u, wu_ref[:, cols]), 0.0)
        acc = acc + _dot((h * h).astype(BF16), wd_ref[cols, :])
    out_ref[...] = _rms(acc, gf_ref[...]) if final_norm else acc


def _merge_ffn(x, o0, l0, o1, l1, o2, l2, o_cmp, o_slc, o_win, gates, ex, wa, wb, wo,
               g_mlp, w_up, w_down, g_final, final_norm):
    B, S, D = x.shape
    T = ROW_TILE
    qw = A_SLOTS * HEAD_DIM
    nat = lambda w: pl.BlockSpec((None, T, w), lambda b, i: (b, i, 0))
    dil = lambda d, w: pl.BlockSpec((d, T // d, w), lambda b, i: (b, i, 0))
    full = lambda a: pl.BlockSpec(a.shape, lambda b, i: (0,) * a.ndim,
                                  pipeline_mode=pl.Buffered(1))
    o_scr, l_scr = pltpu.VMEM((qw // LANES, T, LANES), F32), pltpu.VMEM((T, LANES), F32)
    return pl.pallas_call(
        functools.partial(_merge_ffn_kernel, final_norm=final_norm), grid=(B, S // T),
        in_specs=[nat(D), nat(qw), nat(LANES), dil(4, qw), dil(4, LANES), dil(16, qw),
                  dil(16, LANES), nat(B_Q), nat(B_Q), nat(B_Q), nat(G_COLS), full(ex), full(wa),
                  full(wb), full(wo), full(g_mlp), full(w_up), full(w_down), full(g_final)],
        out_specs=nat(D), out_shape=jax.ShapeDtypeStruct((B, S, D), F32),
        scratch_shapes=[o_scr, l_scr, o_scr, l_scr],
        compiler_params=_cparams(2), name="merge_ffn",
    )(x, o0, l0, o1, l1, o2, l2, o_cmp, o_slc, o_win, gates, ex, wa, wb, wo,
      g_mlp, w_up, w_down, g_final)


def _rope_table(seq):
    half = ROT_DIM // 2
    pos = jnp.arange(seq, dtype=F32)[:, None]
    lane = jnp.arange(LANES, dtype=jnp.int32) % HEAD_DIM
    inv = ROPE_THETA ** (-(2 * (lane % half)).astype(F32) / ROT_DIM)
    ang = pos * inv[None, :]
    cos, sin = jnp.cos(ang), jnp.sin(ang)
    return jnp.stack([
        jnp.where(lane < ROT_DIM, cos, 1.0),
        jnp.where(lane < half, -sin, 0.0),
        jnp.where((lane >= half) & (lane < ROT_DIM), sin, 0.0),
    ])


def _layer_weights(w_in):
    o3 = A_QKV + B_COLS
    o4 = o3 + B_GATE
    w_main = w_in[:, :o3].astype(BF16)
    w_gate = jnp.concatenate(
        [w_in[:, o4:], jnp.pad(w_in[:, o3:o4], ((0, 0), (0, G_PAD - B_GATE)))], axis=1).astype(BF16)
    return w_main, w_gate


def _overlap_matrix(nc_pad, ns):
    c_start = jnp.arange(nc_pad) * CMP_STRIDE
    s_start = jnp.arange(ns) * SEL_LEN
    ov = (c_start[:, None] < s_start[None, :] + SEL_LEN) & (c_start[:, None] + CMP_LEN > s_start[None, :])
    return ov.astype(BF16)


def _gate_expand():
    rows = jnp.arange(G_PAD)[:, None]
    cols = jnp.arange(3 * B_Q)[None, :]
    br, h = cols // B_Q, (cols % B_Q) // HEAD_DIM
    return (rows == h * 3 + br).astype(BF16)


def _layer(x, g_mix, w_in, cmp_pos_k, cmp_w1_k, cmp_w2_k, cmp_pos_v, cmp_w1_v, cmp_w2_v,
           w_branch_a, w_branch_b, w_out, g_mlp, w_up, w_down, g_final, final_norm, tab):
    B, S, D = x.shape
    G = B_KV
    a0, a1, a2, nat_b, cmp_in, gates = _in_proj(x, g_mix.reshape(1, D), *_layer_weights(w_in), tab)

    a_out = []
    for arr, (w, d) in zip((a0, a1, a2), DIL_PAIRS):
        a_out.append(_band_attn(arr, arr, arr, q_blk=0, k_blk=1, v_blk=2, hq=A_SLOTS, hk=A_SLOTS,
                                max_off=w // d, tq=256, with_lse=True))
    (o0, l0), (o1, l1), (o2, l2) = a_out

    nch = S // CMP_STRIDE
    kv_w = G * HEAD_DIM
    k_slc = nat_b[:, :, B_Q + 2 * kv_w:B_Q + 3 * kv_w].reshape(B, S, G, HEAD_DIM)
    v_slc = nat_b[:, :, B_Q + 3 * kv_w:B_Q + 4 * kv_w].reshape(B, S, G, HEAD_DIM)
    pos = jnp.stack([cmp_pos_k, cmp_pos_v]).astype(BF16)[:, :, None, :]
    pos = jnp.pad(pos, ((0, 0), (0, 0), (0, G - 1), (0, 0))).reshape(2, 1, CMP_LEN * G * HEAD_DIM)
    pos = jnp.broadcast_to(pos, (2, 8, CMP_LEN * G * HEAD_DIM))
    w1 = jnp.stack([cmp_w1_k, cmp_w1_v]).astype(BF16).reshape(2, CMP_LEN, 1, HEAD_DIM, CMP_HIDDEN)
    w1 = jnp.broadcast_to(w1, (2, CMP_LEN, G, HEAD_DIM, CMP_HIDDEN)).reshape(
        2, CMP_LEN * G * HEAD_DIM, CMP_HIDDEN)
    kcvc = _compress(cmp_in, pos, w1, jnp.stack([cmp_w2_k, cmp_w2_v]).astype(BF16))
    ns = S // SEL_LEN
    o_cmp, selb = _cmp_attn(nat_b, kcvc[0], kcvc[1].transpose(0, 1, 3, 2),
                            _overlap_matrix(nch, ns).T)

    onehot = ((jnp.arange(S) // SEL_LEN)[:, None] == jnp.arange(ns)[None, :]).astype(BF16)
    k_aug = jnp.concatenate([k_slc.transpose(0, 2, 1, 3),
                             jnp.broadcast_to(onehot, (B, G, S, ns))], axis=3)
    ones_row = (jnp.arange(SLC_VROWS - HEAD_DIM) == 0).astype(BF16)[:, None]
    vt_aug = jnp.concatenate(
        [v_slc.transpose(0, 2, 3, 1),
         jnp.broadcast_to(ones_row, (B, G, SLC_VROWS - HEAD_DIM, S))], axis=2)
    o_slc = _slc_attn(nat_b, selb, k_aug, vt_aug)

    kw_blk = (B_Q + 4 * G * HEAD_DIM) // (G * HEAD_DIM)
    (o_win,) = _band_attn(nat_b, nat_b, nat_b, q_blk=0, k_blk=kw_blk, v_blk=kw_blk + 1,
                          hq=B_HEADS, hk=B_KV, max_off=WIN_LEN - 1, tq=512, with_lse=False)

    return _merge_ffn(x, o0, l0, o1, l1, o2, l2, o_cmp, o_slc, o_win, gates, _gate_expand(),
                      w_branch_a.astype(BF16), w_branch_b.astype(BF16), w_out.astype(BF16),
                      g_mlp.reshape(1, D), w_up.astype(BF16), w_down.astype(BF16),
                      g_final.reshape(1, D), final_norm)


def kernel(x, norm_mix_g, w_in, cmp_pos_k, cmp_w1_k, cmp_w2_k, cmp_pos_v, cmp_w1_v, cmp_w2_v,
           w_branch_a, w_branch_b, w_out, norm_mlp_g, w_up, w_down, norm_final_g):
    B, S, D = x.shape
    depth = w_in.shape[0]
    tab = _rope_table(S)
    for l in range(depth):
        x = _layer(x, norm_mix_g[l], w_in[l], cmp_pos_k[l], cmp_w1_k[l], cmp_w2_k[l],
                   cmp_pos_v[l], cmp_w1_v[l], cmp_w2_v[l], w_branch_a[l], w_branch_b[l],
                   w_out[l], norm_mlp_g[l], w_up[l], w_down[l], norm_final_g, l == depth - 1, tab)
    return x
```

```python
import functools

import jax
import jax.numpy as jnp
from jax import lax
from jax.experimental import pallas as pl
from jax.experimental.pallas import tpu as pltpu

F32 = jnp.float32
BF16 = jnp.bfloat16

D_MODEL = 1024
HEAD_DIM = 64
ROT_DIM = HEAD_DIM // 4
ROPE_THETA = 500000.0
EPS = 1e-6
DIL_PAIRS = ((128, 1), (512, 4), (2048, 16))
A_SLOTS = 8
A_HEADS = A_SLOTS * len(DIL_PAIRS)
B_HEADS = 8
B_KV = 2
CMP_LEN = 32
CMP_STRIDE = 16
CMP_HIDDEN = 4 * HEAD_DIM
SEL_LEN = 64
SEL_TOP = 16
WIN_LEN = 512
FORCE = 1e4
D_FF = 4 * D_MODEL
A_QKV = 3 * A_HEADS * HEAD_DIM
B_Q = B_HEADS * HEAD_DIM
B_KV_COLS = 6 * B_KV * HEAD_DIM
B_GATE = 3 * B_HEADS
MERGE_GATE = 2 * D_MODEL

LANES = 128
NEG = -1e30
QSCALE = HEAD_DIM ** -0.5 * 1.4426950408889634

A_COLS = 3 * A_SLOTS * HEAD_DIM
B_COLS = B_Q + B_KV_COLS
PROJ_CHUNK = 256
G_PAD = PROJ_CHUNK
G_COLS = MERGE_GATE + G_PAD
W_COLS = 3 * A_COLS + B_COLS + G_COLS

ROW_TILE = 512
VMEM_LIMIT = 56 * 1024 * 1024


def _cparams(n_axes, vmem=VMEM_LIMIT):
    return pltpu.CompilerParams(dimension_semantics=("arbitrary",) * n_axes, vmem_limit_bytes=vmem)


def _rms(x, g):
    return x * lax.rsqrt(jnp.mean(x * x, axis=-1, keepdims=True) + EPS) * g


def _dot(a, b):
    return jnp.dot(a, b, preferred_element_type=F32)


def _dot_nt(a, b):
    return lax.dot_general(a, b, (((1,), (1,)), ((), ())), preferred_element_type=F32)


def _reduce_rows(x, op):
    while x.shape[0] > 8 and x.shape[0] % 16 == 0:
        half = x.shape[0] // 2
        x = op(x[:half], x[half:])
    red = jnp.max if op is jnp.maximum else jnp.sum
    return red(x, axis=0, keepdims=True)


def _split_bf16(x, parts):
    out = []
    for _ in range(parts - 1):
        hi = x.astype(BF16)
        out.append(hi)
        x = x - hi.astype(F32)
    out.append(x.astype(BF16))
    return out


_B_ROT = (True,) * (B_Q // LANES) + (True, False, True, False, True, False)


def _rotary(x, cos_t, sin_a, sin_b):
    half = ROT_DIM // 2
    return x * cos_t + pltpu.roll(x, LANES - half, 1) * sin_a + pltpu.roll(x, half, 1) * sin_b


def _in_proj_kernel(x_ref, g_ref, w_ref, wg_ref, tab_ref,
                    a0_ref, a1_ref, a2_ref, b_ref, cmp_ref, gate_ref, u_scr):
    T = x_ref.shape[0]
    u = _rms(x_ref[...], g_ref[...])
    nslab = u_scr.shape[0]
    for k in range(nslab):
        u_scr[k] = u[:, k * LANES:(k + 1) * LANES]
    u_nat = u.astype(BF16)

    def dilated(load, d):
        rows = T // d
        return jnp.concatenate([load(pl.ds(r, rows, stride=d)) for r in range(d)], axis=0)

    def u_rows(rows):
        return jnp.concatenate([u_scr[k, rows, :] for k in range(nslab)], axis=1).astype(BF16)

    q_slabs = A_SLOTS * HEAD_DIM // LANES
    per = PROJ_CHUNK // LANES

    def project(lhs, weights, chunk_cols, rot, tables, store, n_q=0):
        for c, w_col in enumerate(chunk_cols):
            res = _dot(lhs, weights[:, w_col:w_col + PROJ_CHUNK])
            for k in range(per):
                slab = res[:, k * LANES:(k + 1) * LANES]
                if rot[c * per + k]:
                    slab = _rotary(slab, tables[0], tables[1], tables[2])
                if c * per + k < n_q:
                    slab = slab * QSCALE
                store((c * per + k) * LANES, slab)

    a_rot = (True,) * (2 * q_slabs) + (False,) * q_slabs
    group_w = A_SLOTS * HEAD_DIM

    def group_cols(g):
        return [t * A_HEADS * HEAD_DIM + g * group_w + c
                for t in range(3) for c in range(0, group_w, PROJ_CHUNK)]

    tab0 = tuple(tab_ref[k] for k in range(3))

    def store_a0(c0, res):
        a0_ref[:, c0:c0 + LANES] = res.astype(BF16)

    project(u_nat, w_ref, group_cols(0), a_rot, tab0, store_a0, n_q=q_slabs)

    for g, (d, a_ref) in enumerate(((4, a1_ref), (16, a2_ref)), start=1):
        rows = T // d
        tabs = tuple(dilated(lambda rs, k=k: tab_ref[k, rs, :], d) for k in range(3))

        def store_ad(c0, res, a_ref=a_ref, d=d, rows=rows):
            for r in range(d):
                a_ref[r, :, c0:c0 + LANES] = res[r * rows:(r + 1) * rows].astype(BF16)

        project(dilated(u_rows, d), w_ref, group_cols(g), a_rot, tabs, store_ad, n_q=q_slabs)

    def store_b(c0, res):
        b_ref[:, c0:c0 + LANES] = res.astype(BF16)
        if B_Q <= c0 < B_Q + 2 * LANES:
            cmp_ref[:, c0 - B_Q:c0 - B_Q + LANES] = res

    project(u_nat, w_ref, range(A_QKV, A_QKV + B_COLS, PROJ_CHUNK), _B_ROT, tab0, store_b,
            n_q=q_slabs)

    def store_g(c0, res):
        gate_ref[:, c0:c0 + LANES] = jax.nn.sigmoid(res).astype(gate_ref.dtype)

    project(u_nat, wg_ref, range(0, G_COLS, PROJ_CHUNK), (False,) * (G_COLS // LANES), tab0, store_g)


def _in_proj(x, g, w_main, w_gate, tab):
    B, S, D = x.shape
    T = ROW_TILE
    nt = S // T
    out_shape = (
        jax.ShapeDtypeStruct((B, S, A_COLS), BF16),
        jax.ShapeDtypeStruct((B * 4, S // 4, A_COLS), BF16),
        jax.ShapeDtypeStruct((B * 16, S // 16, A_COLS), BF16),
        jax.ShapeDtypeStruct((B, S, B_COLS), BF16),
        jax.ShapeDtypeStruct((B, S, 2 * LANES), F32),
        jax.ShapeDtypeStruct((B, S, G_COLS), BF16),
    )
    in_specs = [
        pl.BlockSpec((None, T, D), lambda b, i: (b, i, 0)),
        pl.BlockSpec((1, D), lambda b, i: (0, 0)),
        pl.BlockSpec(w_main.shape, lambda b, i: (0, 0), pipeline_mode=pl.Buffered(1)),
        pl.BlockSpec(w_gate.shape, lambda b, i: (0, 0), pipeline_mode=pl.Buffered(1)),
        pl.BlockSpec((3, T, LANES), lambda b, i: (0, i, 0)),
    ]
    out_specs = (
        pl.BlockSpec((None, T, A_COLS), lambda b, i: (b, i, 0)),
        pl.BlockSpec((4, T // 4, A_COLS), lambda b, i: (b, i, 0)),
        pl.BlockSpec((16, T // 16, A_COLS), lambda b, i: (b, i, 0)),
        pl.BlockSpec((None, T, B_COLS), lambda b, i: (b, i, 0)),
        pl.BlockSpec((None, T, 2 * LANES), lambda b, i: (b, i, 0)),
        pl.BlockSpec((None, T, G_COLS), lambda b, i: (b, i, 0)),
    )
    return pl.pallas_call(
        _in_proj_kernel, grid=(B, nt), in_specs=in_specs, out_specs=out_specs, out_shape=out_shape,
        scratch_shapes=[pltpu.VMEM((D // LANES, T, LANES), F32)], compiler_params=_cparams(2),
        name="in_proj",
    )(x, g, w_main, w_gate, tab)


def _gelu_tanh(x):
    return 0.5 * x * (1.0 + jnp.tanh(0.7978845608028654 * (x + 0.044715 * (x * x * x))))


def _compress_kernel(x_ref, pos_ref, w1_ref, w2_ref, o_ref):
    nch = x_ref.shape[0] // CMP_STRIDE
    half = CMP_STRIDE * LANES
    chunks = jnp.concatenate(
        [x_ref[pl.ds(j, nch, stride=CMP_STRIDE), :] for j in range(CMP_STRIDE)], axis=1).astype(BF16)
    lane_group = (lax.broadcasted_iota(jnp.int32, (1, half), 1) % LANES) // HEAD_DIM
    pb = _dot(pos_ref[...], w1_ref[...])[0:1]
    for g in range(B_KV):
        c = jnp.where(lane_group == g, chunks, jnp.zeros((), BF16))
        p1 = _dot(c, w1_ref[:half, :])
        p2 = _dot(c, w1_ref[half:, :])
        p2_next = pltpu.roll(p2, nch - 1, 0)
        h = _gelu_tanh(p1 + p2_next + pb)
        o_ref[g] = _dot(h.astype(BF16), w2_ref[...]).astype(BF16)


def _compress(cmp_in, pos, w1, w2):
    B, S, _ = cmp_in.shape
    nch = S // CMP_STRIDE
    kx = 2 * CMP_STRIDE * LANES
    return pl.pallas_call(
        _compress_kernel, grid=(2, B),
        in_specs=[
            pl.BlockSpec((None, S, LANES), lambda t, b: (b, 0, t)),
            pl.BlockSpec((None, 8, kx), lambda t, b: (t, 0, 0)),
            pl.BlockSpec((None, kx, CMP_HIDDEN), lambda t, b: (t, 0, 0)),
            pl.BlockSpec((None, CMP_HIDDEN, HEAD_DIM), lambda t, b: (t, 0, 0)),
        ],
        out_specs=pl.BlockSpec((None, None, B_KV, nch, HEAD_DIM), lambda t, b: (t, b, 0, 0, 0)),
        out_shape=jax.ShapeDtypeStruct((2, B, B_KV, nch, HEAD_DIM), BF16),
        compiler_params=_cparams(2), name="compress",
    )(cmp_in, pos, w1, w2)


CMP_TQ = 256


CMP_KEY_CHUNK = 128


def _cmp_attn_kernel(q_ref, kc_ref, vct_ref, ovt_ref, o_ref, sel_ref, imp_scr):
    i = pl.program_id(1)
    tq = q_ref.shape[0]
    nc = kc_ref.shape[1]
    ns = ovt_ref.shape[0]
    rep = B_HEADS // B_KV
    t = i * tq + lax.broadcasted_iota(jnp.int32, (1, tq), 1)
    live = jnp.where(t >= CMP_LEN - 1, 1.0, 0.0)
    blk = lax.broadcasted_iota(jnp.int32, (ns, 1), 0)
    blk_f = blk.astype(F32)
    cur = jnp.right_shift(t, SEL_LEN.bit_length() - 1)
    forced = (blk == 0) | (blk == cur) | (blk == cur - 1)
    keep = jnp.where(forced | (blk > cur), 0.0, 1.0)
    pinned = jnp.where(forced, FORCE, jnp.where(blk > cur, -FORCE, 0.0))
    removed = -3e38

    def attend(nk):
        n = lax.broadcasted_iota(jnp.int32, (nk, 1), 0)
        hidden = jnp.where((n * CMP_STRIDE + (CMP_LEN - 1)) <= t, 0.0, NEG)
        ovt = ovt_ref[:, :nk]
        o_t = []
        for g in range(B_KV):
            kc = kc_ref[g, :nk, :]
            vct = vct_ref[g, :, :nk]
            psum = jnp.zeros((nk, tq), F32)
            for r in range(rep):
                h = g * rep + r
                qh = q_ref[:, h * HEAD_DIM:(h + 1) * HEAD_DIM]
                s = _dot_nt(kc, qh) + hidden
                p = jnp.exp2(s - _reduce_rows(s, jnp.maximum))
                p = p * (live / _reduce_rows(p, jnp.add))
                o_t.append(_dot(vct, p.astype(BF16)))
                psum = psum + p
            imp = sum(_dot(ovt, piece) for piece in _split_bf16(psum, 3))
            imp_scr[:, g * tq:(g + 1) * tq] = imp * keep + pinned
        o_ref[...] = jnp.transpose(jnp.concatenate(o_t, axis=0)).astype(BF16)

    n_chunks = nc // CMP_KEY_CHUNK
    last_chunk = ((i + 1) * (tq // CMP_STRIDE) - 1) // CMP_KEY_CHUNK
    for c in range(n_chunks):
        pl.when(last_chunk == c)(functools.partial(attend, (c + 1) * CMP_KEY_CHUNK))

    imp = imp_scr[...]
    for _ in range(SEL_TOP):
        mx = jnp.max(imp, axis=0, keepdims=True)
        first = jnp.min(jnp.where(imp == mx, blk_f, float(ns)), axis=0, keepdims=True)
        imp = jnp.where(blk_f == first, removed, imp)
    for g in range(B_KV):
        picked = imp[:, g * tq:(g + 1) * tq] < 0.5 * removed
        sel_ref[g] = jnp.where(picked, 0.0, NEG).astype(BF16)


def _cmp_attn(nat_b, kc, vct, overlap_t):
    B, S, _ = nat_b.shape
    nc = kc.shape[2]
    ns = overlap_t.shape[0]
    tq = CMP_TQ
    return pl.pallas_call(
        _cmp_attn_kernel, grid=(B, S // tq),
        in_specs=[
            pl.BlockSpec((None, tq, B_Q), lambda b, i: (b, i, 0)),
            pl.BlockSpec((None, B_KV, nc, HEAD_DIM), lambda b, i: (b, 0, 0, 0)),
            pl.BlockSpec((None, B_KV, HEAD_DIM, nc), lambda b, i: (b, 0, 0, 0)),
            pl.BlockSpec((ns, nc), lambda b, i: (0, 0)),
        ],
        out_specs=(
            pl.BlockSpec((None, tq, B_Q), lambda b, i: (b, i, 0)),
            pl.BlockSpec((None, B_KV, ns, tq), lambda b, i: (b, 0, 0, i)),
        ),
        out_shape=(
            jax.ShapeDtypeStruct((B, S, B_Q), BF16),
            jax.ShapeDtypeStruct((B, B_KV, ns, S), BF16),
        ),
        scratch_shapes=[pltpu.VMEM((ns, B_KV * tq), F32)],
        compiler_params=_cparams(2), name="cmp_attn",
    )(nat_b, kc, vct, overlap_t)


SLC_TK = 256
SLC_TQ = 2 * SLC_TK


SLC_VROWS = 80


def _slc_attn_kernel(q_ref, selt_ref, k_ref, vt_ref, o_ref, qat_scr, m_scr, acc_scr, s_a, s_b):
    i = pl.program_id(2)
    tq = q_ref.shape[0]
    rep = B_HEADS // B_KV
    selt = selt_ref[...]
    for r in range(rep):
        cols = slice(r * tq, (r + 1) * tq)
        q_r = q_ref[:, r * HEAD_DIM:(r + 1) * HEAD_DIM].astype(F32)
        qat_scr[:HEAD_DIM, cols] = jnp.transpose(q_r).astype(BF16)
        qat_scr[HEAD_DIM:, cols] = selt
    m_scr[...] = jnp.full(m_scr.shape, NEG, F32)
    acc_scr[...] = jnp.zeros(acc_scr.shape, F32)
    rel = (lax.broadcasted_iota(jnp.int32, (SLC_TK, tq), 1)
           - lax.broadcasted_iota(jnp.int32, (SLC_TK, tq), 0))

    def scores(j, s_buf):
        k0 = pl.multiple_of(j * SLC_TK, SLC_TK)
        s_buf[...] = _dot(k_ref[pl.ds(k0, SLC_TK), :], qat_scr[...])

    def update(j, s_buf, diag_offset=None):
        k0 = pl.multiple_of(j * SLC_TK, SLC_TK)
        vt = vt_ref[:, pl.ds(k0, SLC_TK)]
        for r in range(rep):
            cols = slice(r * tq, (r + 1) * tq)
            s = s_buf[:, cols]
            if diag_offset is not None:
                s = jnp.where(rel >= diag_offset, s, NEG)
            m_prev = m_scr[:, cols]
            m_new = jnp.maximum(m_prev, jnp.max(s, axis=0, keepdims=True))
            alpha = jnp.exp2(m_prev - m_new)
            p = jnp.exp2(s - m_new).astype(BF16)
            acc_scr[:, cols] = alpha * acc_scr[:, cols] + _dot(vt, p)
            m_scr[:, cols] = m_new

    def tile_pair(jj, carry):
        a = 2 * jj
        scores(a + 1, s_b)
        update(a, s_a)
        scores(a + 2, s_a)
        update(a + 1, s_b)
        return carry

    scores(0, s_a)
    lax.fori_loop(0, i, tile_pair, 0)
    scores(2 * i + 1, s_b)
    update(2 * i, s_a, diag_offset=0)
    update(2 * i + 1, s_b, diag_offset=SLC_TK)

    for r in range(rep):
        acc = acc_scr[:, r * tq:(r + 1) * tq]
        o_t = acc[:HEAD_DIM] / acc[HEAD_DIM:HEAD_DIM + 1]
        o_ref[:, r * HEAD_DIM:(r + 1) * HEAD_DIM] = jnp.transpose(o_t).astype(BF16)


def _slc_attn(nat_b, selt, k_aug, vt_aug):
    B, S, _ = nat_b.shape
    G = B_KV
    rep = B_HEADS // B_KV
    ns = selt.shape[2]
    tq = SLC_TQ
    gw = rep * HEAD_DIM
    return pl.pallas_call(
        _slc_attn_kernel, grid=(B, G, S // tq),
        in_specs=[
            pl.BlockSpec((None, tq, gw), lambda b, g, i: (b, i, g)),
            pl.BlockSpec((None, None, ns, tq), lambda b, g, i: (b, g, 0, i)),
            pl.BlockSpec((None, None, S, HEAD_DIM + ns), lambda b, g, i: (b, g, 0, 0)),
            pl.BlockSpec((None, None, SLC_VROWS, S), lambda b, g, i: (b, g, 0, 0)),
        ],
        out_specs=pl.BlockSpec((None, tq, gw), lambda b, g, i: (b, i, g)),
        out_shape=jax.ShapeDtypeStruct((B, S, B_Q), BF16),
        scratch_shapes=[pltpu.VMEM((HEAD_DIM + ns, rep * tq), BF16),
                        pltpu.VMEM((1, rep * tq), F32), pltpu.VMEM((SLC_VROWS, rep * tq), F32),
                        pltpu.VMEM((SLC_TK, rep * tq), F32), pltpu.VMEM((SLC_TK, rep * tq), F32)],
        compiler_params=_cparams(3), name="slc_attn",
    )(nat_b, selt, k_aug, vt_aug)


SUB = 128


def _band_kernel(q_ref, kp_ref, kc_ref, vp_ref, vc_ref, o_ref, *lse_refs, shared_kv, prev, max_off):
    i = pl.program_id(1)
    tq = q_ref.shape[0]
    n_slab = q_ref.shape[1] // LANES
    width = prev + SUB
    r_i = lax.broadcasted_iota(jnp.int32, (SUB, width), 0)
    c_i = lax.broadcasted_iota(jnp.int32, (SUB, width), 1)
    diff = r_i - c_i + prev
    in_band = (diff >= 0) & (diff <= max_off)
    lane = lax.broadcasted_iota(jnp.int32, (1, LANES), 1)
    lo_half = lane < HEAD_DIM
    half_mask = {"lo": lo_half, "hi": jnp.logical_not(lo_half)}

    def window(p_ref, c_ref, sub, slab):
        cols = slice(slab * LANES, (slab + 1) * LANES)
        lo = sub * SUB
        parts = []
        if lo < prev:
            parts.append(p_ref[lo:prev, cols])
        parts.append(c_ref[max(0, lo - prev):lo + SUB, cols])
        return jnp.concatenate(parts, axis=0) if len(parts) > 1 else parts[0]

    def swap_halves(x):
        return jnp.concatenate([x[:, HEAD_DIM:], x[:, :HEAD_DIM]], axis=1)

    for sub in range(tq // SUB):
        rows = slice(sub * SUB, (sub + 1) * SUB)
        k_start = i * tq + sub * SUB - prev
        bias = jnp.where(in_band & (c_i + k_start >= 0), 0.0, NEG)
        if shared_kv:
            k2 = window(kp_ref, kc_ref, sub, 0)
            v2 = window(vp_ref, vc_ref, sub, 0)
            groups = [
                (((0, "lo"), (1, "lo"), (2, "hi"), (3, "hi")), k2, v2),
                (((0, "hi"), (1, "hi"), (2, "lo"), (3, "lo")), swap_halves(k2), swap_halves(v2)),
            ]
        else:
            groups = [(((p, "lo"), (p, "hi")), window(kp_ref, kc_ref, sub, p),
                       window(vp_ref, vc_ref, sub, p)) for p in range(n_slab)]
        outs = {}
        for members, k2, v2 in groups:
            nm = len(members)
            zero = jnp.zeros((), BF16)
            lhs = jnp.concatenate(
                [jnp.where(half_mask[half], q_ref[rows, slab * LANES:(slab + 1) * LANES], zero)
                 for slab, half in members], axis=0)
            s = _dot_nt(lhs, k2)
            s = (s.reshape(nm, SUB, width) + bias[None]).reshape(nm * SUB, width)
            m = jnp.max(s, axis=-1, keepdims=True)
            p = jnp.exp2(s - m)
            den = jnp.sum(p, axis=-1, keepdims=True)
            pv = _dot(p.astype(BF16), v2) / den
            lse = m + jnp.log2(den)
            for idx, member in enumerate(members):
                outs[member] = (pv[idx * SUB:(idx + 1) * SUB], lse[idx * SUB:(idx + 1) * SUB])
        lse_tile = jnp.zeros((SUB, LANES), F32)
        for slab in range(n_slab):
            cols = slice(slab * LANES, (slab + 1) * LANES)
            (o_lo, l_lo), (o_hi, l_hi) = outs[(slab, "lo")], outs[(slab, "hi")]
            o_ref[rows, cols] = jnp.where(lo_half, o_lo, o_hi).astype(o_ref.dtype)
            lse_tile = jnp.where(lane == 2 * slab, l_lo,
                                 jnp.where(lane == 2 * slab + 1, l_hi, lse_tile))
        if lse_refs:
            lse_refs[0][rows, :] = lse_tile


def _band_attn(q_arr, k_arr, v_arr, *, q_blk, k_blk, v_blk, hq, hk, max_off, tq, with_lse):
    N, L, _ = q_arr.shape
    assert hq == hk or (hq, hk) == (8, 2), (hq, hk)
    prev = -(-max_off // SUB) * SUB
    m = tq // prev
    qw, kw = hq * HEAD_DIM, hk * HEAD_DIM
    cur = lambda blk: (lambda n, i: (n, i, blk))
    prv = lambda blk: (lambda n, i: (n, jnp.maximum(i * m - 1, 0), blk))
    out_shape = [jax.ShapeDtypeStruct((N, L, qw), BF16)]
    out_specs = [pl.BlockSpec((None, tq, qw), cur(0))]
    if with_lse:
        out_shape.append(jax.ShapeDtypeStruct((N, L, LANES), F32))
        out_specs.append(pl.BlockSpec((None, tq, LANES), cur(0)))
    return pl.pallas_call(
        functools.partial(_band_kernel, shared_kv=hq != hk, prev=prev, max_off=max_off),
        grid=(N, L // tq),
        in_specs=[
            pl.BlockSpec((None, tq, qw), cur(q_blk)),
            pl.BlockSpec((None, prev, kw), prv(k_blk)),
            pl.BlockSpec((None, tq, kw), cur(k_blk)),
            pl.BlockSpec((None, prev, kw), prv(v_blk)),
            pl.BlockSpec((None, tq, kw), cur(v_blk)),
        ],
        out_specs=tuple(out_specs), out_shape=tuple(out_shape),
        compiler_params=_cparams(2), name="band_attn",
    )(q_arr, k_arr, k_arr, v_arr, v_arr)


FF_CHUNK = 512

def _merge_ffn_kernel(x_ref, o0_ref, l0_ref, o1_ref, l1_ref, o2_ref, l2_ref, oc_ref, os_ref, ow_ref,
                      gate_ref, ex_ref, wa_ref, wb_ref, wo_ref, gm_ref, wu_ref, wd_ref, gf_ref,
                      out_ref, s_o1, s_l1, s_o2, s_l2, x1_scr, *, final_norm):
    step = pl.program_id(0)
    T = x_ref.shape[0]

    @pl.when(step == 0)
    def _():
        x1_scr[1] = jnp.zeros(x1_scr.shape[1:], F32)

    x1_prev = x1_scr[(step + 1) % 2]
    u = _rms(x1_prev, gm_ref[...]).astype(BF16)
    acc = x1_prev
    for c in range(D_FF // FF_CHUNK):
        cols = slice(c * FF_CHUNK, (c + 1) * FF_CHUNK)
        h = jnp.maximum(_dot(u, wu_ref[:, cols]), 0.0)
        acc = acc + _dot((h * h).astype(BF16), wd_ref[cols, :])
    out_ref[...] = _rms(acc, gf_ref[...]) if final_norm else acc

    for d, o_ref, l_ref, s_o, s_l in ((4, o1_ref, l1_ref, s_o1, s_l1), (16, o2_ref, l2_ref, s_o2, s_l2)):
        rows = T // d
        for r in range(d):
            o_r = o_ref[r].astype(F32)
            s_l[pl.ds(r, rows, stride=d), :] = l_ref[r]
            for k in range(s_o.shape[0]):
                s_o[k, pl.ds(r, rows, stride=d), :] = o_r[:, k * LANES:(k + 1) * LANES]
    gather = lambda s: jnp.concatenate([s[k] for k in range(s.shape[0])], axis=1)

    l0, l1, l2 = l0_ref[...], s_l1[...], s_l2[...]
    mx = jnp.maximum(jnp.maximum(l0, l1), l2)
    e0, e1, e2 = jnp.exp2(l0 - mx), jnp.exp2(l1 - mx), jnp.exp2(l2 - mx)
    inv = 1.0 / (e0 + e1 + e2)
    lo_half = lax.broadcasted_iota(jnp.int32, (1, LANES), 1) < HEAD_DIM

    def spread(w):
        col = lambda h: jnp.broadcast_to(w[:, h:h + 1], (T, LANES))
        return jnp.concatenate([jnp.where(lo_half, col(2 * p), col(2 * p + 1))
                                for p in range(A_SLOTS // 2)], axis=1)

    y_a = (spread(e0 * inv) * o0_ref[...].astype(F32) + spread(e1 * inv) * gather(s_o1)
           + spread(e2 * inv) * gather(s_o2))

    gbx = _dot(gate_ref[:, MERGE_GATE:], ex_ref[...])
    y_b = (gbx[:, :B_Q] * oc_ref[...].astype(F32)
           + gbx[:, B_Q:2 * B_Q] * os_ref[...].astype(F32)
           + gbx[:, 2 * B_Q:] * ow_ref[...].astype(F32))
    merged = (gate_ref[:, :D_MODEL].astype(F32) * _dot(y_a.astype(BF16), wa_ref[...])
              + gate_ref[:, D_MODEL:MERGE_GATE].astype(F32) * _dot(y_b.astype(BF16), wb_ref[...]))
    x1_scr[step % 2] = x_ref[...] + _dot(merged.astype(BF16), wo_ref[...])


def _merge_ffn(x, o0, l0, o1, l1, o2, l2, o_cmp, o_slc, o_win, gates, ex, wa, wb, wo,
               g_mlp, w_up, w_down, g_final, final_norm):
    B, S, D = x.shape
    T = ROW_TILE
    nt = S // T
    n_tiles = B * nt
    qw = A_SLOTS * HEAD_DIM

    def tile_map(shift):
        def index_map(k):
            m = jnp.clip(k - shift, 0, n_tiles - 1)
            return (m // nt, m % nt, 0)
        return index_map

    nat = lambda w, shift=0: pl.BlockSpec((None, T, w), tile_map(shift))
    dil = lambda d, w: pl.BlockSpec((d, T // d, w), tile_map(0))
    full = lambda a: pl.BlockSpec(a.shape, lambda k: (0,) * a.ndim, pipeline_mode=pl.Buffered(1))
    o_scr, l_scr = pltpu.VMEM((qw // LANES, T, LANES), F32), pltpu.VMEM((T, LANES), F32)
    return pl.pallas_call(
        functools.partial(_merge_ffn_kernel, final_norm=final_norm), grid=(n_tiles + 1,),
        in_specs=[nat(D), nat(qw), nat(LANES), dil(4, qw), dil(4, LANES), dil(16, qw),
                  dil(16, LANES), nat(B_Q), nat(B_Q), nat(B_Q), nat(G_COLS), full(ex), full(wa),
                  full(wb), full(wo), full(g_mlp), full(w_up), full(w_down), full(g_final)],
        out_specs=nat(D, shift=1), out_shape=jax.ShapeDtypeStruct((B, S, D), F32),
        scratch_shapes=[o_scr, l_scr, o_scr, l_scr, pltpu.VMEM((2, T, D), F32)],
        compiler_params=_cparams(1), name="merge_ffn",
    )(x, o0, l0, o1, l1, o2, l2, o_cmp, o_slc, o_win, gates, ex, wa, wb, wo,
      g_mlp, w_up, w_down, g_final)


def _rope_table(seq):
    half = ROT_DIM // 2
    pos = jnp.arange(seq, dtype=F32)[:, None]
    lane = jnp.arange(LANES, dtype=jnp.int32) % HEAD_DIM
    inv = ROPE_THETA ** (-(2 * (lane % half)).astype(F32) / ROT_DIM)
    ang = pos * inv[None, :]
    cos, sin = jnp.cos(ang), jnp.sin(ang)
    return jnp.stack([
        jnp.where(lane < ROT_DIM, cos, 1.0),
        jnp.where(lane < half, -sin, 0.0),
        jnp.where((lane >= half) & (lane < ROT_DIM), sin, 0.0),
    ])


def _layer_weights(w_in):
    o3 = A_QKV + B_COLS
    o4 = o3 + B_GATE
    w_main = w_in[:, :o3].astype(BF16)
    w_gate = jnp.concatenate(
        [w_in[:, o4:], jnp.pad(w_in[:, o3:o4], ((0, 0), (0, G_PAD - B_GATE)))], axis=1).astype(BF16)
    return w_main, w_gate


def _overlap_matrix(nc_pad, ns):
    c_start = jnp.arange(nc_pad) * CMP_STRIDE
    s_start = jnp.arange(ns) * SEL_LEN
    ov = (c_start[:, None] < s_start[None, :] + SEL_LEN) & (c_start[:, None] + CMP_LEN > s_start[None, :])
    return ov.astype(BF16)


def _gate_expand():
    rows = jnp.arange(G_PAD)[:, None]
    cols = jnp.arange(3 * B_Q)[None, :]
    br, h = cols // B_Q, (cols % B_Q) // HEAD_DIM
    return (rows == h * 3 + br).astype(BF16)


def _layer(x, g_mix, w_in, cmp_pos_k, cmp_w1_k, cmp_w2_k, cmp_pos_v, cmp_w1_v, cmp_w2_v,
           w_branch_a, w_branch_b, w_out, g_mlp, w_up, w_down, g_final, final_norm, tab):
    B, S, D = x.shape
    G = B_KV
    a0, a1, a2, nat_b, cmp_in, gates = _in_proj(x, g_mix.reshape(1, D), *_layer_weights(w_in), tab)

    a_out = []
    for arr, (w, d) in zip((a0, a1, a2), DIL_PAIRS):
        a_out.append(_band_attn(arr, arr, arr, q_blk=0, k_blk=1, v_blk=2, hq=A_SLOTS, hk=A_SLOTS,
                                max_off=w // d, tq=256, with_lse=True))
    (o0, l0), (o1, l1), (o2, l2) = a_out

    nch = S // CMP_STRIDE
    kv_w = G * HEAD_DIM
    k_slc = nat_b[:, :, B_Q + 2 * kv_w:B_Q + 3 * kv_w].reshape(B, S, G, HEAD_DIM)
    v_slc = nat_b[:, :, B_Q + 3 * kv_w:B_Q + 4 * kv_w].reshape(B, S, G, HEAD_DIM)
    pos = jnp.stack([cmp_pos_k, cmp_pos_v]).astype(BF16)[:, :, None, :]
    pos = jnp.pad(pos, ((0, 0), (0, 0), (0, G - 1), (0, 0))).reshape(2, 1, CMP_LEN * G * HEAD_DIM)
    pos = jnp.broadcast_to(pos, (2, 8, CMP_LEN * G * HEAD_DIM))
    w1 = jnp.stack([cmp_w1_k, cmp_w1_v]).astype(BF16).reshape(2, CMP_LEN, 1, HEAD_DIM, CMP_HIDDEN)
    w1 = jnp.broadcast_to(w1, (2, CMP_LEN, G, HEAD_DIM, CMP_HIDDEN)).reshape(
        2, CMP_LEN * G * HEAD_DIM, CMP_HIDDEN)
    kcvc = _compress(cmp_in, pos, w1, jnp.stack([cmp_w2_k, cmp_w2_v]).astype(BF16))
    ns = S // SEL_LEN
    o_cmp, selb = _cmp_attn(nat_b, kcvc[0], kcvc[1].transpose(0, 1, 3, 2),
                            _overlap_matrix(nch, ns).T)

    onehot = ((jnp.arange(S) // SEL_LEN)[:, None] == jnp.arange(ns)[None, :]).astype(BF16)
    k_aug = jnp.concatenate([k_slc.transpose(0, 2, 1, 3),
                             jnp.broadcast_to(onehot, (B, G, S, ns))], axis=3)
    ones_row = (jnp.arange(SLC_VROWS - HEAD_DIM) == 0).astype(BF16)[:, None]
    vt_aug = jnp.concatenate(
        [v_slc.transpose(0, 2, 3, 1),
         jnp.broadcast_to(ones_row, (B, G, SLC_VROWS - HEAD_DIM, S))], axis=2)
    o_slc = _slc_attn(nat_b, selb, k_aug, vt_aug)

    kw_blk = (B_Q + 4 * G * HEAD_DIM) // (G * HEAD_DIM)
    (o_win,) = _band_attn(nat_b, nat_b, nat_b, q_blk=0, k_blk=kw_blk, v_blk=kw_blk + 1,
                          hq=B_HEADS, hk=B_KV, max_off=WIN_LEN - 1, tq=512, with_lse=False)

    return _merge_ffn(x, o0, l0, o1, l1, o2, l2, o_cmp, o_slc, o_win, gates, _gate_expand(),
                      w_branch_a.astype(BF16), w_branch_b.astype(BF16), w_out.astype(BF16),
                      g_mlp.reshape(1, D), w_up.astype(BF16), w_down.astype(BF16),
                      g_final.reshape(1, D), final_norm)


def kernel(x, norm_mix_g, w_in, cmp_pos_k, cmp_w1_k, cmp_w2_k, cmp_pos_v, cmp_w1_v, cmp_w2_v,
           w_branch_a, w_branch_b, w_out, norm_mlp_g, w_up, w_down, norm_final_g):
    B, S, D = x.shape
    depth = w_in.shape[0]
    tab = _rope_table(S)
    for l in range(depth):
        x = _layer(x, norm_mix_g[l], w_in[l], cmp_pos_k[l], cmp_w1_k[l], cmp_w2_k[l],
                   cmp_pos_v[l], cmp_w1_v[l], cmp_w2_v[l], w_branch_a[l], w_branch_b[l],
                   w_out[l], norm_mlp_g[l], w_up[l], w_down[l], norm_final_g, l == depth - 1, tab)
    return x
```

```python
import functools

import jax
import jax.numpy as jnp
from jax import lax
from jax.experimental import pallas as pl
from jax.experimental.pallas import tpu as pltpu

F32 = jnp.float32
BF16 = jnp.bfloat16

D_MODEL = 1024
HEAD_DIM = 64
ROT_DIM = HEAD_DIM // 4
ROPE_THETA = 500000.0
EPS = 1e-6
DIL_PAIRS = ((128, 1), (512, 4), (2048, 16))
A_SLOTS = 8
A_HEADS = A_SLOTS * len(DIL_PAIRS)
B_HEADS = 8
B_KV = 2
CMP_LEN = 32
CMP_STRIDE = 16
CMP_HIDDEN = 4 * HEAD_DIM
SEL_LEN = 64
SEL_TOP = 16
WIN_LEN = 512
FORCE = 1e4
D_FF = 4 * D_MODEL
A_QKV = 3 * A_HEADS * HEAD_DIM
B_Q = B_HEADS * HEAD_DIM
B_KV_COLS = 6 * B_KV * HEAD_DIM
B_GATE = 3 * B_HEADS
MERGE_GATE = 2 * D_MODEL

LANES = 128
NEG = -1e30
QSCALE = HEAD_DIM ** -0.5 * 1.4426950408889634

A_COLS = 3 * A_SLOTS * HEAD_DIM
B_COLS = B_Q + B_KV_COLS
PROJ_CHUNK = 256
G_PAD = PROJ_CHUNK
G_COLS = MERGE_GATE + G_PAD

ROW_TILE = 512
VMEM_LIMIT = 56 * 1024 * 1024


def _cparams(n_axes, vmem=VMEM_LIMIT):
    return pltpu.CompilerParams(dimension_semantics=("arbitrary",) * n_axes, vmem_limit_bytes=vmem)


def _rms(x, g):
    return x * lax.rsqrt(jnp.mean(x * x, axis=-1, keepdims=True) + EPS) * g


def _dot(a, b):
    return jnp.dot(a, b, preferred_element_type=F32)


def _dot_nt(a, b):
    return lax.dot_general(a, b, (((1,), (1,)), ((), ())), preferred_element_type=F32)


def _reduce_rows(x, op):
    while x.shape[0] > 8 and x.shape[0] % 16 == 0:
        half = x.shape[0] // 2
        x = op(x[:half], x[half:])
    red = jnp.max if op is jnp.maximum else jnp.sum
    return red(x, axis=0, keepdims=True)


def _split_bf16(x, parts):
    out = []
    for _ in range(parts - 1):
        hi = x.astype(BF16)
        out.append(hi)
        x = x - hi.astype(F32)
    out.append(x.astype(BF16))
    return out


_B_ROT = (True,) * (B_Q // LANES) + (True, False, True, False, True, False)


def _rotary(x, cos_t, sin_a, sin_b):
    half = ROT_DIM // 2
    return x * cos_t + pltpu.roll(x, LANES - half, 1) * sin_a + pltpu.roll(x, half, 1) * sin_b


def _in_proj_kernel(x_ref, g_ref, w_ref, wg_ref, tab_ref,
                    a0_ref, a1_ref, a2_ref, b_ref, cmp_ref, gate_ref, u_scr):
    T = x_ref.shape[0]
    u = _rms(x_ref[...], g_ref[...])
    nslab = u_scr.shape[0]
    for k in range(nslab):
        u_scr[k] = u[:, k * LANES:(k + 1) * LANES]
    u_nat = u.astype(BF16)

    def dilated(load, d):
        rows = T // d
        return jnp.concatenate([load(pl.ds(r, rows, stride=d)) for r in range(d)], axis=0)

    def u_rows(rows):
        return jnp.concatenate([u_scr[k, rows, :] for k in range(nslab)], axis=1).astype(BF16)

    q_slabs = A_SLOTS * HEAD_DIM // LANES
    per = PROJ_CHUNK // LANES

    def project(lhs, weights, chunk_cols, rot, tables, store, n_q=0):
        for c, w_col in enumerate(chunk_cols):
            res = _dot(lhs, weights[:, w_col:w_col + PROJ_CHUNK])
            for k in range(per):
                slab = res[:, k * LANES:(k + 1) * LANES]
                if rot[c * per + k]:
                    slab = _rotary(slab, tables[0], tables[1], tables[2])
                if c * per + k < n_q:
                    slab = slab * QSCALE
                store((c * per + k) * LANES, slab)

    a_rot = (True,) * (2 * q_slabs) + (False,) * q_slabs
    group_w = A_SLOTS * HEAD_DIM

    def group_cols(g):
        return [t * A_HEADS * HEAD_DIM + g * group_w + c
                for t in range(3) for c in range(0, group_w, PROJ_CHUNK)]

    tab0 = tuple(tab_ref[k] for k in range(3))

    def store_a0(c0, res):
        a0_ref[:, c0:c0 + LANES] = res.astype(BF16)

    project(u_nat, w_ref, group_cols(0), a_rot, tab0, store_a0, n_q=q_slabs)

    for g, (d, a_ref) in enumerate(((4, a1_ref), (16, a2_ref)), start=1):
        rows = T // d
        tabs = tuple(dilated(lambda rs, k=k: tab_ref[k, rs, :], d) for k in range(3))

        def store_ad(c0, res, a_ref=a_ref, d=d, rows=rows):
            for r in range(d):
                a_ref[r, :, c0:c0 + LANES] = res[r * rows:(r + 1) * rows].astype(BF16)

        project(dilated(u_rows, d), w_ref, group_cols(g), a_rot, tabs, store_ad, n_q=q_slabs)

    def store_b(c0, res):
        b_ref[:, c0:c0 + LANES] = res.astype(BF16)
        if B_Q <= c0 < B_Q + 2 * LANES:
            cmp_ref[:, c0 - B_Q:c0 - B_Q + LANES] = res

    project(u_nat, w_ref, range(A_QKV, A_QKV + B_COLS, PROJ_CHUNK), _B_ROT, tab0, store_b,
            n_q=q_slabs)

    def store_g(c0, res):
        gate_ref[:, c0:c0 + LANES] = jax.nn.sigmoid(res).astype(gate_ref.dtype)

    project(u_nat, wg_ref, range(0, G_COLS, PROJ_CHUNK), (False,) * (G_COLS // LANES), tab0, store_g)


def _in_proj(x, g, w_main, w_gate, tab):
    B, S, D = x.shape
    T = ROW_TILE
    nt = S // T
    out_shape = (
        jax.ShapeDtypeStruct((B, S, A_COLS), BF16),
        jax.ShapeDtypeStruct((B * 4, S // 4, A_COLS), BF16),
        jax.ShapeDtypeStruct((B * 16, S // 16, A_COLS), BF16),
        jax.ShapeDtypeStruct((B, S, B_COLS), BF16),
        jax.ShapeDtypeStruct((B, S, 2 * LANES), F32),
        jax.ShapeDtypeStruct((B, S, G_COLS), BF16),
    )
    in_specs = [
        pl.BlockSpec((None, T, D), lambda b, i: (b, i, 0)),
        pl.BlockSpec((1, D), lambda b, i: (0, 0)),
        pl.BlockSpec(w_main.shape, lambda b, i: (0, 0), pipeline_mode=pl.Buffered(1)),
        pl.BlockSpec(w_gate.shape, lambda b, i: (0, 0), pipeline_mode=pl.Buffered(1)),
        pl.BlockSpec((3, T, LANES), lambda b, i: (0, i, 0)),
    ]
    out_specs = (
        pl.BlockSpec((None, T, A_COLS), lambda b, i: (b, i, 0)),
        pl.BlockSpec((4, T // 4, A_COLS), lambda b, i: (b, i, 0)),
        pl.BlockSpec((16, T // 16, A_COLS), lambda b, i: (b, i, 0)),
        pl.BlockSpec((None, T, B_COLS), lambda b, i: (b, i, 0)),
        pl.BlockSpec((None, T, 2 * LANES), lambda b, i: (b, i, 0)),
        pl.BlockSpec((None, T, G_COLS), lambda b, i: (b, i, 0)),
    )
    return pl.pallas_call(
        _in_proj_kernel, grid=(B, nt), in_specs=in_specs, out_specs=out_specs, out_shape=out_shape,
        scratch_shapes=[pltpu.VMEM((D // LANES, T, LANES), F32)], compiler_params=_cparams(2),
        name="in_proj",
    )(x, g, w_main, w_gate, tab)


def _gelu_tanh(x):
    return 0.5 * x * (1.0 + jnp.tanh(0.7978845608028654 * (x + 0.044715 * (x * x * x))))


def _compress_kernel(x_ref, pos_ref, w1_ref, w2_ref, o_ref):
    nch = x_ref.shape[0] // CMP_STRIDE
    half = CMP_STRIDE * LANES
    chunks = jnp.concatenate(
        [x_ref[pl.ds(j, nch, stride=CMP_STRIDE), :] for j in range(CMP_STRIDE)], axis=1).astype(BF16)
    lane_group = (lax.broadcasted_iota(jnp.int32, (1, half), 1) % LANES) // HEAD_DIM
    pb = _dot(pos_ref[...], w1_ref[...])[0:1]
    for g in range(B_KV):
        c = jnp.where(lane_group == g, chunks, jnp.zeros((), BF16))
        p1 = _dot(c, w1_ref[:half, :])
        p2 = _dot(c, w1_ref[half:, :])
        p2_next = pltpu.roll(p2, nch - 1, 0)
        h = _gelu_tanh(p1 + p2_next + pb)
        o_ref[g] = _dot(h.astype(BF16), w2_ref[...]).astype(BF16)


def _compress(cmp_in, pos, w1, w2):
    B, S, _ = cmp_in.shape
    nch = S // CMP_STRIDE
    kx = 2 * CMP_STRIDE * LANES
    return pl.pallas_call(
        _compress_kernel, grid=(2, B),
        in_specs=[
            pl.BlockSpec((None, S, LANES), lambda t, b: (b, 0, t)),
            pl.BlockSpec((None, 8, kx), lambda t, b: (t, 0, 0)),
            pl.BlockSpec((None, kx, CMP_HIDDEN), lambda t, b: (t, 0, 0)),
            pl.BlockSpec((None, CMP_HIDDEN, HEAD_DIM), lambda t, b: (t, 0, 0)),
        ],
        out_specs=pl.BlockSpec((None, None, B_KV, nch, HEAD_DIM), lambda t, b: (t, b, 0, 0, 0)),
        out_shape=jax.ShapeDtypeStruct((2, B, B_KV, nch, HEAD_DIM), BF16),
        compiler_params=_cparams(2), name="compress",
    )(cmp_in, pos, w1, w2)


CMP_TQ = 256


CMP_KEY_CHUNK = 128


def _cmp_attn_kernel(q_ref, kc_ref, vct_ref, ovt_ref, o_ref, sel_ref, imp_scr):
    i = pl.program_id(1)
    tq = q_ref.shape[0]
    nc = kc_ref.shape[1]
    ns = ovt_ref.shape[0]
    rep = B_HEADS // B_KV
    t = i * tq + lax.broadcasted_iota(jnp.int32, (1, tq), 1)
    live = jnp.where(t >= CMP_LEN - 1, 1.0, 0.0)
    blk = lax.broadcasted_iota(jnp.int32, (ns, 1), 0)
    blk_f = blk.astype(F32)
    cur = jnp.right_shift(t, SEL_LEN.bit_length() - 1)
    forced = (blk == 0) | (blk == cur) | (blk == cur - 1)
    keep = jnp.where(forced | (blk > cur), 0.0, 1.0)
    pinned = jnp.where(forced, FORCE, jnp.where(blk > cur, -FORCE, 0.0))
    removed = -3e38

    def attend(nk):
        n = lax.broadcasted_iota(jnp.int32, (nk, 1), 0)
        hidden = jnp.where((n * CMP_STRIDE + (CMP_LEN - 1)) <= t, 0.0, NEG)
        ovt = ovt_ref[:, :nk]
        o_t = []
        for g in range(B_KV):
            kc = kc_ref[g, :nk, :]
            vct = vct_ref[g, :, :nk]
            psum = jnp.zeros((nk, tq), F32)
            for r in range(rep):
                h = g * rep + r
                qh = q_ref[:, h * HEAD_DIM:(h + 1) * HEAD_DIM]
                s = _dot_nt(kc, qh) + hidden
                p = jnp.exp2(s - _reduce_rows(s, jnp.maximum))
                p = p * (live / _reduce_rows(p, jnp.add))
                o_t.append(_dot(vct, p.astype(BF16)))
                psum = psum + p
            imp = sum(_dot(ovt, piece) for piece in _split_bf16(psum, 3))
            imp_scr[:, g * tq:(g + 1) * tq] = imp * keep + pinned
        o_ref[...] = jnp.transpose(jnp.concatenate(o_t, axis=0)).astype(BF16)

    n_chunks = nc // CMP_KEY_CHUNK
    last_chunk = ((i + 1) * (tq // CMP_STRIDE) - 1) // CMP_KEY_CHUNK
    for c in range(n_chunks):
        pl.when(last_chunk == c)(functools.partial(attend, (c + 1) * CMP_KEY_CHUNK))

    imp = imp_scr[...]
    for _ in range(SEL_TOP):
        mx = jnp.max(imp, axis=0, keepdims=True)
        first = jnp.min(jnp.where(imp == mx, blk_f, float(ns)), axis=0, keepdims=True)
        imp = jnp.where(blk_f == first, removed, imp)
    for g in range(B_KV):
        picked = imp[:, g * tq:(g + 1) * tq] < 0.5 * removed
        sel_ref[g] = jnp.where(picked, 0.0, NEG).astype(BF16)


def _cmp_attn(nat_b, kc, vct, overlap_t):
    B, S, _ = nat_b.shape
    nc = kc.shape[2]
    ns = overlap_t.shape[0]
    tq = CMP_TQ
    return pl.pallas_call(
        _cmp_attn_kernel, grid=(B, S // tq),
        in_specs=[
            pl.BlockSpec((None, tq, B_Q), lambda b, i: (b, i, 0)),
            pl.BlockSpec((None, B_KV, nc, HEAD_DIM), lambda b, i: (b, 0, 0, 0)),
            pl.BlockSpec((None, B_KV, HEAD_DIM, nc), lambda b, i: (b, 0, 0, 0)),
            pl.BlockSpec((ns, nc), lambda b, i: (0, 0)),
        ],
        out_specs=(
            pl.BlockSpec((None, tq, B_Q), lambda b, i: (b, i, 0)),
            pl.BlockSpec((None, B_KV, ns, tq), lambda b, i: (b, 0, 0, i)),
        ),
        out_shape=(
            jax.ShapeDtypeStruct((B, S, B_Q), BF16),
            jax.ShapeDtypeStruct((B, B_KV, ns, S), BF16),
        ),
        scratch_shapes=[pltpu.VMEM((ns, B_KV * tq), F32)],
        compiler_params=_cparams(2), name="cmp_attn",
    )(nat_b, kc, vct, overlap_t)


SLC_TK = 256
SLC_TQ = 2 * SLC_TK


SLC_VROWS = 80
SLC_CW = 256


def _slc_attn_kernel(q_ref, selt_ref, k_ref, vt_ref, o_ref, qat_scr, m_scr, acc_scr, s_a, s_b):
    i = pl.program_id(2)
    tq = q_ref.shape[0]
    rep = B_HEADS // B_KV
    selt = selt_ref[...]
    for r in range(rep):
        cols = slice(r * tq, (r + 1) * tq)
        q_r = q_ref[:, r * HEAD_DIM:(r + 1) * HEAD_DIM].astype(F32)
        qat_scr[:HEAD_DIM, cols] = jnp.transpose(q_r).astype(BF16)
        qat_scr[HEAD_DIM:, cols] = selt
    m_scr[...] = jnp.full(m_scr.shape, NEG, F32)
    acc_scr[...] = jnp.zeros(acc_scr.shape, F32)
    rel = (lax.broadcasted_iota(jnp.int32, (SLC_TK, tq), 1)
           - lax.broadcasted_iota(jnp.int32, (SLC_TK, tq), 0))

    def scores(j, s_buf):
        k0 = pl.multiple_of(j * SLC_TK, SLC_TK)
        s_buf[...] = _dot(k_ref[pl.ds(k0, SLC_TK), :], qat_scr[...])

    def update(j, s_buf, diag_offset=None):
        k0 = pl.multiple_of(j * SLC_TK, SLC_TK)
        vt = vt_ref[:, pl.ds(k0, SLC_TK)]
        for c in range(rep * tq // SLC_CW):
            cols = slice(c * SLC_CW, (c + 1) * SLC_CW)
            s = s_buf[:, cols]
            if diag_offset is not None:
                q_lo = (c * SLC_CW) % tq
                s = jnp.where(rel[:, q_lo:q_lo + SLC_CW] >= diag_offset, s, NEG)
            m_prev = m_scr[:, cols]
            m_new = jnp.maximum(m_prev, jnp.max(s, axis=0, keepdims=True))
            alpha = jnp.exp2(m_prev - m_new)
            p = jnp.exp2(s - m_new).astype(BF16)
            acc_scr[:, cols] = alpha * acc_scr[:, cols] + _dot(vt, p)
            m_scr[:, cols] = m_new

    def tile_pair(jj, carry):
        a = 2 * jj
        scores(a + 1, s_b)
        update(a, s_a)
        scores(a + 2, s_a)
        update(a + 1, s_b)
        return carry

    def two_pairs(jj, carry):
        tile_pair(2 * jj, carry)
        return tile_pair(2 * jj + 1, carry)

    scores(0, s_a)
    lax.fori_loop(0, i // 2, two_pairs, 0)

    @pl.when(i % 2 == 1)
    def _():
        tile_pair(i - 1, 0)

    scores(2 * i + 1, s_b)
    update(2 * i, s_a, diag_offset=0)
    update(2 * i + 1, s_b, diag_offset=SLC_TK)

    for r in range(rep):
        acc = acc_scr[:, r * tq:(r + 1) * tq]
        o_t = acc[:HEAD_DIM] / acc[HEAD_DIM:HEAD_DIM + 1]
        o_ref[:, r * HEAD_DIM:(r + 1) * HEAD_DIM] = jnp.transpose(o_t).astype(BF16)


def _slc_attn(nat_b, selt, k_aug, vt_aug):
    B, S, _ = nat_b.shape
    G = B_KV
    rep = B_HEADS // B_KV
    ns = selt.shape[2]
    tq = SLC_TQ
    gw = rep * HEAD_DIM
    return pl.pallas_call(
        _slc_attn_kernel, grid=(B, G, S // tq),
        in_specs=[
            pl.BlockSpec((None, tq, gw), lambda b, g, i: (b, i, g)),
            pl.BlockSpec((None, None, ns, tq), lambda b, g, i: (b, g, 0, i)),
            pl.BlockSpec((None, None, S, HEAD_DIM + ns), lambda b, g, i: (b, g, 0, 0)),
            pl.BlockSpec((None, None, SLC_VROWS, S), lambda b, g, i: (b, g, 0, 0)),
        ],
        out_specs=pl.BlockSpec((None, tq, gw), lambda b, g, i: (b, i, g)),
        out_shape=jax.ShapeDtypeStruct((B, S, B_Q), BF16),
        scratch_shapes=[pltpu.VMEM((HEAD_DIM + ns, rep * tq), BF16),
                        pltpu.VMEM((1, rep * tq), F32), pltpu.VMEM((SLC_VROWS, rep * tq), F32),
                        pltpu.VMEM((SLC_TK, rep * tq), F32), pltpu.VMEM((SLC_TK, rep * tq), F32)],
        compiler_params=_cparams(3), name="slc_attn",
    )(nat_b, selt, k_aug, vt_aug)


SUB = 128


def _band_kernel(q_ref, kp_ref, kc_ref, vp_ref, vc_ref, o_ref, *lse_refs, shared_kv, prev, max_off):
    i = pl.program_id(1)
    tq = q_ref.shape[0]
    n_slab = q_ref.shape[1] // LANES
    width = prev + SUB
    r_i = lax.broadcasted_iota(jnp.int32, (SUB, width), 0)
    c_i = lax.broadcasted_iota(jnp.int32, (SUB, width), 1)
    diff = r_i - c_i + prev
    in_band = (diff >= 0) & (diff <= max_off)
    lane = lax.broadcasted_iota(jnp.int32, (1, LANES), 1)
    lo_half = lane < HEAD_DIM
    half_mask = {"lo": lo_half, "hi": jnp.logical_not(lo_half)}

    def window(p_ref, c_ref, sub, slab):
        cols = slice(slab * LANES, (slab + 1) * LANES)
        lo = sub * SUB
        parts = []
        if lo < prev:
            parts.append(p_ref[lo:prev, cols])
        parts.append(c_ref[max(0, lo - prev):lo + SUB, cols])
        return jnp.concatenate(parts, axis=0) if len(parts) > 1 else parts[0]

    def swap_halves(x):
        return jnp.concatenate([x[:, HEAD_DIM:], x[:, :HEAD_DIM]], axis=1)

    for sub in range(tq // SUB):
        rows = slice(sub * SUB, (sub + 1) * SUB)
        k_start = i * tq + sub * SUB - prev
        bias = jnp.where(in_band & (c_i + k_start >= 0), 0.0, NEG)
        if shared_kv:
            k2 = window(kp_ref, kc_ref, sub, 0)
            v2 = window(vp_ref, vc_ref, sub, 0)
            groups = [
                (((0, "lo"), (1, "lo"), (2, "hi"), (3, "hi")), k2, v2),
                (((0, "hi"), (1, "hi"), (2, "lo"), (3, "lo")), swap_halves(k2), swap_halves(v2)),
            ]
        else:
            groups = [(((p, "lo"), (p, "hi")), window(kp_ref, kc_ref, sub, p),
                       window(vp_ref, vc_ref, sub, p)) for p in range(n_slab)]
        outs = {}
        for members, k2, v2 in groups:
            nm = len(members)
            zero = jnp.zeros((), BF16)
            lhs = jnp.concatenate(
                [jnp.where(half_mask[half], q_ref[rows, slab * LANES:(slab + 1) * LANES], zero)
                 for slab, half in members], axis=0)
            s = _dot_nt(lhs, k2)
            s = (s.reshape(nm, SUB, width) + bias[None]).reshape(nm * SUB, width)
            m = jnp.max(s, axis=-1, keepdims=True)
            p = jnp.exp2(s - m)
            den = jnp.sum(p, axis=-1, keepdims=True)
            pv = _dot(p.astype(BF16), v2) / den
            lse = m + jnp.log2(den)
            for idx, member in enumerate(members):
                outs[member] = (pv[idx * SUB:(idx + 1) * SUB], lse[idx * SUB:(idx + 1) * SUB])
        lse_tile = jnp.zeros((SUB, LANES), F32)
        for slab in range(n_slab):
            cols = slice(slab * LANES, (slab + 1) * LANES)
            (o_lo, l_lo), (o_hi, l_hi) = outs[(slab, "lo")], outs[(slab, "hi")]
            o_ref[rows, cols] = jnp.where(lo_half, o_lo, o_hi).astype(o_ref.dtype)
            lse_tile = jnp.where(lane == 2 * slab, l_lo,
                                 jnp.where(lane == 2 * slab + 1, l_hi, lse_tile))
        if lse_refs:
            lse_refs[0][rows, :] = lse_tile


def _band_attn(q_arr, k_arr, v_arr, *, q_blk, k_blk, v_blk, hq, hk, max_off, tq, with_lse):
    N, L, _ = q_arr.shape
    assert hq == hk or (hq, hk) == (8, 2), (hq, hk)
    prev = -(-max_off // SUB) * SUB
    m = tq // prev
    qw, kw = hq * HEAD_DIM, hk * HEAD_DIM
    cur = lambda blk: (lambda n, i: (n, i, blk))
    prv = lambda blk: (lambda n, i: (n, jnp.maximum(i * m - 1, 0), blk))
    out_shape = [jax.ShapeDtypeStruct((N, L, qw), BF16)]
    out_specs = [pl.BlockSpec((None, tq, qw), cur(0))]
    if with_lse:
        out_shape.append(jax.ShapeDtypeStruct((N, L, LANES), F32))
        out_specs.append(pl.BlockSpec((None, tq, LANES), cur(0)))
    return pl.pallas_call(
        functools.partial(_band_kernel, shared_kv=hq != hk, prev=prev, max_off=max_off),
        grid=(N, L // tq),
        in_specs=[
            pl.BlockSpec((None, tq, qw), cur(q_blk)),
            pl.BlockSpec((None, prev, kw), prv(k_blk)),
            pl.BlockSpec((None, tq, kw), cur(k_blk)),
            pl.BlockSpec((None, prev, kw), prv(v_blk)),
            pl.BlockSpec((None, tq, kw), cur(v_blk)),
        ],
        out_specs=tuple(out_specs), out_shape=tuple(out_shape),
        compiler_params=_cparams(2), name="band_attn",
    )(q_arr, k_arr, k_arr, v_arr, v_arr)


FF_CHUNK = 512

def _merge_ffn_kernel(x_ref, o0_ref, l0_ref, o1_ref, l1_ref, o2_ref, l2_ref, oc_ref, os_ref, ow_ref,
                      gate_ref, wa_ref, wb_ref, wo_ref, gm_ref, wu_ref, wd_ref, gf_ref,
                      out_ref, s_o1, s_l1, s_o2, s_l2, x1_scr, *, final_norm):
    step = pl.program_id(0)
    T = x_ref.shape[0]

    @pl.when(step == 0)
    def _():
        x1_scr[1] = jnp.zeros(x1_scr.shape[1:], F32)

    x1_prev = x1_scr[(step + 1) % 2]
    u = _rms(x1_prev, gm_ref[...]).astype(BF16)
    acc = x1_prev
    for c in range(D_FF // FF_CHUNK):
        cols = slice(c * FF_CHUNK, (c + 1) * FF_CHUNK)
        h = jnp.maximum(_dot(u, wu_ref[:, cols]), 0.0)
        acc = acc + _dot((h * h).astype(BF16), wd_ref[cols, :])
    out_ref[...] = _rms(acc, gf_ref[...]) if final_norm else acc

    for d, o_ref, l_ref, s_o, s_l in ((4, o1_ref, l1_ref, s_o1, s_l1), (16, o2_ref, l2_ref, s_o2, s_l2)):
        rows = T // d
        for r in range(d):
            o_r = o_ref[r].astype(F32)
            s_l[pl.ds(r, rows, stride=d), :] = l_ref[r]
            for k in range(s_o.shape[0]):
                s_o[k, pl.ds(r, rows, stride=d), :] = o_r[:, k * LANES:(k + 1) * LANES]
    gather = lambda s: jnp.concatenate([s[k] for k in range(s.shape[0])], axis=1)

    l0, l1, l2 = l0_ref[...], s_l1[...], s_l2[...]
    mx = jnp.maximum(jnp.maximum(l0, l1), l2)
    e0, e1, e2 = jnp.exp2(l0 - mx), jnp.exp2(l1 - mx), jnp.exp2(l2 - mx)
    inv = 1.0 / (e0 + e1 + e2)
    lo_half = lax.broadcasted_iota(jnp.int32, (1, LANES), 1) < HEAD_DIM

    def spread(w, lane_of_head):
        col = lambda h: jnp.broadcast_to(w[:, lane_of_head(h):lane_of_head(h) + 1], (T, LANES))
        return jnp.concatenate([jnp.where(lo_half, col(2 * p), col(2 * p + 1))
                                for p in range(A_SLOTS // 2)], axis=1)

    slot = lambda h: h
    y_a = (spread(e0 * inv, slot) * o0_ref[...].astype(F32) + spread(e1 * inv, slot) * gather(s_o1)
           + spread(e2 * inv, slot) * gather(s_o2))

    gb = gate_ref[:, MERGE_GATE:MERGE_GATE + LANES].astype(F32)
    y_b = (spread(gb, lambda h: 3 * h) * oc_ref[...].astype(F32)
           + spread(gb, lambda h: 3 * h + 1) * os_ref[...].astype(F32)
           + spread(gb, lambda h: 3 * h + 2) * ow_ref[...].astype(F32))
    merged = (gate_ref[:, :D_MODEL].astype(F32) * _dot(y_a.astype(BF16), wa_ref[...])
              + gate_ref[:, D_MODEL:MERGE_GATE].astype(F32) * _dot(y_b.astype(BF16), wb_ref[...]))
    x1_scr[step % 2] = x_ref[...] + _dot(merged.astype(BF16), wo_ref[...])


def _merge_ffn(x, o0, l0, o1, l1, o2, l2, o_cmp, o_slc, o_win, gates, wa, wb, wo,
               g_mlp, w_up, w_down, g_final, final_norm):
    B, S, D = x.shape
    T = ROW_TILE
    nt = S // T
    n_tiles = B * nt
    qw = A_SLOTS * HEAD_DIM

    def tile_map(shift):
        def index_map(k):
            m = jnp.clip(k - shift, 0, n_tiles - 1)
            return (m // nt, m % nt, 0)
        return index_map

    nat = lambda w, shift=0: pl.BlockSpec((None, T, w), tile_map(shift))
    dil = lambda d, w: pl.BlockSpec((d, T // d, w), tile_map(0))
    full = lambda a: pl.BlockSpec(a.shape, lambda k: (0,) * a.ndim, pipeline_mode=pl.Buffered(1))
    o_scr, l_scr = pltpu.VMEM((qw // LANES, T, LANES), F32), pltpu.VMEM((T, LANES), F32)
    return pl.pallas_call(
        functools.partial(_merge_ffn_kernel, final_norm=final_norm), grid=(n_tiles + 1,),
        in_specs=[nat(D), nat(qw), nat(LANES), dil(4, qw), dil(4, LANES), dil(16, qw),
                  dil(16, LANES), nat(B_Q), nat(B_Q), nat(B_Q), nat(G_COLS), full(wa),
                  full(wb), full(wo), full(g_mlp), full(w_up), full(w_down), full(g_final)],
        out_specs=nat(D, shift=1), out_shape=jax.ShapeDtypeStruct((B, S, D), F32),
        scratch_shapes=[o_scr, l_scr, o_scr, l_scr, pltpu.VMEM((2, T, D), F32)],
        compiler_params=_cparams(1), name="merge_ffn",
    )(x, o0, l0, o1, l1, o2, l2, o_cmp, o_slc, o_win, gates, wa, wb, wo,
      g_mlp, w_up, w_down, g_final)


def _rope_table(seq):
    half = ROT_DIM // 2
    pos = jnp.arange(seq, dtype=F32)[:, None]
    lane = jnp.arange(LANES, dtype=jnp.int32) % HEAD_DIM
    inv = ROPE_THETA ** (-(2 * (lane % half)).astype(F32) / ROT_DIM)
    ang = pos * inv[None, :]
    cos, sin = jnp.cos(ang), jnp.sin(ang)
    return jnp.stack([
        jnp.where(lane < ROT_DIM, cos, 1.0),
        jnp.where(lane < half, -sin, 0.0),
        jnp.where((lane >= half) & (lane < ROT_DIM), sin, 0.0),
    ])


def _layer_weights(w_in):
    o3 = A_QKV + B_COLS
    o4 = o3 + B_GATE
    w_main = w_in[:, :o3].astype(BF16)
    w_gate = jnp.concatenate(
        [w_in[:, o4:], jnp.pad(w_in[:, o3:o4], ((0, 0), (0, G_PAD - B_GATE)))], axis=1).astype(BF16)
    return w_main, w_gate


def _overlap_matrix(nc_pad, ns):
    c_start = jnp.arange(nc_pad) * CMP_STRIDE
    s_start = jnp.arange(ns) * SEL_LEN
    ov = (c_start[:, None] < s_start[None, :] + SEL_LEN) & (c_start[:, None] + CMP_LEN > s_start[None, :])
    return ov.astype(BF16)


def _layer(x, g_mix, w_in, cmp_pos_k, cmp_w1_k, cmp_w2_k, cmp_pos_v, cmp_w1_v, cmp_w2_v,
           w_branch_a, w_branch_b, w_out, g_mlp, w_up, w_down, g_final, final_norm, tab):
    B, S, D = x.shape
    G = B_KV
    a0, a1, a2, nat_b, cmp_in, gates = _in_proj(x, g_mix.reshape(1, D), *_layer_weights(w_in), tab)

    a_out = []
    for arr, (w, d) in zip((a0, a1, a2), DIL_PAIRS):
        a_out.append(_band_attn(arr, arr, arr, q_blk=0, k_blk=1, v_blk=2, hq=A_SLOTS, hk=A_SLOTS,
                                max_off=w // d, tq=256, with_lse=True))
    (o0, l0), (o1, l1), (o2, l2) = a_out

    nch = S // CMP_STRIDE
    kv_w = G * HEAD_DIM
    k_slc = nat_b[:, :, B_Q + 2 * kv_w:B_Q + 3 * kv_w].reshape(B, S, G, HEAD_DIM)
    v_slc = nat_b[:, :, B_Q + 3 * kv_w:B_Q + 4 * kv_w].reshape(B, S, G, HEAD_DIM)
    pos = jnp.stack([cmp_pos_k, cmp_pos_v]).astype(BF16)[:, :, None, :]
    pos = jnp.pad(pos, ((0, 0), (0, 0), (0, G - 1), (0, 0))).reshape(2, 1, CMP_LEN * G * HEAD_DIM)
    pos = jnp.broadcast_to(pos, (2, 8, CMP_LEN * G * HEAD_DIM))
    w1 = jnp.stack([cmp_w1_k, cmp_w1_v]).astype(BF16).reshape(2, CMP_LEN, 1, HEAD_DIM, CMP_HIDDEN)
    w1 = jnp.broadcast_to(w1, (2, CMP_LEN, G, HEAD_DIM, CMP_HIDDEN)).reshape(
        2, CMP_LEN * G * HEAD_DIM, CMP_HIDDEN)
    kcvc = _compress(cmp_in, pos, w1, jnp.stack([cmp_w2_k, cmp_w2_v]).astype(BF16))
    ns = S // SEL_LEN
    o_cmp, selb = _cmp_attn(nat_b, kcvc[0], kcvc[1].transpose(0, 1, 3, 2),
                            _overlap_matrix(nch, ns).T)

    onehot = ((jnp.arange(S) // SEL_LEN)[:, None] == jnp.arange(ns)[None, :]).astype(BF16)
    k_aug = jnp.concatenate([k_slc.transpose(0, 2, 1, 3),
                             jnp.broadcast_to(onehot, (B, G, S, ns))], axis=3)
    ones_row = (jnp.arange(SLC_VROWS - HEAD_DIM) == 0).astype(BF16)[:, None]
    vt_aug = jnp.concatenate(
        [v_slc.transpose(0, 2, 3, 1),
         jnp.broadcast_to(ones_row, (B, G, SLC_VROWS - HEAD_DIM, S))], axis=2)
    o_slc = _slc_attn(nat_b, selb, k_aug, vt_aug)

    kw_blk = (B_Q + 4 * G * HEAD_DIM) // (G * HEAD_DIM)
    (o_win,) = _band_attn(nat_b, nat_b, nat_b, q_blk=0, k_blk=kw_blk, v_blk=kw_blk + 1,
                          hq=B_HEADS, hk=B_KV, max_off=WIN_LEN - 1, tq=512, with_lse=False)

    return _merge_ffn(x, o0, l0, o1, l1, o2, l2, o_cmp, o_slc, o_win, gates,
                      w_branch_a.astype(BF16), w_branch_b.astype(BF16), w_out.astype(BF16),
                      g_mlp.reshape(1, D), w_up.astype(BF16), w_down.astype(BF16),
                      g_final.reshape(1, D), final_norm)


def kernel(x, norm_mix_g, w_in, cmp_pos_k, cmp_w1_k, cmp_w2_k, cmp_pos_v, cmp_w1_v, cmp_w2_v,
           w_branch_a, w_branch_b, w_out, norm_mlp_g, w_up, w_down, norm_final_g):
    B, S, D = x.shape
    depth = w_in.shape[0]
    tab = _rope_table(S)
    for l in range(depth):
        x = _layer(x, norm_mix_g[l], w_in[l], cmp_pos_k[l], cmp_w1_k[l], cmp_w2_k[l],
                   cmp_pos_v[l], cmp_w1_v[l], cmp_w2_v[l], w_branch_a[l], w_branch_b[l],
                   w_out[l], norm_mlp_g[l], w_up[l], w_down[l], norm_final_g, l == depth - 1, tab)
    return x
```

```python
import functools

import jax
import jax.numpy as jnp
from jax import lax
from jax.experimental import pallas as pl
from jax.experimental.pallas import tpu as pltpu

F32 = jnp.float32
BF16 = jnp.bfloat16

D_MODEL = 1024
HEAD_DIM = 64
ROT_DIM = HEAD_DIM // 4
ROPE_THETA = 500000.0
EPS = 1e-6
DIL_PAIRS = ((128, 1), (512, 4), (2048, 16))
A_SLOTS = 8
A_HEADS = A_SLOTS * len(DIL_PAIRS)
B_HEADS = 8
B_KV = 2
CMP_LEN = 32
CMP_STRIDE = 16
CMP_HIDDEN = 4 * HEAD_DIM
SEL_LEN = 64
SEL_TOP = 16
WIN_LEN = 512
FORCE = 1e4
D_FF = 4 * D_MODEL
A_QKV = 3 * A_HEADS * HEAD_DIM
B_Q = B_HEADS * HEAD_DIM
B_KV_COLS = 6 * B_KV * HEAD_DIM
B_GATE = 3 * B_HEADS
MERGE_GATE = 2 * D_MODEL

LANES = 128
NEG = -1e30
QSCALE = HEAD_DIM ** -0.5 * 1.4426950408889634

A_COLS = 3 * A_SLOTS * HEAD_DIM
B_COLS = B_Q + B_KV_COLS
PROJ_CHUNK = 256
G_PAD = PROJ_CHUNK
G_COLS = MERGE_GATE + G_PAD

ROW_TILE = 512
VMEM_LIMIT = 56 * 1024 * 1024


def _cparams(n_axes, vmem=VMEM_LIMIT):
    return pltpu.CompilerParams(dimension_semantics=("arbitrary",) * n_axes, vmem_limit_bytes=vmem)


def _rms(x, g):
    return x * lax.rsqrt(jnp.mean(x * x, axis=-1, keepdims=True) + EPS) * g


def _dot(a, b):
    return jnp.dot(a, b, preferred_element_type=F32)


def _dot_nt(a, b):
    return lax.dot_general(a, b, (((1,), (1,)), ((), ())), preferred_element_type=F32)


def _reduce_rows(x, op):
    while x.shape[0] > 8 and x.shape[0] % 16 == 0:
        half = x.shape[0] // 2
        x = op(x[:half], x[half:])
    red = jnp.max if op is jnp.maximum else jnp.sum
    return red(x, axis=0, keepdims=True)


def _split_bf16(x, parts):
    out = []
    for _ in range(parts - 1):
        hi = x.astype(BF16)
        out.append(hi)
        x = x - hi.astype(F32)
    out.append(x.astype(BF16))
    return out


_B_ROT = (True,) * (B_Q // LANES) + (True, False, True, False, True, False)


def _rotary(x, cos_t, sin_a, sin_b):
    half = ROT_DIM // 2
    return x * cos_t + pltpu.roll(x, LANES - half, 1) * sin_a + pltpu.roll(x, half, 1) * sin_b


def _in_proj_kernel(x_ref, g_ref, w_ref, wg_ref, tab_ref,
                    a0_ref, a1_ref, a2_ref, b_ref, cmp_ref, gate_ref, u_scr):
    T = x_ref.shape[0]
    u = _rms(x_ref[...], g_ref[...])
    nslab = u_scr.shape[0]
    for k in range(nslab):
        u_scr[k] = u[:, k * LANES:(k + 1) * LANES]
    u_nat = u.astype(BF16)

    def dilated(load, d):
        rows = T // d
        return jnp.concatenate([load(pl.ds(r, rows, stride=d)) for r in range(d)], axis=0)

    def u_rows(rows):
        return jnp.concatenate([u_scr[k, rows, :] for k in range(nslab)], axis=1).astype(BF16)

    q_slabs = A_SLOTS * HEAD_DIM // LANES
    per = PROJ_CHUNK // LANES

    def project(lhs, weights, chunk_cols, rot, tables, store, n_q=0):
        for c, w_col in enumerate(chunk_cols):
            res = _dot(lhs, weights[:, w_col:w_col + PROJ_CHUNK])
            for k in range(per):
                slab = res[:, k * LANES:(k + 1) * LANES]
                if rot[c * per + k]:
                    slab = _rotary(slab, tables[0], tables[1], tables[2])
                if c * per + k < n_q:
                    slab = slab * QSCALE
                store((c * per + k) * LANES, slab)

    a_rot = (True,) * (2 * q_slabs) + (False,) * q_slabs
    group_w = A_SLOTS * HEAD_DIM

    def group_cols(g):
        return [t * A_HEADS * HEAD_DIM + g * group_w + c
                for t in range(3) for c in range(0, group_w, PROJ_CHUNK)]

    tab0 = tuple(tab_ref[k] for k in range(3))

    def store_a0(c0, res):
        a0_ref[:, c0:c0 + LANES] = res.astype(BF16)

    project(u_nat, w_ref, group_cols(0), a_rot, tab0, store_a0, n_q=q_slabs)

    for g, (d, a_ref) in enumerate(((4, a1_ref), (16, a2_ref)), start=1):
        rows = T // d
        tabs = tuple(dilated(lambda rs, k=k: tab_ref[k, rs, :], d) for k in range(3))

        def store_ad(c0, res, a_ref=a_ref, d=d, rows=rows):
            for r in range(d):
                a_ref[r, :, c0:c0 + LANES] = res[r * rows:(r + 1) * rows].astype(BF16)

        project(dilated(u_rows, d), w_ref, group_cols(g), a_rot, tabs, store_ad, n_q=q_slabs)

    def store_b(c0, res):
        b_ref[:, c0:c0 + LANES] = res.astype(BF16)
        if B_Q <= c0 < B_Q + 2 * LANES:
            cmp_ref[:, c0 - B_Q:c0 - B_Q + LANES] = res

    project(u_nat, w_ref, range(A_QKV, A_QKV + B_COLS, PROJ_CHUNK), _B_ROT, tab0, store_b,
            n_q=q_slabs)

    def store_g(c0, res):
        gate_ref[:, c0:c0 + LANES] = jax.nn.sigmoid(res).astype(gate_ref.dtype)

    project(u_nat, wg_ref, range(0, G_COLS, PROJ_CHUNK), (False,) * (G_COLS // LANES), tab0, store_g)


def _in_proj(x, g, w_main, w_gate, tab):
    B, S, D = x.shape
    T = ROW_TILE
    nt = S // T
    out_shape = (
        jax.ShapeDtypeStruct((B, S, A_COLS), BF16),
        jax.ShapeDtypeStruct((B * 4, S // 4, A_COLS), BF16),
        jax.ShapeDtypeStruct((B * 16, S // 16, A_COLS), BF16),
        jax.ShapeDtypeStruct((B, S, B_COLS), BF16),
        jax.ShapeDtypeStruct((B, S, 2 * LANES), F32),
        jax.ShapeDtypeStruct((B, S, G_COLS), BF16),
    )
    in_specs = [
        pl.BlockSpec((None, T, D), lambda b, i: (b, i, 0)),
        pl.BlockSpec((1, D), lambda b, i: (0, 0)),
        pl.BlockSpec(w_main.shape, lambda b, i: (0, 0), pipeline_mode=pl.Buffered(1)),
        pl.BlockSpec(w_gate.shape, lambda b, i: (0, 0), pipeline_mode=pl.Buffered(1)),
        pl.BlockSpec((3, T, LANES), lambda b, i: (0, i, 0)),
    ]
    out_specs = (
        pl.BlockSpec((None, T, A_COLS), lambda b, i: (b, i, 0)),
        pl.BlockSpec((4, T // 4, A_COLS), lambda b, i: (b, i, 0)),
        pl.BlockSpec((16, T // 16, A_COLS), lambda b, i: (b, i, 0)),
        pl.BlockSpec((None, T, B_COLS), lambda b, i: (b, i, 0)),
        pl.BlockSpec((None, T, 2 * LANES), lambda b, i: (b, i, 0)),
        pl.BlockSpec((None, T, G_COLS), lambda b, i: (b, i, 0)),
    )
    return pl.pallas_call(
        _in_proj_kernel, grid=(B, nt), in_specs=in_specs, out_specs=out_specs, out_shape=out_shape,
        scratch_shapes=[pltpu.VMEM((D // LANES, T, LANES), F32)], compiler_params=_cparams(2),
        name="in_proj",
    )(x, g, w_main, w_gate, tab)


def _gelu_tanh(x):
    return 0.5 * x * (1.0 + jnp.tanh(0.7978845608028654 * (x + 0.044715 * (x * x * x))))


def _compress_kernel(x_ref, pos_ref, w1_ref, w2_ref, o_ref):
    nch = x_ref.shape[0] // CMP_STRIDE
    half = CMP_STRIDE * LANES
    chunks = jnp.concatenate(
        [x_ref[pl.ds(j, nch, stride=CMP_STRIDE), :] for j in range(CMP_STRIDE)], axis=1).astype(BF16)
    lane_group = (lax.broadcasted_iota(jnp.int32, (1, half), 1) % LANES) // HEAD_DIM
    pb = _dot(pos_ref[...], w1_ref[...])[0:1]
    for g in range(B_KV):
        c = jnp.where(lane_group == g, chunks, jnp.zeros((), BF16))
        p1 = _dot(c, w1_ref[:half, :])
        p2 = _dot(c, w1_ref[half:, :])
        p2_next = pltpu.roll(p2, nch - 1, 0)
        h = _gelu_tanh(p1 + p2_next + pb)
        o_ref[g] = _dot(h.astype(BF16), w2_ref[...]).astype(BF16)


def _compress(cmp_in, pos, w1, w2):
    B, S, _ = cmp_in.shape
    nch = S // CMP_STRIDE
    kx = 2 * CMP_STRIDE * LANES
    return pl.pallas_call(
        _compress_kernel, grid=(2, B),
        in_specs=[
            pl.BlockSpec((None, S, LANES), lambda t, b: (b, 0, t)),
            pl.BlockSpec((None, 8, kx), lambda t, b: (t, 0, 0)),
            pl.BlockSpec((None, kx, CMP_HIDDEN), lambda t, b: (t, 0, 0)),
            pl.BlockSpec((None, CMP_HIDDEN, HEAD_DIM), lambda t, b: (t, 0, 0)),
        ],
        out_specs=pl.BlockSpec((None, None, B_KV, nch, HEAD_DIM), lambda t, b: (t, b, 0, 0, 0)),
        out_shape=jax.ShapeDtypeStruct((2, B, B_KV, nch, HEAD_DIM), BF16),
        compiler_params=_cparams(2), name="compress",
    )(cmp_in, pos, w1, w2)


CMP_TQ = 1024


CMP_KEY_CHUNK = 128


def _cmp_attn_kernel(q_ref, kc_ref, vct_ref, ovt_ref, o_ref, sel_ref, imp_scr):
    i = pl.program_id(1)
    tq = q_ref.shape[0]
    nc = kc_ref.shape[1]
    ns = ovt_ref.shape[0]
    rep = B_HEADS // B_KV
    t = i * tq + lax.broadcasted_iota(jnp.int32, (1, tq), 1)
    live = jnp.where(t >= CMP_LEN - 1, 1.0, 0.0)
    blk = lax.broadcasted_iota(jnp.int32, (ns, 1), 0)
    blk_f = blk.astype(F32)
    cur = jnp.right_shift(t, SEL_LEN.bit_length() - 1)
    forced = (blk == 0) | (blk == cur) | (blk == cur - 1)
    keep = jnp.where(forced | (blk > cur), 0.0, 1.0)
    pinned = jnp.where(forced, FORCE, jnp.where(blk > cur, -FORCE, 0.0))
    removed = -3e38

    def attend(nk):
        n = lax.broadcasted_iota(jnp.int32, (nk, 1), 0)
        hidden = jnp.where((n * CMP_STRIDE + (CMP_LEN - 1)) <= t, 0.0, NEG)
        ovt = ovt_ref[:, :nk]
        o_t = []
        for g in range(B_KV):
            kc = kc_ref[g, :nk, :]
            vct = vct_ref[g, :, :nk]
            psum = jnp.zeros((nk, tq), F32)
            for r in range(rep):
                h = g * rep + r
                qh = q_ref[:, h * HEAD_DIM:(h + 1) * HEAD_DIM]
                s = _dot_nt(kc, qh) + hidden
                p = jnp.exp2(s - _reduce_rows(s, jnp.maximum))
                p = p * (live / _reduce_rows(p, jnp.add))
                o_t.append(_dot(vct, p.astype(BF16)))
                psum = psum + p
            imp = sum(_dot(ovt, piece) for piece in _split_bf16(psum, 3))
            imp_scr[:, g * tq:(g + 1) * tq] = imp * keep + pinned
        o_ref[...] = jnp.transpose(jnp.concatenate(o_t, axis=0)).astype(BF16)

    n_chunks = nc // CMP_KEY_CHUNK
    last_chunk = ((i + 1) * (tq // CMP_STRIDE) - 1) // CMP_KEY_CHUNK
    for c in range(n_chunks):
        pl.when(last_chunk == c)(functools.partial(attend, (c + 1) * CMP_KEY_CHUNK))

    imp = imp_scr[...]
    for _ in range(SEL_TOP):
        mx = jnp.max(imp, axis=0, keepdims=True)
        first = jnp.min(jnp.where(imp == mx, blk_f, float(ns)), axis=0, keepdims=True)
        imp = jnp.where(blk_f == first, removed, imp)
    for g in range(B_KV):
        picked = imp[:, g * tq:(g + 1) * tq] < 0.5 * removed
        sel_ref[g] = jnp.where(picked, 0.0, NEG).astype(BF16)


def _cmp_attn(nat_b, kc, vct, overlap_t):
    B, S, _ = nat_b.shape
    nc = kc.shape[2]
    ns = overlap_t.shape[0]
    tq = CMP_TQ
    return pl.pallas_call(
        _cmp_attn_kernel, grid=(B, S // tq),
        in_specs=[
            pl.BlockSpec((None, tq, B_Q), lambda b, i: (b, i, 0)),
            pl.BlockSpec((None, B_KV, nc, HEAD_DIM), lambda b, i: (b, 0, 0, 0)),
            pl.BlockSpec((None, B_KV, HEAD_DIM, nc), lambda b, i: (b, 0, 0, 0)),
            pl.BlockSpec((ns, nc), lambda b, i: (0, 0)),
        ],
        out_specs=(
            pl.BlockSpec((None, tq, B_Q), lambda b, i: (b, i, 0)),
            pl.BlockSpec((None, B_KV, ns, tq), lambda b, i: (b, 0, 0, i)),
        ),
        out_shape=(
            jax.ShapeDtypeStruct((B, S, B_Q), BF16),
            jax.ShapeDtypeStruct((B, B_KV, ns, S), BF16),
        ),
        scratch_shapes=[pltpu.VMEM((ns, B_KV * tq), F32)],
        compiler_params=_cparams(2), name="cmp_attn",
    )(nat_b, kc, vct, overlap_t)


SLC_TK = 256
SLC_TQ = 2 * SLC_TK


SLC_VROWS = 80
SLC_CW = 256


def _slc_attn_kernel(q_ref, selt_ref, k_ref, vt_ref, o_ref, qat_scr, m_scr, acc_scr, s_a, s_b):
    i = pl.program_id(2)
    tq = q_ref.shape[0]
    rep = B_HEADS // B_KV
    selt = selt_ref[...]
    for r in range(rep):
        cols = slice(r * tq, (r + 1) * tq)
        q_r = q_ref[:, r * HEAD_DIM:(r + 1) * HEAD_DIM].astype(F32)
        qat_scr[:HEAD_DIM, cols] = jnp.transpose(q_r).astype(BF16)
        qat_scr[HEAD_DIM:, cols] = selt
    m_scr[...] = jnp.full(m_scr.shape, NEG, F32)
    acc_scr[...] = jnp.zeros(acc_scr.shape, F32)
    rel = (lax.broadcasted_iota(jnp.int32, (SLC_TK, tq), 1)
           - lax.broadcasted_iota(jnp.int32, (SLC_TK, tq), 0))

    def scores(j, s_buf):
        k0 = pl.multiple_of(j * SLC_TK, SLC_TK)
        s_buf[...] = _dot(k_ref[pl.ds(k0, SLC_TK), :], qat_scr[...])

    def update(j, s_buf, diag_offset=None):
        k0 = pl.multiple_of(j * SLC_TK, SLC_TK)
        vt = vt_ref[:, pl.ds(k0, SLC_TK)]
        for c in range(rep * tq // SLC_CW):
            cols = slice(c * SLC_CW, (c + 1) * SLC_CW)
            q_lo = (c * SLC_CW) % tq
            if diag_offset is not None and q_lo + SLC_CW <= diag_offset:
                continue
            s = s_buf[:, cols]
            if diag_offset is not None:
                s = jnp.where(rel[:, q_lo:q_lo + SLC_CW] >= diag_offset, s, NEG)
            m_prev = m_scr[:, cols]
            m_new = jnp.maximum(m_prev, jnp.max(s, axis=0, keepdims=True))
            alpha = jnp.exp2(m_prev - m_new)
            p = jnp.exp2(s - m_new).astype(BF16)
            acc_scr[:, cols] = alpha * acc_scr[:, cols] + _dot(vt, p)
            m_scr[:, cols] = m_new

    def tile_pair(jj, carry):
        a = 2 * jj
        scores(a + 1, s_b)
        update(a, s_a)
        scores(a + 2, s_a)
        update(a + 1, s_b)
        return carry

    def two_pairs(jj, carry):
        tile_pair(2 * jj, carry)
        return tile_pair(2 * jj + 1, carry)

    scores(0, s_a)
    lax.fori_loop(0, i // 2, two_pairs, 0)

    @pl.when(i % 2 == 1)
    def _():
        tile_pair(i - 1, 0)

    scores(2 * i + 1, s_b)
    update(2 * i, s_a, diag_offset=0)
    update(2 * i + 1, s_b, diag_offset=SLC_TK)

    for r in range(rep):
        acc = acc_scr[:, r * tq:(r + 1) * tq]
        o_t = acc[:HEAD_DIM] / acc[HEAD_DIM:HEAD_DIM + 1]
        o_ref[:, r * HEAD_DIM:(r + 1) * HEAD_DIM] = jnp.transpose(o_t).astype(BF16)


def _slc_attn(nat_b, selt, k_aug, vt_aug):
    B, S, _ = nat_b.shape
    G = B_KV
    rep = B_HEADS // B_KV
    ns = selt.shape[2]
    tq = SLC_TQ
    gw = rep * HEAD_DIM
    return pl.pallas_call(
        _slc_attn_kernel, grid=(B, G, S // tq),
        in_specs=[
            pl.BlockSpec((None, tq, gw), lambda b, g, i: (b, i, g)),
            pl.BlockSpec((None, None, ns, tq), lambda b, g, i: (b, g, 0, i)),
            pl.BlockSpec((None, None, S, HEAD_DIM + ns), lambda b, g, i: (b, g, 0, 0)),
            pl.BlockSpec((None, None, SLC_VROWS, S), lambda b, g, i: (b, g, 0, 0)),
        ],
        out_specs=pl.BlockSpec((None, tq, gw), lambda b, g, i: (b, i, g)),
        out_shape=jax.ShapeDtypeStruct((B, S, B_Q), BF16),
        scratch_shapes=[pltpu.VMEM((HEAD_DIM + ns, rep * tq), BF16),
                        pltpu.VMEM((1, rep * tq), F32), pltpu.VMEM((SLC_VROWS, rep * tq), F32),
                        pltpu.VMEM((SLC_TK, rep * tq), F32), pltpu.VMEM((SLC_TK, rep * tq), F32)],
        compiler_params=_cparams(3), name="slc_attn",
    )(nat_b, selt, k_aug, vt_aug)


SUB = 128


def _band_kernel(q_ref, kp_ref, kc_ref, vp_ref, vc_ref, o_ref, *lse_refs, shared_kv, prev, max_off):
    i = pl.program_id(1)
    tq = q_ref.shape[0]
    n_slab = q_ref.shape[1] // LANES
    width = prev + SUB
    r_i = lax.broadcasted_iota(jnp.int32, (SUB, width), 0)
    c_i = lax.broadcasted_iota(jnp.int32, (SUB, width), 1)
    diff = r_i - c_i + prev
    in_band = (diff >= 0) & (diff <= max_off)
    lane = lax.broadcasted_iota(jnp.int32, (1, LANES), 1)
    lo_half = lane < HEAD_DIM
    half_mask = {"lo": lo_half, "hi": jnp.logical_not(lo_half)}

    def window(p_ref, c_ref, sub, slab):
        cols = slice(slab * LANES, (slab + 1) * LANES)
        lo = sub * SUB
        parts = []
        if lo < prev:
            parts.append(p_ref[lo:prev, cols])
        parts.append(c_ref[max(0, lo - prev):lo + SUB, cols])
        return jnp.concatenate(parts, axis=0) if len(parts) > 1 else parts[0]

    def swap_halves(x):
        return jnp.concatenate([x[:, HEAD_DIM:], x[:, :HEAD_DIM]], axis=1)

    for sub in range(tq // SUB):
        rows = slice(sub * SUB, (sub + 1) * SUB)
        k_start = i * tq + sub * SUB - prev
        bias = jnp.where(in_band & (c_i + k_start >= 0), 0.0, NEG)
        if shared_kv:
            k2 = window(kp_ref, kc_ref, sub, 0)
            v2 = window(vp_ref, vc_ref, sub, 0)
            groups = [
                (((0, "lo"), (1, "lo"), (2, "hi"), (3, "hi")), k2, v2),
                (((0, "hi"), (1, "hi"), (2, "lo"), (3, "lo")), swap_halves(k2), swap_halves(v2)),
            ]
        else:
            groups = [(((p, "lo"), (p, "hi")), window(kp_ref, kc_ref, sub, p),
                       window(vp_ref, vc_ref, sub, p)) for p in range(n_slab)]
        outs = {}
        for members, k2, v2 in groups:
            nm = len(members)
            zero = jnp.zeros((), BF16)
            lhs = jnp.concatenate(
                [jnp.where(half_mask[half], q_ref[rows, slab * LANES:(slab + 1) * LANES], zero)
                 for slab, half in members], axis=0)
            s = _dot_nt(lhs, k2)
            s = (s.reshape(nm, SUB, width) + bias[None]).reshape(nm * SUB, width)
            m = jnp.max(s, axis=-1, keepdims=True)
            p = jnp.exp2(s - m)
            den = jnp.sum(p, axis=-1, keepdims=True)
            pv = _dot(p.astype(BF16), v2) / den
            lse = m + jnp.log2(den)
            for idx, member in enumerate(members):
                outs[member] = (pv[idx * SUB:(idx + 1) * SUB], lse[idx * SUB:(idx + 1) * SUB])
        lse_tile = jnp.zeros((SUB, LANES), F32)
        for slab in range(n_slab):
            cols = slice(slab * LANES, (slab + 1) * LANES)
            (o_lo, l_lo), (o_hi, l_hi) = outs[(slab, "lo")], outs[(slab, "hi")]
            o_ref[rows, cols] = jnp.where(lo_half, o_lo, o_hi).astype(o_ref.dtype)
            lse_tile = jnp.where(lane == 2 * slab, l_lo,
                                 jnp.where(lane == 2 * slab + 1, l_hi, lse_tile))
        if lse_refs:
            lse_refs[0][rows, :] = lse_tile


def _band_attn(q_arr, k_arr, v_arr, *, q_blk, k_blk, v_blk, hq, hk, max_off, tq, with_lse):
    N, L, _ = q_arr.shape
    assert hq == hk or (hq, hk) == (8, 2), (hq, hk)
    prev = -(-max_off // SUB) * SUB
    m = tq // prev
    qw, kw = hq * HEAD_DIM, hk * HEAD_DIM
    cur = lambda blk: (lambda n, i: (n, i, blk))
    prv = lambda blk: (lambda n, i: (n, jnp.maximum(i * m - 1, 0), blk))
    out_shape = [jax.ShapeDtypeStruct((N, L, qw), BF16)]
    out_specs = [pl.BlockSpec((None, tq, qw), cur(0))]
    if with_lse:
        out_shape.append(jax.ShapeDtypeStruct((N, L, LANES), F32))
        out_specs.append(pl.BlockSpec((None, tq, LANES), cur(0)))
    return pl.pallas_call(
        functools.partial(_band_kernel, shared_kv=hq != hk, prev=prev, max_off=max_off),
        grid=(N, L // tq),
        in_specs=[
            pl.BlockSpec((None, tq, qw), cur(q_blk)),
            pl.BlockSpec((None, prev, kw), prv(k_blk)),
            pl.BlockSpec((None, tq, kw), cur(k_blk)),
            pl.BlockSpec((None, prev, kw), prv(v_blk)),
            pl.BlockSpec((None, tq, kw), cur(v_blk)),
        ],
        out_specs=tuple(out_specs), out_shape=tuple(out_shape),
        compiler_params=_cparams(2), name="band_attn",
    )(q_arr, k_arr, k_arr, v_arr, v_arr)


FF_CHUNK = 512

def _merge_ffn_kernel(x_ref, o0_ref, l0_ref, o1_ref, l1_ref, o2_ref, l2_ref, oc_ref, os_ref, ow_ref,
                      gate_ref, wa_ref, wb_ref, wo_ref, gm_ref, wu_ref, wd_ref, gf_ref,
                      out_ref, s_o1, s_l1, s_o2, s_l2, x1_scr, *, final_norm):
    step = pl.program_id(0)
    T = x_ref.shape[0]

    @pl.when(step == 0)
    def _():
        x1_scr[1] = jnp.zeros(x1_scr.shape[1:], F32)

    x1_prev = x1_scr[(step + 1) % 2]
    u = _rms(x1_prev, gm_ref[...]).astype(BF16)
    acc = x1_prev
    for c in range(D_FF // FF_CHUNK):
        cols = slice(c * FF_CHUNK, (c + 1) * FF_CHUNK)
        h = jnp.maximum(_dot(u, wu_ref[:, cols]), 0.0)
        acc = acc + _dot((h * h).astype(BF16), wd_ref[cols, :])
    out_ref[...] = _rms(acc, gf_ref[...]) if final_norm else acc

    for d, o_ref, l_ref, s_o, s_l in ((4, o1_ref, l1_ref, s_o1, s_l1), (16, o2_ref, l2_ref, s_o2, s_l2)):
        rows = T // d
        for r in range(d):
            o_r = o_ref[r].astype(F32)
            s_l[pl.ds(r, rows, stride=d), :] = l_ref[r]
            for k in range(s_o.shape[0]):
                s_o[k, pl.ds(r, rows, stride=d), :] = o_r[:, k * LANES:(k + 1) * LANES]
    gather = lambda s: jnp.concatenate([s[k] for k in range(s.shape[0])], axis=1)

    l0, l1, l2 = l0_ref[...], s_l1[...], s_l2[...]
    mx = jnp.maximum(jnp.maximum(l0, l1), l2)
    e0, e1, e2 = jnp.exp2(l0 - mx), jnp.exp2(l1 - mx), jnp.exp2(l2 - mx)
    inv = 1.0 / (e0 + e1 + e2)
    lo_half = lax.broadcasted_iota(jnp.int32, (1, LANES), 1) < HEAD_DIM

    def spread(w, lane_of_head):
        col = lambda h: jnp.broadcast_to(w[:, lane_of_head(h):lane_of_head(h) + 1], (T, LANES))
        return jnp.concatenate([jnp.where(lo_half, col(2 * p), col(2 * p + 1))
                                for p in range(A_SLOTS // 2)], axis=1)

    slot = lambda h: h
    y_a = (spread(e0 * inv, slot) * o0_ref[...].astype(F32) + spread(e1 * inv, slot) * gather(s_o1)
           + spread(e2 * inv, slot) * gather(s_o2))

    gb = gate_ref[:, MERGE_GATE:MERGE_GATE + LANES].astype(F32)
    y_b = (spread(gb, lambda h: 3 * h) * oc_ref[...].astype(F32)
           + spread(gb, lambda h: 3 * h + 1) * os_ref[...].astype(F32)
           + spread(gb, lambda h: 3 * h + 2) * ow_ref[...].astype(F32))
    merged = (gate_ref[:, :D_MODEL].astype(F32) * _dot(y_a.astype(BF16), wa_ref[...])
              + gate_ref[:, D_MODEL:MERGE_GATE].astype(F32) * _dot(y_b.astype(BF16), wb_ref[...]))
    x1_scr[step % 2] = x_ref[...] + _dot(merged.astype(BF16), wo_ref[...])


def _merge_ffn(x, o0, l0, o1, l1, o2, l2, o_cmp, o_slc, o_win, gates, wa, wb, wo,
               g_mlp, w_up, w_down, g_final, final_norm):
    B, S, D = x.shape
    T = ROW_TILE
    nt = S // T
    n_tiles = B * nt
    qw = A_SLOTS * HEAD_DIM

    def tile_map(shift):
        def index_map(k):
            m = jnp.clip(k - shift, 0, n_tiles - 1)
            return (m // nt, m % nt, 0)
        return index_map

    nat = lambda w, shift=0: pl.BlockSpec((None, T, w), tile_map(shift))
    dil = lambda d, w: pl.BlockSpec((d, T // d, w), tile_map(0))
    full = lambda a: pl.BlockSpec(a.shape, lambda k: (0,) * a.ndim, pipeline_mode=pl.Buffered(1))
    o_scr, l_scr = pltpu.VMEM((qw // LANES, T, LANES), F32), pltpu.VMEM((T, LANES), F32)
    return pl.pallas_call(
        functools.partial(_merge_ffn_kernel, final_norm=final_norm), grid=(n_tiles + 1,),
        in_specs=[nat(D), nat(qw), nat(LANES), dil(4, qw), dil(4, LANES), dil(16, qw),
                  dil(16, LANES), nat(B_Q), nat(B_Q), nat(B_Q), nat(G_COLS), full(wa),
                  full(wb), full(wo), full(g_mlp), full(w_up), full(w_down), full(g_final)],
        out_specs=nat(D, shift=1), out_shape=jax.ShapeDtypeStruct((B, S, D), F32),
        scratch_shapes=[o_scr, l_scr, o_scr, l_scr, pltpu.VMEM((2, T, D), F32)],
        compiler_params=_cparams(1), name="merge_ffn",
    )(x, o0, l0, o1, l1, o2, l2, o_cmp, o_slc, o_win, gates, wa, wb, wo,
      g_mlp, w_up, w_down, g_final)


def _rope_table(seq):
    half = ROT_DIM // 2
    pos = jnp.arange(seq, dtype=F32)[:, None]
    inv = ROPE_THETA ** (-jnp.arange(0, ROT_DIM, 2, dtype=F32) / ROT_DIM)
    ang = pos * inv[None, :]
    cos, sin = jnp.cos(ang), jnp.sin(ang)
    pad = lambda parts, fill: jnp.tile(jnp.concatenate(
        parts + [jnp.full((seq, HEAD_DIM - ROT_DIM), fill, F32)], axis=1), (1, LANES // HEAD_DIM))
    zero = jnp.zeros_like(sin)
    return jnp.stack([pad([cos, cos], 1.0), pad([-sin, zero], 0.0), pad([zero, sin], 0.0)])


def _layer_weights(w_in):
    o3 = A_QKV + B_COLS
    o4 = o3 + B_GATE
    w_main = w_in.astype(BF16)
    w_gate = jnp.concatenate(
        [w_main[:, o4:], jnp.pad(w_main[:, o3:o4], ((0, 0), (0, G_PAD - B_GATE)))], axis=1)
    return w_main, w_gate


def _overlap_matrix(nc_pad, ns):
    c_start = jnp.arange(nc_pad) * CMP_STRIDE
    s_start = jnp.arange(ns) * SEL_LEN
    ov = (c_start[:, None] < s_start[None, :] + SEL_LEN) & (c_start[:, None] + CMP_LEN > s_start[None, :])
    return ov.astype(BF16)


def _layer(x, g_mix, w_in, cmp_pos_k, cmp_w1_k, cmp_w2_k, cmp_pos_v, cmp_w1_v, cmp_w2_v,
           w_branch_a, w_branch_b, w_out, g_mlp, w_up, w_down, g_final, final_norm, tab):
    B, S, D = x.shape
    G = B_KV
    a0, a1, a2, nat_b, cmp_in, gates = _in_proj(x, g_mix.reshape(1, D), *_layer_weights(w_in), tab)

    a_out = []
    for arr, (w, d) in zip((a0, a1, a2), DIL_PAIRS):
        a_out.append(_band_attn(arr, arr, arr, q_blk=0, k_blk=1, v_blk=2, hq=A_SLOTS, hk=A_SLOTS,
                                max_off=w // d, tq=min(512, S // d), with_lse=True))
    (o0, l0), (o1, l1), (o2, l2) = a_out

    nch = S // CMP_STRIDE
    kv_w = G * HEAD_DIM
    k_slc = nat_b[:, :, B_Q + 2 * kv_w:B_Q + 3 * kv_w].reshape(B, S, G, HEAD_DIM)
    v_slc = nat_b[:, :, B_Q + 3 * kv_w:B_Q + 4 * kv_w].reshape(B, S, G, HEAD_DIM)
    pos = jnp.stack([cmp_pos_k, cmp_pos_v]).astype(BF16)[:, :, None, :]
    pos = jnp.pad(pos, ((0, 0), (0, 0), (0, G - 1), (0, 0))).reshape(2, 1, CMP_LEN * G * HEAD_DIM)
    pos = jnp.broadcast_to(pos, (2, 8, CMP_LEN * G * HEAD_DIM))
    w1 = jnp.stack([cmp_w1_k, cmp_w1_v]).astype(BF16).reshape(2, CMP_LEN, 1, HEAD_DIM, CMP_HIDDEN)
    w1 = jnp.broadcast_to(w1, (2, CMP_LEN, G, HEAD_DIM, CMP_HIDDEN)).reshape(
        2, CMP_LEN * G * HEAD_DIM, CMP_HIDDEN)
    kcvc = _compress(cmp_in, pos, w1, jnp.stack([cmp_w2_k, cmp_w2_v]).astype(BF16))
    ns = S // SEL_LEN
    o_cmp, selb = _cmp_attn(nat_b, kcvc[0], kcvc[1].transpose(0, 1, 3, 2),
                            _overlap_matrix(nch, ns).T)

    onehot = ((jnp.arange(S) // SEL_LEN)[:, None] == jnp.arange(ns)[None, :]).astype(BF16)
    k_aug = jnp.concatenate([k_slc.transpose(0, 2, 1, 3),
                             jnp.broadcast_to(onehot, (B, G, S, ns))], axis=3)
    ones_row = (jnp.arange(SLC_VROWS - HEAD_DIM) == 0).astype(BF16)[:, None]
    vt_aug = jnp.concatenate(
        [v_slc.transpose(0, 2, 3, 1),
         jnp.broadcast_to(ones_row, (B, G, SLC_VROWS - HEAD_DIM, S))], axis=2)
    o_slc = _slc_attn(nat_b, selb, k_aug, vt_aug)

    kw_blk = (B_Q + 4 * G * HEAD_DIM) // (G * HEAD_DIM)
    (o_win,) = _band_attn(nat_b, nat_b, nat_b, q_blk=0, k_blk=kw_blk, v_blk=kw_blk + 1,
                          hq=B_HEADS, hk=B_KV, max_off=WIN_LEN - 1, tq=512, with_lse=False)

    return _merge_ffn(x, o0, l0, o1, l1, o2, l2, o_cmp, o_slc, o_win, gates,
                      w_branch_a.astype(BF16), w_branch_b.astype(BF16), w_out.astype(BF16),
                      g_mlp.reshape(1, D), w_up.astype(BF16), w_down.astype(BF16),
                      g_final.reshape(1, D), final_norm)


def kernel(x, norm_mix_g, w_in, cmp_pos_k, cmp_w1_k, cmp_w2_k, cmp_pos_v, cmp_w1_v, cmp_w2_v,
           w_branch_a, w_branch_b, w_out, norm_mlp_g, w_up, w_down, norm_final_g):
    B, S, D = x.shape
    depth = w_in.shape[0]
    tab = _rope_table(S)
    for l in range(depth):
        x = _layer(x, norm_mix_g[l], w_in[l], cmp_pos_k[l], cmp_w1_k[l], cmp_w2_k[l],
                   cmp_pos_v[l], cmp_w1_v[l], cmp_w2_v[l], w_branch_a[l], w_branch_b[l],
                   w_out[l], norm_mlp_g[l], w_up[l], w_down[l], norm_final_g, l == depth - 1, tab)
    return x
```

```python
import functools

import jax
import jax.numpy as jnp
from jax import lax
from jax.experimental import pallas as pl
from jax.experimental.pallas import tpu as pltpu

F32 = jnp.float32
BF16 = jnp.bfloat16

D_MODEL = 1024
HEAD_DIM = 64
ROT_DIM = HEAD_DIM // 4
ROPE_THETA = 500000.0
EPS = 1e-6
DIL_PAIRS = ((128, 1), (512, 4), (2048, 16))
A_SLOTS = 8
A_HEADS = A_SLOTS * len(DIL_PAIRS)
B_HEADS = 8
B_KV = 2
CMP_LEN = 32
CMP_STRIDE = 16
CMP_HIDDEN = 4 * HEAD_DIM
SEL_LEN = 64
SEL_TOP = 16
WIN_LEN = 512
FORCE = 1e4
D_FF = 4 * D_MODEL
A_QKV = 3 * A_HEADS * HEAD_DIM
B_Q = B_HEADS * HEAD_DIM
B_KV_COLS = 6 * B_KV * HEAD_DIM
B_GATE = 3 * B_HEADS
MERGE_GATE = 2 * D_MODEL

LANES = 128
NEG = -1e30
QSCALE = HEAD_DIM ** -0.5 * 1.4426950408889634

A_COLS = 3 * A_SLOTS * HEAD_DIM
B_COLS = B_Q + B_KV_COLS
PROJ_CHUNK = 256
G_PAD = PROJ_CHUNK
G_COLS = MERGE_GATE + G_PAD

ROW_TILE = 512
VMEM_LIMIT = 56 * 1024 * 1024


def _cparams(n_axes, vmem=VMEM_LIMIT):
    return pltpu.CompilerParams(dimension_semantics=("arbitrary",) * n_axes, vmem_limit_bytes=vmem)


def _rms(x, g):
    return x * lax.rsqrt(jnp.mean(x * x, axis=-1, keepdims=True) + EPS) * g


def _dot(a, b):
    return jnp.dot(a, b, preferred_element_type=F32)


def _dot_nt(a, b):
    return lax.dot_general(a, b, (((1,), (1,)), ((), ())), preferred_element_type=F32)


def _reduce_rows(x, op):
    while x.shape[0] > 8 and x.shape[0] % 16 == 0:
        half = x.shape[0] // 2
        x = op(x[:half], x[half:])
    red = jnp.max if op is jnp.maximum else jnp.sum
    return red(x, axis=0, keepdims=True)


def _split_bf16(x, parts):
    out = []
    for _ in range(parts - 1):
        hi = x.astype(BF16)
        out.append(hi)
        x = x - hi.astype(F32)
    out.append(x.astype(BF16))
    return out


_B_ROT = (True,) * (B_Q // LANES) + (True, False, True, False, True, False)


def _rotary(x, cos_t, sin_a, sin_b):
    half = ROT_DIM // 2
    return x * cos_t + pltpu.roll(x, LANES - half, 1) * sin_a + pltpu.roll(x, half, 1) * sin_b


def _in_proj_kernel(x_ref, g_ref, w_ref, wg_ref, tab_ref,
                    a0_ref, a1_ref, a2_ref, b_ref, cmp_ref, gate_ref, u_scr):
    T = x_ref.shape[0]
    u = _rms(x_ref[...], g_ref[...])
    nslab = u_scr.shape[0]
    for k in range(nslab):
        u_scr[k] = u[:, k * LANES:(k + 1) * LANES]
    u_nat = u.astype(BF16)

    def dilated(load, d):
        rows = T // d
        return jnp.concatenate([load(pl.ds(r, rows, stride=d)) for r in range(d)], axis=0)

    def u_rows(rows):
        return jnp.concatenate([u_scr[k, rows, :] for k in range(nslab)], axis=1).astype(BF16)

    q_slabs = A_SLOTS * HEAD_DIM // LANES
    per = PROJ_CHUNK // LANES

    def project(lhs, weights, chunk_cols, rot, tables, store, n_q=0):
        for c, w_col in enumerate(chunk_cols):
            res = _dot(lhs, weights[:, w_col:w_col + PROJ_CHUNK])
            for k in range(per):
                slab = res[:, k * LANES:(k + 1) * LANES]
                if rot[c * per + k]:
                    slab = _rotary(slab, tables[0], tables[1], tables[2])
                if c * per + k < n_q:
                    slab = slab * QSCALE
                store((c * per + k) * LANES, slab)

    a_rot = (True,) * (2 * q_slabs) + (False,) * q_slabs
    group_w = A_SLOTS * HEAD_DIM

    def group_cols(g):
        return [t * A_HEADS * HEAD_DIM + g * group_w + c
                for t in range(3) for c in range(0, group_w, PROJ_CHUNK)]

    tab0 = tuple(tab_ref[k] for k in range(3))

    def store_a0(c0, res):
        a0_ref[:, c0:c0 + LANES] = res.astype(BF16)

    project(u_nat, w_ref, group_cols(0), a_rot, tab0, store_a0, n_q=q_slabs)

    for g, (d, a_ref) in enumerate(((4, a1_ref), (16, a2_ref)), start=1):
        rows = T // d
        tabs = tuple(dilated(lambda rs, k=k: tab_ref[k, rs, :], d) for k in range(3))

        def store_ad(c0, res, a_ref=a_ref, d=d, rows=rows):
            for r in range(d):
                a_ref[r, :, c0:c0 + LANES] = res[r * rows:(r + 1) * rows].astype(BF16)

        project(dilated(u_rows, d), w_ref, group_cols(g), a_rot, tabs, store_ad, n_q=q_slabs)

    def store_b(c0, res):
        b_ref[:, c0:c0 + LANES] = res.astype(BF16)
        if B_Q <= c0 < B_Q + 2 * LANES:
            cmp_ref[:, c0 - B_Q:c0 - B_Q + LANES] = res

    project(u_nat, w_ref, range(A_QKV, A_QKV + B_COLS, PROJ_CHUNK), _B_ROT, tab0, store_b,
            n_q=q_slabs)

    def store_g(c0, res):
        gate_ref[:, c0:c0 + LANES] = jax.nn.sigmoid(res).astype(gate_ref.dtype)

    project(u_nat, wg_ref, range(0, G_COLS, PROJ_CHUNK), (False,) * (G_COLS // LANES), tab0, store_g)


def _in_proj(x, g, w_main, w_gate, tab):
    B, S, D = x.shape
    T = ROW_TILE
    nt = S // T
    out_shape = (
        jax.ShapeDtypeStruct((B, S, A_COLS), BF16),
        jax.ShapeDtypeStruct((B * 4, S // 4, A_COLS), BF16),
        jax.ShapeDtypeStruct((B * 16, S // 16, A_COLS), BF16),
        jax.ShapeDtypeStruct((B, S, B_COLS), BF16),
        jax.ShapeDtypeStruct((B, S, 2 * LANES), F32),
        jax.ShapeDtypeStruct((B, S, G_COLS), BF16),
    )
    in_specs = [
        pl.BlockSpec((None, T, D), lambda b, i: (b, i, 0)),
        pl.BlockSpec((1, D), lambda b, i: (0, 0)),
        pl.BlockSpec(w_main.shape, lambda b, i: (0, 0), pipeline_mode=pl.Buffered(1)),
        pl.BlockSpec(w_gate.shape, lambda b, i: (0, 0), pipeline_mode=pl.Buffered(1)),
        pl.BlockSpec((3, T, LANES), lambda b, i: (0, i, 0)),
    ]
    out_specs = (
        pl.BlockSpec((None, T, A_COLS), lambda b, i: (b, i, 0)),
        pl.BlockSpec((4, T // 4, A_COLS), lambda b, i: (b, i, 0)),
        pl.BlockSpec((16, T // 16, A_COLS), lambda b, i: (b, i, 0)),
        pl.BlockSpec((None, T, B_COLS), lambda b, i: (b, i, 0)),
        pl.BlockSpec((None, T, 2 * LANES), lambda b, i: (b, i, 0)),
        pl.BlockSpec((None, T, G_COLS), lambda b, i: (b, i, 0)),
    )
    return pl.pallas_call(
        _in_proj_kernel, grid=(B, nt), in_specs=in_specs, out_specs=out_specs, out_shape=out_shape,
        scratch_shapes=[pltpu.VMEM((D // LANES, T, LANES), F32)], compiler_params=_cparams(2),
        name="in_proj",
    )(x, g, w_main, w_gate, tab)


def _gelu_tanh(x):
    return 0.5 * x * (1.0 + jnp.tanh(0.7978845608028654 * (x + 0.044715 * (x * x * x))))


def _compress_kernel(x_ref, pos_ref, w1_ref, w2_ref, o_ref):
    nch = x_ref.shape[0] // CMP_STRIDE
    half = CMP_STRIDE * LANES
    chunks = jnp.concatenate(
        [x_ref[pl.ds(j, nch, stride=CMP_STRIDE), :] for j in range(CMP_STRIDE)], axis=1).astype(BF16)
    lane_group = (lax.broadcasted_iota(jnp.int32, (1, half), 1) % LANES) // HEAD_DIM
    pb = _dot(pos_ref[...], w1_ref[...])[0:1]
    for g in range(B_KV):
        c = jnp.where(lane_group == g, chunks, jnp.zeros((), BF16))
        p1 = _dot(c, w1_ref[:half, :])
        p2 = _dot(c, w1_ref[half:, :])
        p2_next = pltpu.roll(p2, nch - 1, 0)
        h = _gelu_tanh(p1 + p2_next + pb)
        o_ref[g] = _dot(h.astype(BF16), w2_ref[...]).astype(BF16)


def _compress(cmp_in, pos, w1, w2):
    B, S, _ = cmp_in.shape
    nch = S // CMP_STRIDE
    kx = 2 * CMP_STRIDE * LANES
    return pl.pallas_call(
        _compress_kernel, grid=(2, B),
        in_specs=[
            pl.BlockSpec((None, S, LANES), lambda t, b: (b, 0, t)),
            pl.BlockSpec((None, 8, kx), lambda t, b: (t, 0, 0)),
            pl.BlockSpec((None, kx, CMP_HIDDEN), lambda t, b: (t, 0, 0)),
            pl.BlockSpec((None, CMP_HIDDEN, HEAD_DIM), lambda t, b: (t, 0, 0)),
        ],
        out_specs=pl.BlockSpec((None, None, B_KV, nch, HEAD_DIM), lambda t, b: (t, b, 0, 0, 0)),
        out_shape=jax.ShapeDtypeStruct((2, B, B_KV, nch, HEAD_DIM), BF16),
        compiler_params=_cparams(2), name="compress",
    )(cmp_in, pos, w1, w2)


CMP_TQ = 1024


CMP_KEY_CHUNK = 128


CMP_BLK_CHUNK = 16


def _round_up(x, m):
    return -(-x // m) * m


def _cmp_attn_kernel(q_ref, kc_ref, vct_ref, ovt_ref, o_ref, sel_ref):
    i = pl.program_id(1)
    tq = q_ref.shape[0]
    nc = kc_ref.shape[1]
    ns = ovt_ref.shape[0]
    rep = B_HEADS // B_KV
    removed = -3e38

    def tile(c):
        nk = min(nc, _round_up((c + 1) * tq // CMP_STRIDE, CMP_KEY_CHUNK))
        nb = min(ns, _round_up((c + 1) * tq // SEL_LEN, CMP_BLK_CHUNK))
        t = c * tq + lax.broadcasted_iota(jnp.int32, (1, tq), 1)
        n = lax.broadcasted_iota(jnp.int32, (nk, 1), 0)
        hidden = jnp.where((n * CMP_STRIDE + (CMP_LEN - 1)) <= t, 0.0, NEG)
        live = jnp.where(t >= CMP_LEN - 1, 1.0, 0.0)
        blk = lax.broadcasted_iota(jnp.int32, (nb, 1), 0)
        blk_f = blk.astype(F32)
        cur = jnp.right_shift(t, SEL_LEN.bit_length() - 1)
        forced = (blk == 0) | (blk == cur) | (blk == cur - 1)
        keep = jnp.where(forced | (blk > cur), 0.0, 1.0)
        pinned = jnp.where(forced, removed, jnp.where(blk > cur, -FORCE, 0.0))
        ovt = ovt_ref[:nb, :nk]
        o_t, imps = [], []
        for g in range(B_KV):
            kc = kc_ref[g, :nk, :]
            vct = vct_ref[g, :, :nk]
            psum = jnp.zeros((nk, tq), F32)
            for r in range(rep):
                h = g * rep + r
                qh = q_ref[:, h * HEAD_DIM:(h + 1) * HEAD_DIM]
                s = _dot_nt(kc, qh) + hidden
                p = jnp.exp2(s - _reduce_rows(s, jnp.maximum))
                p = p * (live / _reduce_rows(p, jnp.add))
                o_t.append(_dot(vct, p.astype(BF16)))
                psum = psum + p
            imp = sum(_dot(ovt, piece) for piece in _split_bf16(psum, 3))
            imps.append(imp * keep + pinned)
        o_ref[...] = jnp.transpose(jnp.concatenate(o_t, axis=0)).astype(BF16)

        imp = jnp.concatenate(imps, axis=1)

        def take_next(imp, allow=None):
            mx = jnp.max(imp, axis=0, keepdims=True)
            first = jnp.min(jnp.where(imp == mx, blk_f, float(ns)), axis=0, keepdims=True)
            hit = blk_f == first
            if allow is not None:
                hit = hit & allow
            return jnp.where(hit, removed, imp)

        n_forced_max = 3
        for _ in range(SEL_TOP - n_forced_max):
            imp = take_next(imp)
        if c * tq < 2 * SEL_LEN:
            t2 = jnp.concatenate([t] * B_KV, axis=1)
            imp = take_next(imp, t2 < 2 * SEL_LEN)
            imp = take_next(imp, t2 < SEL_LEN)
        for g in range(B_KV):
            picked = imp[:, g * tq:(g + 1) * tq] < 0.5 * removed
            sel_ref[g, :nb, :] = jnp.where(picked, 0.0, NEG).astype(BF16)
            if nb < ns:
                sel_ref[g, nb:, :] = jnp.full((ns - nb, tq), NEG, BF16)

    for c in range(nc * CMP_STRIDE // tq):
        pl.when(i == c)(functools.partial(tile, c))


def _cmp_attn(nat_b, kc, vct, overlap_t):
    B, S, _ = nat_b.shape
    nc = kc.shape[2]
    ns = overlap_t.shape[0]
    tq = CMP_TQ
    return pl.pallas_call(
        _cmp_attn_kernel, grid=(B, S // tq),
        in_specs=[
            pl.BlockSpec((None, tq, B_Q), lambda b, i: (b, i, 0)),
            pl.BlockSpec((None, B_KV, nc, HEAD_DIM), lambda b, i: (b, 0, 0, 0)),
            pl.BlockSpec((None, B_KV, HEAD_DIM, nc), lambda b, i: (b, 0, 0, 0)),
            pl.BlockSpec((ns, nc), lambda b, i: (0, 0)),
        ],
        out_specs=(
            pl.BlockSpec((None, tq, B_Q), lambda b, i: (b, i, 0)),
            pl.BlockSpec((None, B_KV, ns, tq), lambda b, i: (b, 0, 0, i)),
        ),
        out_shape=(
            jax.ShapeDtypeStruct((B, S, B_Q), BF16),
            jax.ShapeDtypeStruct((B, B_KV, ns, S), BF16),
        ),
        compiler_params=_cparams(2), name="cmp_attn",
    )(nat_b, kc, vct, overlap_t)


SLC_TK = 256
SLC_TQ = 2 * SLC_TK


SLC_VROWS = 80
SLC_CW = 256


def _slc_attn_kernel(q_ref, selt_ref, k_ref, vt_ref, o_ref, qat_scr, m_scr, acc_scr, s_a, s_b):
    i = pl.program_id(2)
    tq = q_ref.shape[0]
    rep = B_HEADS // B_KV
    selt = selt_ref[...]
    for r in range(rep):
        cols = slice(r * tq, (r + 1) * tq)
        q_r = q_ref[:, r * HEAD_DIM:(r + 1) * HEAD_DIM].astype(F32)
        qat_scr[:HEAD_DIM, cols] = jnp.transpose(q_r).astype(BF16)
        qat_scr[HEAD_DIM:, cols] = selt
    m_scr[...] = jnp.full(m_scr.shape, NEG, F32)
    acc_scr[...] = jnp.zeros(acc_scr.shape, F32)
    rel = (lax.broadcasted_iota(jnp.int32, (SLC_TK, tq), 1)
           - lax.broadcasted_iota(jnp.int32, (SLC_TK, tq), 0))

    def scores(j, s_buf):
        k0 = pl.multiple_of(j * SLC_TK, SLC_TK)
        s_buf[...] = _dot(k_ref[pl.ds(k0, SLC_TK), :], qat_scr[...])

    def update(j, s_buf, diag_offset=None):
        k0 = pl.multiple_of(j * SLC_TK, SLC_TK)
        vt = vt_ref[:, pl.ds(k0, SLC_TK)]
        for c in range(rep * tq // SLC_CW):
            cols = slice(c * SLC_CW, (c + 1) * SLC_CW)
            q_lo = (c * SLC_CW) % tq
            if diag_offset is not None and q_lo + SLC_CW <= diag_offset:
                continue
            s = s_buf[:, cols]
            if diag_offset is not None:
                s = jnp.where(rel[:, q_lo:q_lo + SLC_CW] >= diag_offset, s, NEG)
            m_prev = m_scr[:, cols]
            m_new = jnp.maximum(m_prev, jnp.max(s, axis=0, keepdims=True))
            alpha = jnp.exp2(m_prev - m_new)
            p = jnp.exp2(s - m_new).astype(BF16)
            acc_scr[:, cols] = alpha * acc_scr[:, cols] + _dot(vt, p)
            m_scr[:, cols] = m_new

    def tile_pair(jj, carry):
        a = 2 * jj
        scores(a + 1, s_b)
        update(a, s_a)
        scores(a + 2, s_a)
        update(a + 1, s_b)
        return carry

    def two_pairs(jj, carry):
        tile_pair(2 * jj, carry)
        return tile_pair(2 * jj + 1, carry)

    scores(0, s_a)
    lax.fori_loop(0, i // 2, two_pairs, 0)

    @pl.when(i % 2 == 1)
    def _():
        tile_pair(i - 1, 0)

    scores(2 * i + 1, s_b)
    update(2 * i, s_a, diag_offset=0)
    update(2 * i + 1, s_b, diag_offset=SLC_TK)

    for r in range(rep):
        acc = acc_scr[:, r * tq:(r + 1) * tq]
        o_t = acc[:HEAD_DIM] / acc[HEAD_DIM:HEAD_DIM + 1]
        o_ref[:, r * HEAD_DIM:(r + 1) * HEAD_DIM] = jnp.transpose(o_t).astype(BF16)


def _slc_attn(nat_b, selt, k_aug, vt_aug):
    B, S, _ = nat_b.shape
    G = B_KV
    rep = B_HEADS // B_KV
    ns = selt.shape[2]
    tq = SLC_TQ
    gw = rep * HEAD_DIM
    return pl.pallas_call(
        _slc_attn_kernel, grid=(B, G, S // tq),
        in_specs=[
            pl.BlockSpec((None, tq, gw), lambda b, g, i: (b, i, g)),
            pl.BlockSpec((None, None, ns, tq), lambda b, g, i: (b, g, 0, i)),
            pl.BlockSpec((None, None, S, HEAD_DIM + ns), lambda b, g, i: (b, g, 0, 0)),
            pl.BlockSpec((None, None, SLC_VROWS, S), lambda b, g, i: (b, g, 0, 0)),
        ],
        out_specs=pl.BlockSpec((None, tq, gw), lambda b, g, i: (b, i, g)),
        out_shape=jax.ShapeDtypeStruct((B, S, B_Q), BF16),
        scratch_shapes=[pltpu.VMEM((HEAD_DIM + ns, rep * tq), BF16),
                        pltpu.VMEM((1, rep * tq), F32), pltpu.VMEM((SLC_VROWS, rep * tq), F32),
                        pltpu.VMEM((SLC_TK, rep * tq), F32), pltpu.VMEM((SLC_TK, rep * tq), F32)],
        compiler_params=_cparams(3), name="slc_attn",
    )(nat_b, selt, k_aug, vt_aug)


SUB = 128
BAND_TQ = 1024


def _band_kernel(q_ref, kp_ref, kc_ref, vp_ref, vc_ref, o_ref, *lse_refs, shared_kv, prev, max_off):
    i = pl.program_id(1)
    tq = q_ref.shape[0]
    n_slab = q_ref.shape[1] // LANES
    width = prev + SUB
    r_i = lax.broadcasted_iota(jnp.int32, (SUB, width), 0)
    c_i = lax.broadcasted_iota(jnp.int32, (SUB, width), 1)
    diff = r_i - c_i + prev
    in_band = (diff >= 0) & (diff <= max_off)
    lane = lax.broadcasted_iota(jnp.int32, (1, LANES), 1)
    lo_half = lane < HEAD_DIM
    half_mask = {"lo": lo_half, "hi": jnp.logical_not(lo_half)}

    def window(p_ref, c_ref, sub, slab):
        cols = slice(slab * LANES, (slab + 1) * LANES)
        lo = sub * SUB
        parts = []
        if lo < prev:
            parts.append(p_ref[lo:prev, cols])
        parts.append(c_ref[max(0, lo - prev):lo + SUB, cols])
        return jnp.concatenate(parts, axis=0) if len(parts) > 1 else parts[0]

    def swap_halves(x):
        return jnp.concatenate([x[:, HEAD_DIM:], x[:, :HEAD_DIM]], axis=1)

    for sub in range(tq // SUB):
        rows = slice(sub * SUB, (sub + 1) * SUB)
        k_start = i * tq + sub * SUB - prev
        bias = jnp.where(in_band & (c_i + k_start >= 0), 0.0, NEG)
        if shared_kv:
            k2 = window(kp_ref, kc_ref, sub, 0)
            v2 = window(vp_ref, vc_ref, sub, 0)
            groups = [
                (((0, "lo"), (1, "lo"), (2, "hi"), (3, "hi")), k2, v2),
                (((0, "hi"), (1, "hi"), (2, "lo"), (3, "lo")), swap_halves(k2), swap_halves(v2)),
            ]
        else:
            groups = [(((p, "lo"), (p, "hi")), window(kp_ref, kc_ref, sub, p),
                       window(vp_ref, vc_ref, sub, p)) for p in range(n_slab)]
        outs = {}
        for members, k2, v2 in groups:
            nm = len(members)
            zero = jnp.zeros((), BF16)
            lhs = jnp.concatenate(
                [jnp.where(half_mask[half], q_ref[rows, slab * LANES:(slab + 1) * LANES], zero)
                 for slab, half in members], axis=0)
            s = _dot_nt(lhs, k2)
            s = (s.reshape(nm, SUB, width) + bias[None]).reshape(nm * SUB, width)
            m = jnp.max(s, axis=-1, keepdims=True)
            p = jnp.exp2(s - m)
            den = jnp.sum(p, axis=-1, keepdims=True)
            pv = _dot(p.astype(BF16), v2) / den
            lse = m + jnp.log2(den)
            for idx, member in enumerate(members):
                outs[member] = (pv[idx * SUB:(idx + 1) * SUB], lse[idx * SUB:(idx + 1) * SUB])
        lse_tile = jnp.zeros((SUB, LANES), F32)
        for slab in range(n_slab):
            cols = slice(slab * LANES, (slab + 1) * LANES)
            (o_lo, l_lo), (o_hi, l_hi) = outs[(slab, "lo")], outs[(slab, "hi")]
            o_ref[rows, cols] = jnp.where(lo_half, o_lo, o_hi).astype(o_ref.dtype)
            lse_tile = jnp.where(lane == 2 * slab, l_lo,
                                 jnp.where(lane == 2 * slab + 1, l_hi, lse_tile))
        if lse_refs:
            lse_refs[0][rows, :] = lse_tile


def _band_attn(q_arr, k_arr, v_arr, *, q_blk, k_blk, v_blk, hq, hk, max_off, tq, with_lse):
    N, L, _ = q_arr.shape
    assert hq == hk or (hq, hk) == (8, 2), (hq, hk)
    prev = -(-max_off // SUB) * SUB
    m = tq // prev
    qw, kw = hq * HEAD_DIM, hk * HEAD_DIM
    cur = lambda blk: (lambda n, i: (n, i, blk))
    prv = lambda blk: (lambda n, i: (n, jnp.maximum(i * m - 1, 0), blk))
    out_shape = [jax.ShapeDtypeStruct((N, L, qw), BF16)]
    out_specs = [pl.BlockSpec((None, tq, qw), cur(0))]
    if with_lse:
        out_shape.append(jax.ShapeDtypeStruct((N, L, LANES), F32))
        out_specs.append(pl.BlockSpec((None, tq, LANES), cur(0)))
    return pl.pallas_call(
        functools.partial(_band_kernel, shared_kv=hq != hk, prev=prev, max_off=max_off),
        grid=(N, L // tq),
        in_specs=[
            pl.BlockSpec((None, tq, qw), cur(q_blk)),
            pl.BlockSpec((None, prev, kw), prv(k_blk)),
            pl.BlockSpec((None, tq, kw), cur(k_blk)),
            pl.BlockSpec((None, prev, kw), prv(v_blk)),
            pl.BlockSpec((None, tq, kw), cur(v_blk)),
        ],
        out_specs=tuple(out_specs), out_shape=tuple(out_shape),
        compiler_params=_cparams(2), name="band_attn",
    )(q_arr, k_arr, k_arr, v_arr, v_arr)


FF_CHUNK = 512

def _merge_ffn_kernel(x_ref, o0_ref, l0_ref, o1_ref, l1_ref, o2_ref, l2_ref, oc_ref, os_ref, ow_ref,
                      gate_ref, wa_ref, wb_ref, wo_ref, gm_ref, wu_ref, wd_ref, gf_ref,
                      out_ref, s_o1, s_l1, s_o2, s_l2, x1_scr, *, final_norm):
    step = pl.program_id(0)
    T = x_ref.shape[0]

    @pl.when(step == 0)
    def _():
        x1_scr[1] = jnp.zeros(x1_scr.shape[1:], F32)

    x1_prev = x1_scr[(step + 1) % 2]
    u = _rms(x1_prev, gm_ref[...]).astype(BF16)
    acc = x1_prev
    for c in range(D_FF // FF_CHUNK):
        cols = slice(c * FF_CHUNK, (c + 1) * FF_CHUNK)
        h = jnp.maximum(_dot(u, wu_ref[:, cols]), 0.0)
        acc = acc + _dot((h * h).astype(BF16), wd_ref[cols, :])
    out_ref[...] = _rms(acc, gf_ref[...]) if final_norm else acc

    for d, o_ref, l_ref, s_o, s_l in ((4, o1_ref, l1_ref, s_o1, s_l1), (16, o2_ref, l2_ref, s_o2, s_l2)):
        rows = T // d
        for r in range(d):
            o_r = o_ref[r].astype(F32)
            s_l[pl.ds(r, rows, stride=d), :] = l_ref[r]
            for k in range(s_o.shape[0]):
                s_o[k, pl.ds(r, rows, stride=d), :] = o_r[:, k * LANES:(k + 1) * LANES]
    gather = lambda s: jnp.concatenate([s[k] for k in range(s.shape[0])], axis=1)

    l0, l1, l2 = l0_ref[...], s_l1[...], s_l2[...]
    mx = jnp.maximum(jnp.maximum(l0, l1), l2)
    e0, e1, e2 = jnp.exp2(l0 - mx), jnp.exp2(l1 - mx), jnp.exp2(l2 - mx)
    inv = 1.0 / (e0 + e1 + e2)
    lo_half = lax.broadcasted_iota(jnp.int32, (1, LANES), 1) < HEAD_DIM

    def spread(w, lane_of_head):
        col = lambda h: jnp.broadcast_to(w[:, lane_of_head(h):lane_of_head(h) + 1], (T, LANES))
        return jnp.concatenate([jnp.where(lo_half, col(2 * p), col(2 * p + 1))
                                for p in range(A_SLOTS // 2)], axis=1)

    slot = lambda h: h
    y_a = (spread(e0 * inv, slot) * o0_ref[...].astype(F32) + spread(e1 * inv, slot) * gather(s_o1)
           + spread(e2 * inv, slot) * gather(s_o2))

    gb = gate_ref[:, MERGE_GATE:MERGE_GATE + LANES].astype(F32)
    y_b = (spread(gb, lambda h: 3 * h) * oc_ref[...].astype(F32)
           + spread(gb, lambda h: 3 * h + 1) * os_ref[...].astype(F32)
           + spread(gb, lambda h: 3 * h + 2) * ow_ref[...].astype(F32))
    merged = (gate_ref[:, :D_MODEL].astype(F32) * _dot(y_a.astype(BF16), wa_ref[...])
              + gate_ref[:, D_MODEL:MERGE_GATE].astype(F32) * _dot(y_b.astype(BF16), wb_ref[...]))
    x1_scr[step % 2] = x_ref[...] + _dot(merged.astype(BF16), wo_ref[...])


def _merge_ffn(x, o0, l0, o1, l1, o2, l2, o_cmp, o_slc, o_win, gates, wa, wb, wo,
               g_mlp, w_up, w_down, g_final, final_norm):
    B, S, D = x.shape
    T = ROW_TILE
    nt = S // T
    n_tiles = B * nt
    qw = A_SLOTS * HEAD_DIM

    def tile_map(shift):
        def index_map(k):
            m = jnp.clip(k - shift, 0, n_tiles - 1)
            return (m // nt, m % nt, 0)
        return index_map

    nat = lambda w, shift=0: pl.BlockSpec((None, T, w), tile_map(shift))
    dil = lambda d, w: pl.BlockSpec((d, T // d, w), tile_map(0))
    full = lambda a: pl.BlockSpec(a.shape, lambda k: (0,) * a.ndim, pipeline_mode=pl.Buffered(1))
    o_scr, l_scr = pltpu.VMEM((qw // LANES, T, LANES), F32), pltpu.VMEM((T, LANES), F32)
    return pl.pallas_call(
        functools.partial(_merge_ffn_kernel, final_norm=final_norm), grid=(n_tiles + 1,),
        in_specs=[nat(D), nat(qw), nat(LANES), dil(4, qw), dil(4, LANES), dil(16, qw),
                  dil(16, LANES), nat(B_Q), nat(B_Q), nat(B_Q), nat(G_COLS), full(wa),
                  full(wb), full(wo), full(g_mlp), full(w_up), full(w_down), full(g_final)],
        out_specs=nat(D, shift=1), out_shape=jax.ShapeDtypeStruct((B, S, D), F32),
        scratch_shapes=[o_scr, l_scr, o_scr, l_scr, pltpu.VMEM((2, T, D), F32)],
        compiler_params=_cparams(1), name="merge_ffn",
    )(x, o0, l0, o1, l1, o2, l2, o_cmp, o_slc, o_win, gates, wa, wb, wo,
      g_mlp, w_up, w_down, g_final)


def _rope_table(seq):
    half = ROT_DIM // 2
    pos = jnp.arange(seq, dtype=F32)[:, None]
    inv = ROPE_THETA ** (-jnp.arange(0, ROT_DIM, 2, dtype=F32) / ROT_DIM)
    ang = pos * inv[None, :]
    cos, sin = jnp.cos(ang), jnp.sin(ang)
    pad = lambda parts, fill: jnp.tile(jnp.concatenate(
        parts + [jnp.full((seq, HEAD_DIM - ROT_DIM), fill, F32)], axis=1), (1, LANES // HEAD_DIM))
    zero = jnp.zeros_like(sin)
    return jnp.stack([pad([cos, cos], 1.0), pad([-sin, zero], 0.0), pad([zero, sin], 0.0)])


def _layer_weights(w_in):
    o3 = A_QKV + B_COLS
    o4 = o3 + B_GATE
    w_main = w_in.astype(BF16)
    w_gate = jnp.concatenate(
        [w_main[:, o4:], jnp.pad(w_main[:, o3:o4], ((0, 0), (0, G_PAD - B_GATE)))], axis=1)
    return w_main, w_gate


def _overlap_matrix(nc_pad, ns):
    c_start = jnp.arange(nc_pad) * CMP_STRIDE
    s_start = jnp.arange(ns) * SEL_LEN
    ov = (c_start[:, None] < s_start[None, :] + SEL_LEN) & (c_start[:, None] + CMP_LEN > s_start[None, :])
    return ov.astype(BF16)


def _layer(x, g_mix, w_in, cmp_pos_k, cmp_w1_k, cmp_w2_k, cmp_pos_v, cmp_w1_v, cmp_w2_v,
           w_branch_a, w_branch_b, w_out, g_mlp, w_up, w_down, g_final, final_norm, tab):
    B, S, D = x.shape
    G = B_KV
    a0, a1, a2, nat_b, cmp_in, gates = _in_proj(x, g_mix.reshape(1, D), *_layer_weights(w_in), tab)

    a_out = []
    for arr, (w, d) in zip((a0, a1, a2), DIL_PAIRS):
        a_out.append(_band_attn(arr, arr, arr, q_blk=0, k_blk=1, v_blk=2, hq=A_SLOTS, hk=A_SLOTS,
                                max_off=w // d, tq=min(BAND_TQ, S // d), with_lse=True))
    (o0, l0), (o1, l1), (o2, l2) = a_out

    nch = S // CMP_STRIDE
    kv_w = G * HEAD_DIM
    k_slc = nat_b[:, :, B_Q + 2 * kv_w:B_Q + 3 * kv_w].reshape(B, S, G, HEAD_DIM)
    v_slc = nat_b[:, :, B_Q + 3 * kv_w:B_Q + 4 * kv_w].reshape(B, S, G, HEAD_DIM)
    pos = jnp.stack([cmp_pos_k, cmp_pos_v]).astype(BF16)[:, :, None, :]
    pos = jnp.pad(pos, ((0, 0), (0, 0), (0, G - 1), (0, 0))).reshape(2, 1, CMP_LEN * G * HEAD_DIM)
    pos = jnp.broadcast_to(pos, (2, 8, CMP_LEN * G * HEAD_DIM))
    w1 = jnp.stack([cmp_w1_k, cmp_w1_v]).astype(BF16).reshape(2, CMP_LEN, 1, HEAD_DIM, CMP_HIDDEN)
    w1 = jnp.broadcast_to(w1, (2, CMP_LEN, G, HEAD_DIM, CMP_HIDDEN)).reshape(
        2, CMP_LEN * G * HEAD_DIM, CMP_HIDDEN)
    kcvc = _compress(cmp_in, pos, w1, jnp.stack([cmp_w2_k, cmp_w2_v]).astype(BF16))
    ns = S // SEL_LEN
    o_cmp, selb = _cmp_attn(nat_b, kcvc[0], kcvc[1].transpose(0, 1, 3, 2),
                            _overlap_matrix(nch, ns).T)

    onehot = ((jnp.arange(S) // SEL_LEN)[:, None] == jnp.arange(ns)[None, :]).astype(BF16)
    k_aug = jnp.concatenate([k_slc.transpose(0, 2, 1, 3),
                             jnp.broadcast_to(onehot, (B, G, S, ns))], axis=3)
    ones_row = (jnp.arange(SLC_VROWS - HEAD_DIM) == 0).astype(BF16)[:, None]
    vt_aug = jnp.concatenate(
        [v_slc.transpose(0, 2, 3, 1),
         jnp.broadcast_to(ones_row, (B, G, SLC_VROWS - HEAD_DIM, S))], axis=2)
    o_slc = _slc_attn(nat_b, selb, k_aug, vt_aug)

    kw_blk = (B_Q + 4 * G * HEAD_DIM) // (G * HEAD_DIM)
    (o_win,) = _band_attn(nat_b, nat_b, nat_b, q_blk=0, k_blk=kw_blk, v_blk=kw_blk + 1,
                          hq=B_HEADS, hk=B_KV, max_off=WIN_LEN - 1, tq=min(BAND_TQ, S),
                          with_lse=False)

    return _merge_ffn(x, o0, l0, o1, l1, o2, l2, o_cmp, o_slc, o_win, gates,
                      w_branch_a.astype(BF16), w_branch_b.astype(BF16), w_out.astype(BF16),
                      g_mlp.reshape(1, D), w_up.astype(BF16), w_down.astype(BF16),
                      g_final.reshape(1, D), final_norm)


def kernel(x, norm_mix_g, w_in, cmp_pos_k, cmp_w1_k, cmp_w2_k, cmp_pos_v, cmp_w1_v, cmp_w2_v,
           w_branch_a, w_branch_b, w_out, norm_mlp_g, w_up, w_down, norm_final_g):
    B, S, D = x.shape
    depth = w_in.shape[0]
    tab = _rope_table(S)
    for l in range(depth):
        x = _layer(x, norm_mix_g[l], w_in[l], cmp_pos_k[l], cmp_w1_k[l], cmp_w2_k[l],
                   cmp_pos_v[l], cmp_w1_v[l], cmp_w2_v[l], w_branch_a[l], w_branch_b[l],
                   w_out[l], norm_mlp_g[l], w_up[l], w_down[l], norm_final_g, l == depth - 1, tab)
    return x
```

```python
import functools

import jax
import jax.numpy as jnp
from jax import lax
from jax.experimental import pallas as pl
from jax.experimental.pallas import tpu as pltpu

F32 = jnp.float32
BF16 = jnp.bfloat16

D_MODEL = 1024
HEAD_DIM = 64
ROT_DIM = HEAD_DIM // 4
ROPE_THETA = 500000.0
EPS = 1e-6
DIL_PAIRS = ((128, 1), (512, 4), (2048, 16))
A_SLOTS = 8
A_HEADS = A_SLOTS * len(DIL_PAIRS)
B_HEADS = 8
B_KV = 2
CMP_LEN = 32
CMP_STRIDE = 16
CMP_HIDDEN = 4 * HEAD_DIM
SEL_LEN = 64
SEL_TOP = 16
WIN_LEN = 512
FORCE = 1e4
D_FF = 4 * D_MODEL
A_QKV = 3 * A_HEADS * HEAD_DIM
B_Q = B_HEADS * HEAD_DIM
B_KV_COLS = 6 * B_KV * HEAD_DIM
B_GATE = 3 * B_HEADS
MERGE_GATE = 2 * D_MODEL

LANES = 128
NEG = -1e30
QSCALE = HEAD_DIM ** -0.5 * 1.4426950408889634

A_COLS = 3 * A_SLOTS * HEAD_DIM
B_COLS = B_Q + B_KV_COLS
PROJ_CHUNK = 256
G_PAD = PROJ_CHUNK
G_COLS = MERGE_GATE + G_PAD

ROW_TILE = 512
VMEM_LIMIT = 56 * 1024 * 1024


def _cparams(n_axes, vmem=VMEM_LIMIT):
    return pltpu.CompilerParams(dimension_semantics=("arbitrary",) * n_axes, vmem_limit_bytes=vmem)


def _rms(x, g):
    return x * lax.rsqrt(jnp.mean(x * x, axis=-1, keepdims=True) + EPS) * g


def _dot(a, b):
    return jnp.dot(a, b, preferred_element_type=F32)


def _dot_nt(a, b):
    return lax.dot_general(a, b, (((1,), (1,)), ((), ())), preferred_element_type=F32)


def _reduce_rows(x, op):
    while x.shape[0] > 8 and x.shape[0] % 16 == 0:
        half = x.shape[0] // 2
        x = op(x[:half], x[half:])
    red = jnp.max if op is jnp.maximum else jnp.sum
    return red(x, axis=0, keepdims=True)


def _split_bf16(x, parts):
    out = []
    for _ in range(parts - 1):
        hi = x.astype(BF16)
        out.append(hi)
        x = x - hi.astype(F32)
    out.append(x.astype(BF16))
    return out


_B_ROT = (True,) * (B_Q // LANES) + (True, False, True, False, True, False)


def _rotary(x, cos_t, sin_a, sin_b):
    half = ROT_DIM // 2
    return x * cos_t + pltpu.roll(x, LANES - half, 1) * sin_a + pltpu.roll(x, half, 1) * sin_b


def _in_proj_kernel(x_ref, g_ref, w_ref, wg_ref, tab_ref,
                    a0_ref, a1_ref, a2_ref, b_ref, cmp_ref, gate_ref, u_scr):
    T = x_ref.shape[0]
    u = _rms(x_ref[...], g_ref[...])
    nslab = u_scr.shape[0]
    for k in range(nslab):
        u_scr[k] = u[:, k * LANES:(k + 1) * LANES]
    u_nat = u.astype(BF16)

    def dilated(load, d):
        rows = T // d
        return jnp.concatenate([load(pl.ds(r, rows, stride=d)) for r in range(d)], axis=0)

    def u_rows(rows):
        return jnp.concatenate([u_scr[k, rows, :] for k in range(nslab)], axis=1).astype(BF16)

    q_slabs = A_SLOTS * HEAD_DIM // LANES
    per = PROJ_CHUNK // LANES

    def project(lhs, weights, chunk_cols, rot, tables, store, n_q=0):
        for c, w_col in enumerate(chunk_cols):
            res = _dot(lhs, weights[:, w_col:w_col + PROJ_CHUNK])
            for k in range(per):
                slab = res[:, k * LANES:(k + 1) * LANES]
                if rot[c * per + k]:
                    slab = _rotary(slab, tables[0], tables[1], tables[2])
                if c * per + k < n_q:
                    slab = slab * QSCALE
                store((c * per + k) * LANES, slab)

    a_rot = (True,) * (2 * q_slabs) + (False,) * q_slabs
    group_w = A_SLOTS * HEAD_DIM

    def group_cols(g):
        return [t * A_HEADS * HEAD_DIM + g * group_w + c
                for t in range(3) for c in range(0, group_w, PROJ_CHUNK)]

    tab0 = tuple(tab_ref[k] for k in range(3))

    def store_a0(c0, res):
        a0_ref[:, c0:c0 + LANES] = res.astype(BF16)

    project(u_nat, w_ref, group_cols(0), a_rot, tab0, store_a0, n_q=q_slabs)

    for g, (d, a_ref) in enumerate(((4, a1_ref), (16, a2_ref)), start=1):
        rows = T // d
        tabs = tuple(dilated(lambda rs, k=k: tab_ref[k, rs, :], d) for k in range(3))

        def store_ad(c0, res, a_ref=a_ref, d=d, rows=rows):
            for r in range(d):
                a_ref[r, :, c0:c0 + LANES] = res[r * rows:(r + 1) * rows].astype(BF16)

        project(dilated(u_rows, d), w_ref, group_cols(g), a_rot, tabs, store_ad, n_q=q_slabs)

    def store_b(c0, res):
        b_ref[:, c0:c0 + LANES] = res.astype(BF16)
        if B_Q <= c0 < B_Q + 2 * LANES:
            cmp_ref[:, c0 - B_Q:c0 - B_Q + LANES] = res

    project(u_nat, w_ref, range(A_QKV, A_QKV + B_COLS, PROJ_CHUNK), _B_ROT, tab0, store_b,
            n_q=q_slabs)

    def store_g(c0, res):
        gate_ref[:, c0:c0 + LANES] = jax.nn.sigmoid(res).astype(gate_ref.dtype)

    project(u_nat, wg_ref, range(0, G_COLS, PROJ_CHUNK), (False,) * (G_COLS // LANES), tab0, store_g)


def _in_proj(x, g, w_main, w_gate, tab):
    B, S, D = x.shape
    T = ROW_TILE
    nt = S // T
    out_shape = (
        jax.ShapeDtypeStruct((B, S, A_COLS), BF16),
        jax.ShapeDtypeStruct((B * 4, S // 4, A_COLS), BF16),
        jax.ShapeDtypeStruct((B * 16, S // 16, A_COLS), BF16),
        jax.ShapeDtypeStruct((B, S, B_COLS), BF16),
        jax.ShapeDtypeStruct((B, S, 2 * LANES), F32),
        jax.ShapeDtypeStruct((B, S, G_COLS), BF16),
    )
    in_specs = [
        pl.BlockSpec((None, T, D), lambda b, i: (b, i, 0)),
        pl.BlockSpec((1, D), lambda b, i: (0, 0)),
        pl.BlockSpec(w_main.shape, lambda b, i: (0, 0), pipeline_mode=pl.Buffered(1)),
        pl.BlockSpec(w_gate.shape, lambda b, i: (0, 0), pipeline_mode=pl.Buffered(1)),
        pl.BlockSpec((3, T, LANES), lambda b, i: (0, i, 0)),
    ]
    out_specs = (
        pl.BlockSpec((None, T, A_COLS), lambda b, i: (b, i, 0)),
        pl.BlockSpec((4, T // 4, A_COLS), lambda b, i: (b, i, 0)),
        pl.BlockSpec((16, T // 16, A_COLS), lambda b, i: (b, i, 0)),
        pl.BlockSpec((None, T, B_COLS), lambda b, i: (b, i, 0)),
        pl.BlockSpec((None, T, 2 * LANES), lambda b, i: (b, i, 0)),
        pl.BlockSpec((None, T, G_COLS), lambda b, i: (b, i, 0)),
    )
    return pl.pallas_call(
        _in_proj_kernel, grid=(B, nt), in_specs=in_specs, out_specs=out_specs, out_shape=out_shape,
        scratch_shapes=[pltpu.VMEM((D // LANES, T, LANES), F32)], compiler_params=_cparams(2),
        name="in_proj",
    )(x, g, w_main, w_gate, tab)


def _gelu_tanh(x):
    return 0.5 * x * (1.0 + jnp.tanh(0.7978845608028654 * (x + 0.044715 * (x * x * x))))


def _compress_kernel(x_ref, pos_ref, w1_ref, w2_ref, o_ref):
    nch = x_ref.shape[0] // CMP_STRIDE
    half = CMP_STRIDE * LANES
    chunks = jnp.concatenate(
        [x_ref[pl.ds(j, nch, stride=CMP_STRIDE), :] for j in range(CMP_STRIDE)], axis=1).astype(BF16)
    lane_group = (lax.broadcasted_iota(jnp.int32, (1, half), 1) % LANES) // HEAD_DIM
    pb = _dot(pos_ref[...], w1_ref[...])[0:1]
    for g in range(B_KV):
        c = jnp.where(lane_group == g, chunks, jnp.zeros((), BF16))
        p1 = _dot(c, w1_ref[:half, :])
        p2 = _dot(c, w1_ref[half:, :])
        p2_next = pltpu.roll(p2, nch - 1, 0)
        h = _gelu_tanh(p1 + p2_next + pb)
        o_ref[g] = _dot(h.astype(BF16), w2_ref[...]).astype(BF16)


def _compress(cmp_in, pos, w1, w2):
    B, S, _ = cmp_in.shape
    nch = S // CMP_STRIDE
    kx = 2 * CMP_STRIDE * LANES
    return pl.pallas_call(
        _compress_kernel, grid=(2, B),
        in_specs=[
            pl.BlockSpec((None, S, LANES), lambda t, b: (b, 0, t)),
            pl.BlockSpec((None, 8, kx), lambda t, b: (t, 0, 0)),
            pl.BlockSpec((None, kx, CMP_HIDDEN), lambda t, b: (t, 0, 0)),
            pl.BlockSpec((None, CMP_HIDDEN, HEAD_DIM), lambda t, b: (t, 0, 0)),
        ],
        out_specs=pl.BlockSpec((None, None, B_KV, nch, HEAD_DIM), lambda t, b: (t, b, 0, 0, 0)),
        out_shape=jax.ShapeDtypeStruct((2, B, B_KV, nch, HEAD_DIM), BF16),
        compiler_params=_cparams(2), name="compress",
    )(cmp_in, pos, w1, w2)


CMP_TQ = 1024


CMP_KEY_CHUNK = 128


def _cmp_attn_kernel(q_ref, kc_ref, vct_ref, ovt_ref, o_ref, sel_ref, imp_scr):
    i = pl.program_id(1)
    tq = q_ref.shape[0]
    nc = kc_ref.shape[1]
    ns = ovt_ref.shape[0]
    rep = B_HEADS // B_KV
    t = i * tq + lax.broadcasted_iota(jnp.int32, (1, tq), 1)
    live = jnp.where(t >= CMP_LEN - 1, 1.0, 0.0)
    blk = lax.broadcasted_iota(jnp.int32, (ns, 1), 0)
    blk_f = blk.astype(F32)
    cur = jnp.right_shift(t, SEL_LEN.bit_length() - 1)
    removed = -3e38
    forced = (blk == 0) | (blk == cur) | (blk == cur - 1)
    keep = jnp.where(forced | (blk > cur), 0.0, 1.0)
    pinned = jnp.where(forced, removed, jnp.where(blk > cur, -FORCE, 0.0))

    def attend(nk):
        n = lax.broadcasted_iota(jnp.int32, (nk, 1), 0)
        hidden = jnp.where((n * CMP_STRIDE + (CMP_LEN - 1)) <= t, 0.0, NEG)
        ovt = ovt_ref[:, :nk]
        o_t = []
        for g in range(B_KV):
            kc = kc_ref[g, :nk, :]
            vct = vct_ref[g, :, :nk]
            psum = jnp.zeros((nk, tq), F32)
            for r in range(rep):
                h = g * rep + r
                qh = q_ref[:, h * HEAD_DIM:(h + 1) * HEAD_DIM]
                s = _dot_nt(kc, qh) + hidden
                p = jnp.exp2(s - _reduce_rows(s, jnp.maximum))
                p = p * (live / _reduce_rows(p, jnp.add))
                o_t.append(_dot(vct, p.astype(BF16)))
                psum = psum + p
            imp = sum(_dot(ovt, piece) for piece in _split_bf16(psum, 3))
            imp_scr[:, g * tq:(g + 1) * tq] = imp * keep + pinned
        o_ref[...] = jnp.transpose(jnp.concatenate(o_t, axis=0)).astype(BF16)

    n_chunks = nc // CMP_KEY_CHUNK
    last_chunk = ((i + 1) * (tq // CMP_STRIDE) - 1) // CMP_KEY_CHUNK
    for c in range(n_chunks):
        pl.when(last_chunk == c)(functools.partial(attend, (c + 1) * CMP_KEY_CHUNK))

    def take_next(imp, allow=None):
        mx = jnp.max(imp, axis=0, keepdims=True)
        first = jnp.min(jnp.where(imp == mx, blk_f, float(ns)), axis=0, keepdims=True)
        hit = blk_f == first
        if allow is not None:
            hit = hit & allow
        return jnp.where(hit, removed, imp)

    imp = imp_scr[...]
    n_forced_max = 3
    for _ in range(SEL_TOP - n_forced_max):
        imp = take_next(imp)
    imp_scr[...] = imp

    @pl.when(i * tq < 2 * SEL_LEN)
    def _():
        t2 = jnp.concatenate([t] * B_KV, axis=1)
        more = take_next(imp_scr[...], t2 < 2 * SEL_LEN)
        imp_scr[...] = take_next(more, t2 < SEL_LEN)

    for g in range(B_KV):
        picked = imp_scr[:, g * tq:(g + 1) * tq] < 0.5 * removed
        sel_ref[g] = jnp.where(picked, 0.0, NEG).astype(BF16)


def _cmp_attn(nat_b, kc, vct, overlap_t):
    B, S, _ = nat_b.shape
    nc = kc.shape[2]
    ns = overlap_t.shape[0]
    tq = CMP_TQ
    return pl.pallas_call(
        _cmp_attn_kernel, grid=(B, S // tq),
        in_specs=[
            pl.BlockSpec((None, tq, B_Q), lambda b, i: (b, i, 0)),
            pl.BlockSpec((None, B_KV, nc, HEAD_DIM), lambda b, i: (b, 0, 0, 0)),
            pl.BlockSpec((None, B_KV, HEAD_DIM, nc), lambda b, i: (b, 0, 0, 0)),
            pl.BlockSpec((ns, nc), lambda b, i: (0, 0)),
        ],
        out_specs=(
            pl.BlockSpec((None, tq, B_Q), lambda b, i: (b, i, 0)),
            pl.BlockSpec((None, B_KV, ns, tq), lambda b, i: (b, 0, 0, i)),
        ),
        out_shape=(
            jax.ShapeDtypeStruct((B, S, B_Q), BF16),
            jax.ShapeDtypeStruct((B, B_KV, ns, S), BF16),
        ),
        scratch_shapes=[pltpu.VMEM((ns, B_KV * tq), F32)],
        compiler_params=_cparams(2), name="cmp_attn",
    )(nat_b, kc, vct, overlap_t)


SLC_TK = 256
SLC_TQ = 2 * SLC_TK


SLC_VROWS = 80
SLC_CW = 256


def _slc_attn_kernel(q_ref, selt_ref, k_ref, vt_ref, o_ref, qat_scr, m_scr, acc_scr, s_a, s_b):
    i = pl.program_id(2)
    tq = q_ref.shape[0]
    rep = B_HEADS // B_KV
    selt = selt_ref[...]
    for r in range(rep):
        cols = slice(r * tq, (r + 1) * tq)
        q_r = q_ref[:, r * HEAD_DIM:(r + 1) * HEAD_DIM].astype(F32)
        qat_scr[:HEAD_DIM, cols] = jnp.transpose(q_r).astype(BF16)
        qat_scr[HEAD_DIM:, cols] = selt
    m_scr[...] = jnp.full(m_scr.shape, NEG, F32)
    acc_scr[...] = jnp.zeros(acc_scr.shape, F32)
    rel = (lax.broadcasted_iota(jnp.int32, (SLC_TK, tq), 1)
           - lax.broadcasted_iota(jnp.int32, (SLC_TK, tq), 0))

    def scores(j, s_buf):
        k0 = pl.multiple_of(j * SLC_TK, SLC_TK)
        s_buf[...] = _dot(k_ref[pl.ds(k0, SLC_TK), :], qat_scr[...])

    def update(j, s_buf, diag_offset=None):
        k0 = pl.multiple_of(j * SLC_TK, SLC_TK)
        vt = vt_ref[:, pl.ds(k0, SLC_TK)]
        for c in range(rep * tq // SLC_CW):
            cols = slice(c * SLC_CW, (c + 1) * SLC_CW)
            q_lo = (c * SLC_CW) % tq
            if diag_offset is not None and q_lo + SLC_CW <= diag_offset:
                continue
            s = s_buf[:, cols]
            if diag_offset is not None:
                s = jnp.where(rel[:, q_lo:q_lo + SLC_CW] >= diag_offset, s, NEG)
            m_prev = m_scr[:, cols]
            m_new = jnp.maximum(m_prev, jnp.max(s, axis=0, keepdims=True))
            alpha = jnp.exp2(m_prev - m_new)
            p = jnp.exp2(s - m_new).astype(BF16)
            acc_scr[:, cols] = alpha * acc_scr[:, cols] + _dot(vt, p)
            m_scr[:, cols] = m_new

    def tile_pair(jj, carry):
        a = 2 * jj
        scores(a + 1, s_b)
        update(a, s_a)
        scores(a + 2, s_a)
        update(a + 1, s_b)
        return carry

    def two_pairs(jj, carry):
        tile_pair(2 * jj, carry)
        return tile_pair(2 * jj + 1, carry)

    scores(0, s_a)
    lax.fori_loop(0, i // 2, two_pairs, 0)

    @pl.when(i % 2 == 1)
    def _():
        tile_pair(i - 1, 0)

    scores(2 * i + 1, s_b)
    update(2 * i, s_a, diag_offset=0)
    update(2 * i + 1, s_b, diag_offset=SLC_TK)

    for r in range(rep):
        acc = acc_scr[:, r * tq:(r + 1) * tq]
        o_t = acc[:HEAD_DIM] / acc[HEAD_DIM:HEAD_DIM + 1]
        o_ref[:, r * HEAD_DIM:(r + 1) * HEAD_DIM] = jnp.transpose(o_t).astype(BF16)


def _slc_attn(nat_b, selt, k_aug, vt_aug):
    B, S, _ = nat_b.shape
    G = B_KV
    rep = B_HEADS // B_KV
    ns = selt.shape[2]
    tq = SLC_TQ
    gw = rep * HEAD_DIM
    return pl.pallas_call(
        _slc_attn_kernel, grid=(B, G, S // tq),
        in_specs=[
            pl.BlockSpec((None, tq, gw), lambda b, g, i: (b, i, g)),
            pl.BlockSpec((None, None, ns, tq), lambda b, g, i: (b, g, 0, i)),
            pl.BlockSpec((None, None, S, HEAD_DIM + ns), lambda b, g, i: (b, g, 0, 0)),
            pl.BlockSpec((None, None, SLC_VROWS, S), lambda b, g, i: (b, g, 0, 0)),
        ],
        out_specs=pl.BlockSpec((None, tq, gw), lambda b, g, i: (b, i, g)),
        out_shape=jax.ShapeDtypeStruct((B, S, B_Q), BF16),
        scratch_shapes=[pltpu.VMEM((HEAD_DIM + ns, rep * tq), BF16),
                        pltpu.VMEM((1, rep * tq), F32), pltpu.VMEM((SLC_VROWS, rep * tq), F32),
                        pltpu.VMEM((SLC_TK, rep * tq), F32), pltpu.VMEM((SLC_TK, rep * tq), F32)],
        compiler_params=_cparams(3), name="slc_attn",
    )(nat_b, selt, k_aug, vt_aug)


SUB = 128
BAND_TQ = 1024


def _band_kernel(q_ref, kp_ref, kc_ref, vp_ref, vc_ref, o_ref, *lse_refs, shared_kv, prev, max_off):
    i = pl.program_id(1)
    tq = q_ref.shape[0]
    n_slab = q_ref.shape[1] // LANES
    width = prev + SUB
    r_i = lax.broadcasted_iota(jnp.int32, (SUB, width), 0)
    c_i = lax.broadcasted_iota(jnp.int32, (SUB, width), 1)
    diff = r_i - c_i + prev
    in_band = (diff >= 0) & (diff <= max_off)
    lane = lax.broadcasted_iota(jnp.int32, (1, LANES), 1)
    lo_half = lane < HEAD_DIM
    half_mask = {"lo": lo_half, "hi": jnp.logical_not(lo_half)}

    def window(p_ref, c_ref, sub, slab):
        cols = slice(slab * LANES, (slab + 1) * LANES)
        lo = sub * SUB
        parts = []
        if lo < prev:
            parts.append(p_ref[lo:prev, cols])
        parts.append(c_ref[max(0, lo - prev):lo + SUB, cols])
        return jnp.concatenate(parts, axis=0) if len(parts) > 1 else parts[0]

    def swap_halves(x):
        return jnp.concatenate([x[:, HEAD_DIM:], x[:, :HEAD_DIM]], axis=1)

    for sub in range(tq // SUB):
        rows = slice(sub * SUB, (sub + 1) * SUB)
        k_start = i * tq + sub * SUB - prev
        bias = jnp.where(in_band & (c_i + k_start >= 0), 0.0, NEG)
        if shared_kv:
            k2 = window(kp_ref, kc_ref, sub, 0)
            v2 = window(vp_ref, vc_ref, sub, 0)
            groups = [
                (((0, "lo"), (1, "lo"), (2, "hi"), (3, "hi")), k2, v2),
                (((0, "hi"), (1, "hi"), (2, "lo"), (3, "lo")), swap_halves(k2), swap_halves(v2)),
            ]
        else:
            groups = [(((p, "lo"), (p, "hi")), window(kp_ref, kc_ref, sub, p),
                       window(vp_ref, vc_ref, sub, p)) for p in range(n_slab)]
        outs = {}
        for members, k2, v2 in groups:
            nm = len(members)
            zero = jnp.zeros((), BF16)
            lhs = jnp.concatenate(
                [jnp.where(half_mask[half], q_ref[rows, slab * LANES:(slab + 1) * LANES], zero)
                 for slab, half in members], axis=0)
            s = _dot_nt(lhs, k2)
            s = (s.reshape(nm, SUB, width) + bias[None]).reshape(nm * SUB, width)
            m = jnp.max(s, axis=-1, keepdims=True)
            p = jnp.exp2(s - m)
            den = jnp.sum(p, axis=-1, keepdims=True)
            pv = _dot(p.astype(BF16), v2) / den
            lse = m + jnp.log2(den)
            for idx, member in enumerate(members):
                outs[member] = (pv[idx * SUB:(idx + 1) * SUB], lse[idx * SUB:(idx + 1) * SUB])
        lse_tile = jnp.zeros((SUB, LANES), F32)
        for slab in range(n_slab):
            cols = slice(slab * LANES, (slab + 1) * LANES)
            (o_lo, l_lo), (o_hi, l_hi) = outs[(slab, "lo")], outs[(slab, "hi")]
            o_ref[rows, cols] = jnp.where(lo_half, o_lo, o_hi).astype(o_ref.dtype)
            lse_tile = jnp.where(lane == 2 * slab, l_lo,
                                 jnp.where(lane == 2 * slab + 1, l_hi, lse_tile))
        if lse_refs:
            lse_refs[0][rows, :] = lse_tile


def _band_attn(q_arr, k_arr, v_arr, *, q_blk, k_blk, v_blk, hq, hk, max_off, tq, with_lse):
    N, L, _ = q_arr.shape
    assert hq == hk or (hq, hk) == (8, 2), (hq, hk)
    prev = -(-max_off // SUB) * SUB
    m = tq // prev
    qw, kw = hq * HEAD_DIM, hk * HEAD_DIM
    cur = lambda blk: (lambda n, i: (n, i, blk))
    prv = lambda blk: (lambda n, i: (n, jnp.maximum(i * m - 1, 0), blk))
    out_shape = [jax.ShapeDtypeStruct((N, L, qw), BF16)]
    out_specs = [pl.BlockSpec((None, tq, qw), cur(0))]
    if with_lse:
        out_shape.append(jax.ShapeDtypeStruct((N, L, LANES), F32))
        out_specs.append(pl.BlockSpec((None, tq, LANES), cur(0)))
    return pl.pallas_call(
        functools.partial(_band_kernel, shared_kv=hq != hk, prev=prev, max_off=max_off),
        grid=(N, L // tq),
        in_specs=[
            pl.BlockSpec((None, tq, qw), cur(q_blk)),
            pl.BlockSpec((None, prev, kw), prv(k_blk)),
            pl.BlockSpec((None, tq, kw), cur(k_blk)),
            pl.BlockSpec((None, prev, kw), prv(v_blk)),
            pl.BlockSpec((None, tq, kw), cur(v_blk)),
        ],
        out_specs=tuple(out_specs), out_shape=tuple(out_shape),
        compiler_params=_cparams(2), name="band_attn",
    )(q_arr, k_arr, k_arr, v_arr, v_arr)


FF_CHUNK = 512

def _merge_ffn_kernel(x_ref, o0_ref, l0_ref, o1_ref, l1_ref, o2_ref, l2_ref, oc_ref, os_ref, ow_ref,
                      gate_ref, wa_ref, wb_ref, wo_ref, gm_ref, wu_ref, wd_ref, gf_ref,
                      out_ref, s_o1, s_l1, s_o2, s_l2, x1_scr, *, final_norm):
    step = pl.program_id(0)
    T = x_ref.shape[0]

    @pl.when(step == 0)
    def _():
        x1_scr[1] = jnp.zeros(x1_scr.shape[1:], F32)

    x1_prev = x1_scr[(step + 1) % 2]
    u = _rms(x1_prev, gm_ref[...]).astype(BF16)
    acc = x1_prev
    for c in range(D_FF // FF_CHUNK):
        cols = slice(c * FF_CHUNK, (c + 1) * FF_CHUNK)
        h = jnp.maximum(_dot(u, wu_ref[:, cols]), 0.0)
        acc = acc + _dot((h * h).astype(BF16), wd_ref[cols, :])
    out_ref[...] = _rms(acc, gf_ref[...]) if final_norm else acc

    for d, o_ref, l_ref, s_o, s_l in ((4, o1_ref, l1_ref, s_o1, s_l1), (16, o2_ref, l2_ref, s_o2, s_l2)):
        rows = T // d
        for r in range(d):
            o_r = o_ref[r].astype(F32)
            s_l[pl.ds(r, rows, stride=d), :] = l_ref[r]
            for k in range(s_o.shape[0]):
                s_o[k, pl.ds(r, rows, stride=d), :] = o_r[:, k * LANES:(k + 1) * LANES]
    gather = lambda s: jnp.concatenate([s[k] for k in range(s.shape[0])], axis=1)

    l0, l1, l2 = l0_ref[...], s_l1[...], s_l2[...]
    mx = jnp.maximum(jnp.maximum(l0, l1), l2)
    e0, e1, e2 = jnp.exp2(l0 - mx), jnp.exp2(l1 - mx), jnp.exp2(l2 - mx)
    inv = 1.0 / (e0 + e1 + e2)
    lo_half = lax.broadcasted_iota(jnp.int32, (1, LANES), 1) < HEAD_DIM

    def spread(w, lane_of_head):
        col = lambda h: jnp.broadcast_to(w[:, lane_of_head(h):lane_of_head(h) + 1], (T, LANES))
        return jnp.concatenate([jnp.where(lo_half, col(2 * p), col(2 * p + 1))
                                for p in range(A_SLOTS // 2)], axis=1)

    slot = lambda h: h
    y_a = (spread(e0 * inv, slot) * o0_ref[...].astype(F32) + spread(e1 * inv, slot) * gather(s_o1)
           + spread(e2 * inv, slot) * gather(s_o2))

    gb = gate_ref[:, MERGE_GATE:MERGE_GATE + LANES].astype(F32)
    y_b = (spread(gb, lambda h: 3 * h) * oc_ref[...].astype(F32)
           + spread(gb, lambda h: 3 * h + 1) * os_ref[...].astype(F32)
           + spread(gb, lambda h: 3 * h + 2) * ow_ref[...].astype(F32))
    merged = (gate_ref[:, :D_MODEL].astype(F32) * _dot(y_a.astype(BF16), wa_ref[...])
              + gate_ref[:, D_MODEL:MERGE_GATE].astype(F32) * _dot(y_b.astype(BF16), wb_ref[...]))
    x1_scr[step % 2] = x_ref[...] + _dot(merged.astype(BF16), wo_ref[...])


def _merge_ffn(x, o0, l0, o1, l1, o2, l2, o_cmp, o_slc, o_win, gates, wa, wb, wo,
               g_mlp, w_up, w_down, g_final, final_norm):
    B, S, D = x.shape
    T = ROW_TILE
    nt = S // T
    n_tiles = B * nt
    qw = A_SLOTS * HEAD_DIM

    def tile_map(shift):
        def index_map(k):
            m = jnp.clip(k - shift, 0, n_tiles - 1)
            return (m // nt, m % nt, 0)
        return index_map

    nat = lambda w, shift=0: pl.BlockSpec((None, T, w), tile_map(shift))
    dil = lambda d, w: pl.BlockSpec((d, T // d, w), tile_map(0))
    full = lambda a: pl.BlockSpec(a.shape, lambda k: (0,) * a.ndim, pipeline_mode=pl.Buffered(1))
    o_scr, l_scr = pltpu.VMEM((qw // LANES, T, LANES), F32), pltpu.VMEM((T, LANES), F32)
    return pl.pallas_call(
        functools.partial(_merge_ffn_kernel, final_norm=final_norm), grid=(n_tiles + 1,),
        in_specs=[nat(D), nat(qw), nat(LANES), dil(4, qw), dil(4, LANES), dil(16, qw),
                  dil(16, LANES), nat(B_Q), nat(B_Q), nat(B_Q), nat(G_COLS), full(wa),
                  full(wb), full(wo), full(g_mlp), full(w_up), full(w_down), full(g_final)],
        out_specs=nat(D, shift=1), out_shape=jax.ShapeDtypeStruct((B, S, D), F32),
        scratch_shapes=[o_scr, l_scr, o_scr, l_scr, pltpu.VMEM((2, T, D), F32)],
        compiler_params=_cparams(1), name="merge_ffn",
    )(x, o0, l0, o1, l1, o2, l2, o_cmp, o_slc, o_win, gates, wa, wb, wo,
      g_mlp, w_up, w_down, g_final)


def _rope_table(seq):
    half = ROT_DIM // 2
    pos = jnp.arange(seq, dtype=F32)[:, None]
    inv = ROPE_THETA ** (-jnp.arange(0, ROT_DIM, 2, dtype=F32) / ROT_DIM)
    ang = pos * inv[None, :]
    cos, sin = jnp.cos(ang), jnp.sin(ang)
    pad = lambda parts, fill: jnp.tile(jnp.concatenate(
        parts + [jnp.full((seq, HEAD_DIM - ROT_DIM), fill, F32)], axis=1), (1, LANES // HEAD_DIM))
    zero = jnp.zeros_like(sin)
    return jnp.stack([pad([cos, cos], 1.0), pad([-sin, zero], 0.0), pad([zero, sin], 0.0)])


def _layer_weights(w_in):
    o3 = A_QKV + B_COLS
    o4 = o3 + B_GATE
    w_main = w_in.astype(BF16)
    w_gate = jnp.concatenate(
        [w_main[:, o4:], jnp.pad(w_main[:, o3:o4], ((0, 0), (0, G_PAD - B_GATE)))], axis=1)
    return w_main, w_gate


def _overlap_matrix(nc_pad, ns):
    c_start = jnp.arange(nc_pad) * CMP_STRIDE
    s_start = jnp.arange(ns) * SEL_LEN
    ov = (c_start[:, None] < s_start[None, :] + SEL_LEN) & (c_start[:, None] + CMP_LEN > s_start[None, :])
    return ov.astype(BF16)


def _layer(x, g_mix, w_in, cmp_pos_k, cmp_w1_k, cmp_w2_k, cmp_pos_v, cmp_w1_v, cmp_w2_v,
           w_branch_a, w_branch_b, w_out, g_mlp, w_up, w_down, g_final, final_norm, tab):
    B, S, D = x.shape
    G = B_KV
    a0, a1, a2, nat_b, cmp_in, gates = _in_proj(x, g_mix.reshape(1, D), *_layer_weights(w_in), tab)

    a_out = []
    for arr, (w, d) in zip((a0, a1, a2), DIL_PAIRS):
        a_out.append(_band_attn(arr, arr, arr, q_blk=0, k_blk=1, v_blk=2, hq=A_SLOTS, hk=A_SLOTS,
                                max_off=w // d, tq=min(BAND_TQ, S // d), with_lse=True))
    (o0, l0), (o1, l1), (o2, l2) = a_out

    nch = S // CMP_STRIDE
    kv_w = G * HEAD_DIM
    k_slc = nat_b[:, :, B_Q + 2 * kv_w:B_Q + 3 * kv_w].reshape(B, S, G, HEAD_DIM)
    v_slc = nat_b[:, :, B_Q + 3 * kv_w:B_Q + 4 * kv_w].reshape(B, S, G, HEAD_DIM)
    pos = jnp.stack([cmp_pos_k, cmp_pos_v]).astype(BF16)[:, :, None, :]
    pos = jnp.pad(pos, ((0, 0), (0, 0), (0, G - 1), (0, 0))).reshape(2, 1, CMP_LEN * G * HEAD_DIM)
    pos = jnp.broadcast_to(pos, (2, 8, CMP_LEN * G * HEAD_DIM))
    w1 = jnp.stack([cmp_w1_k, cmp_w1_v]).astype(BF16).reshape(2, CMP_LEN, 1, HEAD_DIM, CMP_HIDDEN)
    w1 = jnp.broadcast_to(w1, (2, CMP_LEN, G, HEAD_DIM, CMP_HIDDEN)).reshape(
        2, CMP_LEN * G * HEAD_DIM, CMP_HIDDEN)
    kcvc = _compress(cmp_in, pos, w1, jnp.stack([cmp_w2_k, cmp_w2_v]).astype(BF16))
    ns = S // SEL_LEN
    o_cmp, selb = _cmp_attn(nat_b, kcvc[0], kcvc[1].transpose(0, 1, 3, 2),
                            _overlap_matrix(nch, ns).T)

    onehot = ((jnp.arange(S) // SEL_LEN)[:, None] == jnp.arange(ns)[None, :]).astype(BF16)
    k_aug = jnp.concatenate([k_slc.transpose(0, 2, 1, 3),
                             jnp.broadcast_to(onehot, (B, G, S, ns))], axis=3)
    ones_row = (jnp.arange(SLC_VROWS - HEAD_DIM) == 0).astype(BF16)[:, None]
    vt_aug = jnp.concatenate(
        [v_slc.transpose(0, 2, 3, 1),
         jnp.broadcast_to(ones_row, (B, G, SLC_VROWS - HEAD_DIM, S))], axis=2)
    o_slc = _slc_attn(nat_b, selb, k_aug, vt_aug)

    kw_blk = (B_Q + 4 * G * HEAD_DIM) // (G * HEAD_DIM)
    (o_win,) = _band_attn(nat_b, nat_b, nat_b, q_blk=0, k_blk=kw_blk, v_blk=kw_blk + 1,
                          hq=B_HEADS, hk=B_KV, max_off=WIN_LEN - 1, tq=min(BAND_TQ, S),
                          with_lse=False)

    return _merge_ffn(x, o0, l0, o1, l1, o2, l2, o_cmp, o_slc, o_win, gates,
                      w_branch_a.astype(BF16), w_branch_b.astype(BF16), w_out.astype(BF16),
                      g_mlp.reshape(1, D), w_up.astype(BF16), w_down.astype(BF16),
                      g_final.reshape(1, D), final_norm)


def kernel(x, norm_mix_g, w_in, cmp_pos_k, cmp_w1_k, cmp_w2_k, cmp_pos_v, cmp_w1_v, cmp_w2_v,
           w_branch_a, w_branch_b, w_out, norm_mlp_g, w_up, w_down, norm_final_g):
    B, S, D = x.shape
    depth = w_in.shape[0]
    tab = _rope_table(S)
    for l in range(depth):
        x = _layer(x, norm_mix_g[l], w_in[l], cmp_pos_k[l], cmp_w1_k[l], cmp_w2_k[l],
                   cmp_pos_v[l], cmp_w1_v[l], cmp_w2_v[l], w_branch_a[l], w_branch_b[l],
                   w_out[l], norm_mlp_g[l], w_up[l], w_down[l], norm_final_g, l == depth - 1, tab)
    return x
```

```python
import functools

import jax
import jax.numpy as jnp
from jax import lax
from jax.experimental import pallas as pl
from jax.experimental.pallas import tpu as pltpu

F32 = jnp.float32
BF16 = jnp.bfloat16

D_MODEL = 1024
HEAD_DIM = 64
ROT_DIM = HEAD_DIM // 4
ROPE_THETA = 500000.0
EPS = 1e-6
DIL_PAIRS = ((128, 1), (512, 4), (2048, 16))
A_SLOTS = 8
A_HEADS = A_SLOTS * len(DIL_PAIRS)
B_HEADS = 8
B_KV = 2
CMP_LEN = 32
CMP_STRIDE = 16
CMP_HIDDEN = 4 * HEAD_DIM
SEL_LEN = 64
SEL_TOP = 16
WIN_LEN = 512
FORCE = 1e4
D_FF = 4 * D_MODEL
A_QKV = 3 * A_HEADS * HEAD_DIM
B_Q = B_HEADS * HEAD_DIM
B_KV_COLS = 6 * B_KV * HEAD_DIM
B_GATE = 3 * B_HEADS
MERGE_GATE = 2 * D_MODEL

LANES = 128
NEG = -1e30
QSCALE = HEAD_DIM ** -0.5 * 1.4426950408889634

A_COLS = 3 * A_SLOTS * HEAD_DIM
B_COLS = B_Q + B_KV_COLS
PROJ_CHUNK = 256
G_PAD = PROJ_CHUNK
G_COLS = MERGE_GATE + G_PAD

ROW_TILE = 512
VMEM_LIMIT = 56 * 1024 * 1024


def _cparams(n_axes, vmem=VMEM_LIMIT):
    return pltpu.CompilerParams(dimension_semantics=("arbitrary",) * n_axes, vmem_limit_bytes=vmem)


def _rms(x, g):
    return x * lax.rsqrt(jnp.mean(x * x, axis=-1, keepdims=True) + EPS) * g


def _dot(a, b):
    return jnp.dot(a, b, preferred_element_type=F32)


def _dot_nt(a, b):
    return lax.dot_general(a, b, (((1,), (1,)), ((), ())), preferred_element_type=F32)


def _reduce_rows(x, op):
    while x.shape[0] > 8 and x.shape[0] % 16 == 0:
        half = x.shape[0] // 2
        x = op(x[:half], x[half:])
    red = jnp.max if op is jnp.maximum else jnp.sum
    return red(x, axis=0, keepdims=True)


def _split_bf16(x, parts):
    out = []
    for _ in range(parts - 1):
        hi = x.astype(BF16)
        out.append(hi)
        x = x - hi.astype(F32)
    out.append(x.astype(BF16))
    return out


_B_ROT = (True,) * (B_Q // LANES) + (True, False, True, False, True, False)


def _rotary(x, cos_t, sin_a, sin_b):
    half = ROT_DIM // 2
    return x * cos_t + pltpu.roll(x, LANES - half, 1) * sin_a + pltpu.roll(x, half, 1) * sin_b


def _in_proj_kernel(x_ref, g_ref, w_ref, wg_ref, tab_ref,
                    a0_ref, a1_ref, a2_ref, b_ref, cmp_ref, gate_ref, u_scr):
    T = x_ref.shape[0]
    u = _rms(x_ref[...], g_ref[...])
    nslab = u_scr.shape[0]
    for k in range(nslab):
        u_scr[k] = u[:, k * LANES:(k + 1) * LANES]
    u_nat = u.astype(BF16)

    def dilated(load, d):
        rows = T // d
        return jnp.concatenate([load(pl.ds(r, rows, stride=d)) for r in range(d)], axis=0)

    def u_rows(rows):
        return jnp.concatenate([u_scr[k, rows, :] for k in range(nslab)], axis=1).astype(BF16)

    q_slabs = A_SLOTS * HEAD_DIM // LANES
    per = PROJ_CHUNK // LANES

    def project(lhs, weights, chunk_cols, rot, tables, store, n_q=0):
        for c, w_col in enumerate(chunk_cols):
            res = _dot(lhs, weights[:, w_col:w_col + PROJ_CHUNK])
            for k in range(per):
                slab = res[:, k * LANES:(k + 1) * LANES]
                if rot[c * per + k]:
                    slab = _rotary(slab, tables[0], tables[1], tables[2])
                if c * per + k < n_q:
                    slab = slab * QSCALE
                store((c * per + k) * LANES, slab)

    a_rot = (True,) * (2 * q_slabs) + (False,) * q_slabs
    group_w = A_SLOTS * HEAD_DIM

    def group_cols(g):
        return [t * A_HEADS * HEAD_DIM + g * group_w + c
                for t in range(3) for c in range(0, group_w, PROJ_CHUNK)]

    tab0 = tuple(tab_ref[k] for k in range(3))

    def store_a0(c0, res):
        a0_ref[:, c0:c0 + LANES] = res.astype(BF16)

    project(u_nat, w_ref, group_cols(0), a_rot, tab0, store_a0, n_q=q_slabs)

    for g, (d, a_ref) in enumerate(((4, a1_ref), (16, a2_ref)), start=1):
        rows = T // d
        tabs = tuple(dilated(lambda rs, k=k: tab_ref[k, rs, :], d) for k in range(3))

        def store_ad(c0, res, a_ref=a_ref, d=d, rows=rows):
            for r in range(d):
                a_ref[r, :, c0:c0 + LANES] = res[r * rows:(r + 1) * rows].astype(BF16)

        project(dilated(u_rows, d), w_ref, group_cols(g), a_rot, tabs, store_ad, n_q=q_slabs)

    def store_b(c0, res):
        b_ref[:, c0:c0 + LANES] = res.astype(BF16)
        if B_Q <= c0 < B_Q + 2 * LANES:
            cmp_ref[:, c0 - B_Q:c0 - B_Q + LANES] = res

    project(u_nat, w_ref, range(A_QKV, A_QKV + B_COLS, PROJ_CHUNK), _B_ROT, tab0, store_b,
            n_q=q_slabs)

    def store_g(c0, res):
        gate_ref[:, c0:c0 + LANES] = jax.nn.sigmoid(res).astype(gate_ref.dtype)

    project(u_nat, wg_ref, range(0, G_COLS, PROJ_CHUNK), (False,) * (G_COLS // LANES), tab0, store_g)


def _in_proj(x, g, w_main, w_gate, tab):
    B, S, D = x.shape
    T = ROW_TILE
    nt = S // T
    out_shape = (
        jax.ShapeDtypeStruct((B, S, A_COLS), BF16),
        jax.ShapeDtypeStruct((B * 4, S // 4, A_COLS), BF16),
        jax.ShapeDtypeStruct((B * 16, S // 16, A_COLS), BF16),
        jax.ShapeDtypeStruct((B, S, B_COLS), BF16),
        jax.ShapeDtypeStruct((B, S, 2 * LANES), F32),
        jax.ShapeDtypeStruct((B, S, G_COLS), BF16),
    )
    in_specs = [
        pl.BlockSpec((None, T, D), lambda b, i: (b, i, 0)),
        pl.BlockSpec((1, D), lambda b, i: (0, 0)),
        pl.BlockSpec(w_main.shape, lambda b, i: (0, 0), pipeline_mode=pl.Buffered(1)),
        pl.BlockSpec(w_gate.shape, lambda b, i: (0, 0), pipeline_mode=pl.Buffered(1)),
        pl.BlockSpec((3, T, LANES), lambda b, i: (0, i, 0)),
    ]
    out_specs = (
        pl.BlockSpec((None, T, A_COLS), lambda b, i: (b, i, 0)),
        pl.BlockSpec((4, T // 4, A_COLS), lambda b, i: (b, i, 0)),
        pl.BlockSpec((16, T // 16, A_COLS), lambda b, i: (b, i, 0)),
        pl.BlockSpec((None, T, B_COLS), lambda b, i: (b, i, 0)),
        pl.BlockSpec((None, T, 2 * LANES), lambda b, i: (b, i, 0)),
        pl.BlockSpec((None, T, G_COLS), lambda b, i: (b, i, 0)),
    )
    return pl.pallas_call(
        _in_proj_kernel, grid=(B, nt), in_specs=in_specs, out_specs=out_specs, out_shape=out_shape,
        scratch_shapes=[pltpu.VMEM((D // LANES, T, LANES), F32)], compiler_params=_cparams(2),
        name="in_proj",
    )(x, g, w_main, w_gate, tab)


def _gelu_tanh(x):
    return 0.5 * x * (1.0 + jnp.tanh(0.7978845608028654 * (x + 0.044715 * (x * x * x))))


def _compress_kernel(x_ref, pos_ref, w1_ref, w2_ref, o_ref):
    nch = x_ref.shape[0] // CMP_STRIDE
    half = CMP_STRIDE * LANES
    chunks = jnp.concatenate(
        [x_ref[pl.ds(j, nch, stride=CMP_STRIDE), :] for j in range(CMP_STRIDE)], axis=1).astype(BF16)
    lane_group = (lax.broadcasted_iota(jnp.int32, (1, half), 1) % LANES) // HEAD_DIM
    pb = _dot(pos_ref[...], w1_ref[...])[0:1]
    for g in range(B_KV):
        c = jnp.where(lane_group == g, chunks, jnp.zeros((), BF16))
        p1 = _dot(c, w1_ref[:half, :])
        p2 = _dot(c, w1_ref[half:, :])
        p2_next = pltpu.roll(p2, nch - 1, 0)
        h = _gelu_tanh(p1 + p2_next + pb)
        o_ref[g] = _dot(h.astype(BF16), w2_ref[...]).astype(BF16)


def _compress(cmp_in, pos, w1, w2):
    B, S, _ = cmp_in.shape
    nch = S // CMP_STRIDE
    kx = 2 * CMP_STRIDE * LANES
    return pl.pallas_call(
        _compress_kernel, grid=(2, B),
        in_specs=[
            pl.BlockSpec((None, S, LANES), lambda t, b: (b, 0, t)),
            pl.BlockSpec((None, 8, kx), lambda t, b: (t, 0, 0)),
            pl.BlockSpec((None, kx, CMP_HIDDEN), lambda t, b: (t, 0, 0)),
            pl.BlockSpec((None, CMP_HIDDEN, HEAD_DIM), lambda t, b: (t, 0, 0)),
        ],
        out_specs=pl.BlockSpec((None, None, B_KV, nch, HEAD_DIM), lambda t, b: (t, b, 0, 0, 0)),
        out_shape=jax.ShapeDtypeStruct((2, B, B_KV, nch, HEAD_DIM), BF16),
        compiler_params=_cparams(2), name="compress",
    )(cmp_in, pos, w1, w2)


CMP_TQ = 1024


CMP_KEY_CHUNK = 128


def _cmp_attn_kernel(q_ref, kc_ref, vct_ref, ovt_ref, o_ref, sel_ref, imp_scr):
    i = pl.program_id(1)
    tq = q_ref.shape[0]
    nc = kc_ref.shape[1]
    ns = ovt_ref.shape[0]
    rep = B_HEADS // B_KV
    t = i * tq + lax.broadcasted_iota(jnp.int32, (1, tq), 1)
    live = jnp.where(t >= CMP_LEN - 1, 1.0, 0.0)
    blk = lax.broadcasted_iota(jnp.int32, (ns, 1), 0)
    blk_f = blk.astype(F32)
    cur = jnp.right_shift(t, SEL_LEN.bit_length() - 1)
    removed = -3e38
    forced = (blk == 0) | (blk == cur) | (blk == cur - 1)
    keep = jnp.where(forced | (blk > cur), 0.0, 1.0)
    pinned = jnp.where(forced, removed, jnp.where(blk > cur, -FORCE, 0.0))

    def attend(nk):
        n = lax.broadcasted_iota(jnp.int32, (nk, 1), 0)
        hidden = jnp.where((n * CMP_STRIDE + (CMP_LEN - 1)) <= t, 0.0, NEG)
        ovt = ovt_ref[:, :nk]
        o_t = []
        for g in range(B_KV):
            kc = kc_ref[g, :nk, :]
            vct = vct_ref[g, :, :nk]
            psum = jnp.zeros((nk, tq), F32)
            for r in range(rep):
                h = g * rep + r
                qh = q_ref[:, h * HEAD_DIM:(h + 1) * HEAD_DIM]
                s = _dot_nt(kc, qh) + hidden
                p = jnp.exp2(s - _reduce_rows(s, jnp.maximum))
                p = p * (live / _reduce_rows(p, jnp.add))
                o_t.append(_dot(vct, p.astype(BF16)))
                psum = psum + p
            imp = sum(_dot(ovt, piece) for piece in _split_bf16(psum, 3))
            imp_scr[:, g * tq:(g + 1) * tq] = imp * keep + pinned
        o_ref[...] = jnp.transpose(jnp.concatenate(o_t, axis=0)).astype(BF16)

    n_chunks = nc // CMP_KEY_CHUNK
    last_chunk = ((i + 1) * (tq // CMP_STRIDE) - 1) // CMP_KEY_CHUNK
    for c in range(n_chunks):
        pl.when(last_chunk == c)(functools.partial(attend, (c + 1) * CMP_KEY_CHUNK))

    def take_next(imp, allow=None):
        rank = blk_f[:imp.shape[0]]
        mx = jnp.max(imp, axis=0, keepdims=True)
        first = jnp.min(jnp.where(imp == mx, rank, float(ns)), axis=0, keepdims=True)
        hit = rank == first
        if allow is not None:
            hit = hit & allow
        return jnp.where(hit, removed, imp)

    def rank_blocks(nb):
        imp = imp_scr[:nb, :]
        for _ in range(SEL_TOP - 3):
            imp = take_next(imp)
        imp_scr[:nb, :] = imp

    early = (i + 1) * tq <= (ns // 2) * SEL_LEN
    pl.when(early)(functools.partial(rank_blocks, ns // 2))
    pl.when(jnp.logical_not(early))(functools.partial(rank_blocks, ns))

    @pl.when(i * tq < 2 * SEL_LEN)
    def _():
        t2 = jnp.concatenate([t] * B_KV, axis=1)
        more = take_next(imp_scr[...], t2 < 2 * SEL_LEN)
        imp_scr[...] = take_next(more, t2 < SEL_LEN)

    for g in range(B_KV):
        picked = imp_scr[:, g * tq:(g + 1) * tq] < 0.5 * removed
        sel_ref[g] = jnp.where(picked, 0.0, NEG).astype(BF16)


def _cmp_attn(nat_b, kc, vct, overlap_t):
    B, S, _ = nat_b.shape
    nc = kc.shape[2]
    ns = overlap_t.shape[0]
    tq = CMP_TQ
    return pl.pallas_call(
        _cmp_attn_kernel, grid=(B, S // tq),
        in_specs=[
            pl.BlockSpec((None, tq, B_Q), lambda b, i: (b, i, 0)),
            pl.BlockSpec((None, B_KV, nc, HEAD_DIM), lambda b, i: (b, 0, 0, 0)),
            pl.BlockSpec((None, B_KV, HEAD_DIM, nc), lambda b, i: (b, 0, 0, 0)),
            pl.BlockSpec((ns, nc), lambda b, i: (0, 0)),
        ],
        out_specs=(
            pl.BlockSpec((None, tq, B_Q), lambda b, i: (b, i, 0)),
            pl.BlockSpec((None, B_KV, ns, tq), lambda b, i: (b, 0, 0, i)),
        ),
        out_shape=(
            jax.ShapeDtypeStruct((B, S, B_Q), BF16),
            jax.ShapeDtypeStruct((B, B_KV, ns, S), BF16),
        ),
        scratch_shapes=[pltpu.VMEM((ns, B_KV * tq), F32)],
        compiler_params=_cparams(2), name="cmp_attn",
    )(nat_b, kc, vct, overlap_t)


SLC_TK = 256
SLC_TQ = 2 * SLC_TK


SLC_VROWS = 80
SLC_CW = 256


def _slc_attn_kernel(q_ref, selt_ref, kin_ref, hot_ref, vin_ref, o_ref,
                     k_ref, vt_ref, qat_scr, m_scr, acc_scr, s_a, s_b):
    i = pl.program_id(2)
    tq = q_ref.shape[0]
    rep = B_HEADS // B_KV

    @pl.when(i == 0)
    def _():
        k_ref[:, :HEAD_DIM] = kin_ref[...]
        k_ref[:, HEAD_DIM:] = hot_ref[...]
        vt_ref[:HEAD_DIM, :] = vin_ref[...]
        pad_rows = vt_ref.shape[0] - HEAD_DIM
        row = lax.broadcasted_iota(jnp.int32, (pad_rows, vt_ref.shape[1]), 0)
        vt_ref[HEAD_DIM:, :] = jnp.where(row == 0, 1.0, 0.0).astype(BF16)

    selt = selt_ref[...]
    for r in range(rep):
        cols = slice(r * tq, (r + 1) * tq)
        q_r = q_ref[:, r * HEAD_DIM:(r + 1) * HEAD_DIM].astype(F32)
        qat_scr[:HEAD_DIM, cols] = jnp.transpose(q_r).astype(BF16)
        qat_scr[HEAD_DIM:, cols] = selt
    m_scr[...] = jnp.full(m_scr.shape, NEG, F32)
    acc_scr[...] = jnp.zeros(acc_scr.shape, F32)
    rel = (lax.broadcasted_iota(jnp.int32, (SLC_TK, tq), 1)
           - lax.broadcasted_iota(jnp.int32, (SLC_TK, tq), 0))

    def scores(j, s_buf):
        k0 = pl.multiple_of(j * SLC_TK, SLC_TK)
        s_buf[...] = _dot(k_ref[pl.ds(k0, SLC_TK), :], qat_scr[...])

    def update(j, s_buf, diag_offset=None):
        k0 = pl.multiple_of(j * SLC_TK, SLC_TK)
        vt = vt_ref[:, pl.ds(k0, SLC_TK)]
        for c in range(rep * tq // SLC_CW):
            cols = slice(c * SLC_CW, (c + 1) * SLC_CW)
            q_lo = (c * SLC_CW) % tq
            if diag_offset is not None and q_lo + SLC_CW <= diag_offset:
                continue
            s = s_buf[:, cols]
            if diag_offset is not None:
                s = jnp.where(rel[:, q_lo:q_lo + SLC_CW] >= diag_offset, s, NEG)
            m_prev = m_scr[:, cols]
            m_new = jnp.maximum(m_prev, jnp.max(s, axis=0, keepdims=True))
            alpha = jnp.exp2(m_prev - m_new)
            p = jnp.exp2(s - m_new).astype(BF16)
            acc_scr[:, cols] = alpha * acc_scr[:, cols] + _dot(vt, p)
            m_scr[:, cols] = m_new

    def tile_pair(jj, carry):
        a = 2 * jj
        scores(a + 1, s_b)
        update(a, s_a)
        scores(a + 2, s_a)
        update(a + 1, s_b)
        return carry

    def two_pairs(jj, carry):
        tile_pair(2 * jj, carry)
        return tile_pair(2 * jj + 1, carry)

    scores(0, s_a)
    lax.fori_loop(0, i // 2, two_pairs, 0)

    @pl.when(i % 2 == 1)
    def _():
        tile_pair(i - 1, 0)

    scores(2 * i + 1, s_b)
    update(2 * i, s_a, diag_offset=0)
    update(2 * i + 1, s_b, diag_offset=SLC_TK)

    for r in range(rep):
        acc = acc_scr[:, r * tq:(r + 1) * tq]
        o_t = acc[:HEAD_DIM] / acc[HEAD_DIM:HEAD_DIM + 1]
        o_ref[:, r * HEAD_DIM:(r + 1) * HEAD_DIM] = jnp.transpose(o_t).astype(BF16)


def _slc_attn(nat_b, selt, k_t, onehot, v_t):
    B, S, _ = nat_b.shape
    G = B_KV
    rep = B_HEADS // B_KV
    ns = selt.shape[2]
    tq = SLC_TQ
    gw = rep * HEAD_DIM
    return pl.pallas_call(
        _slc_attn_kernel, grid=(B, G, S // tq),
        in_specs=[
            pl.BlockSpec((None, tq, gw), lambda b, g, i: (b, i, g)),
            pl.BlockSpec((None, None, ns, tq), lambda b, g, i: (b, g, 0, i)),
            pl.BlockSpec((None, None, S, HEAD_DIM), lambda b, g, i: (b, g, 0, 0)),
            pl.BlockSpec((S, ns), lambda b, g, i: (0, 0), pipeline_mode=pl.Buffered(1)),
            pl.BlockSpec((None, None, HEAD_DIM, S), lambda b, g, i: (b, g, 0, 0)),
        ],
        out_specs=pl.BlockSpec((None, tq, gw), lambda b, g, i: (b, i, g)),
        out_shape=jax.ShapeDtypeStruct((B, S, B_Q), BF16),
        scratch_shapes=[pltpu.VMEM((S, HEAD_DIM + ns), BF16), pltpu.VMEM((SLC_VROWS, S), BF16),
                        pltpu.VMEM((HEAD_DIM + ns, rep * tq), BF16),
                        pltpu.VMEM((1, rep * tq), F32), pltpu.VMEM((SLC_VROWS, rep * tq), F32),
                        pltpu.VMEM((SLC_TK, rep * tq), F32), pltpu.VMEM((SLC_TK, rep * tq), F32)],
        compiler_params=_cparams(3), name="slc_attn",
    )(nat_b, selt, k_t, onehot, v_t)


SUB = 128
BAND_TQ = 1024


def _band_kernel(q_ref, kp_ref, kc_ref, vp_ref, vc_ref, o_ref, *lse_refs, shared_kv, prev, max_off):
    i = pl.program_id(1)
    tq = q_ref.shape[0]
    n_slab = q_ref.shape[1] // LANES
    width = prev + SUB
    r_i = lax.broadcasted_iota(jnp.int32, (SUB, width), 0)
    c_i = lax.broadcasted_iota(jnp.int32, (SUB, width), 1)
    diff = r_i - c_i + prev
    in_band = (diff >= 0) & (diff <= max_off)
    lane = lax.broadcasted_iota(jnp.int32, (1, LANES), 1)
    lo_half = lane < HEAD_DIM
    half_mask = {"lo": lo_half, "hi": jnp.logical_not(lo_half)}

    def window(p_ref, c_ref, sub, slab):
        cols = slice(slab * LANES, (slab + 1) * LANES)
        lo = sub * SUB
        parts = []
        if lo < prev:
            parts.append(p_ref[lo:prev, cols])
        parts.append(c_ref[max(0, lo - prev):lo + SUB, cols])
        return jnp.concatenate(parts, axis=0) if len(parts) > 1 else parts[0]

    def swap_halves(x):
        return jnp.concatenate([x[:, HEAD_DIM:], x[:, :HEAD_DIM]], axis=1)

    for sub in range(tq // SUB):
        rows = slice(sub * SUB, (sub + 1) * SUB)
        k_start = i * tq + sub * SUB - prev
        bias = jnp.where(in_band & (c_i + k_start >= 0), 0.0, NEG)
        if shared_kv:
            k2 = window(kp_ref, kc_ref, sub, 0)
            v2 = window(vp_ref, vc_ref, sub, 0)
            groups = [
                (((0, "lo"), (1, "lo"), (2, "hi"), (3, "hi")), k2, v2),
                (((0, "hi"), (1, "hi"), (2, "lo"), (3, "lo")), swap_halves(k2), swap_halves(v2)),
            ]
        else:
            groups = [(((p, "lo"), (p, "hi")), window(kp_ref, kc_ref, sub, p),
                       window(vp_ref, vc_ref, sub, p)) for p in range(n_slab)]
        outs = {}
        for members, k2, v2 in groups:
            nm = len(members)
            zero = jnp.zeros((), BF16)
            lhs = jnp.concatenate(
                [jnp.where(half_mask[half], q_ref[rows, slab * LANES:(slab + 1) * LANES], zero)
                 for slab, half in members], axis=0)
            s = _dot_nt(lhs, k2)
            s = (s.reshape(nm, SUB, width) + bias[None]).reshape(nm * SUB, width)
            m = jnp.max(s, axis=-1, keepdims=True)
            p = jnp.exp2(s - m)
            den = jnp.sum(p, axis=-1, keepdims=True)
            pv = _dot(p.astype(BF16), v2) / den
            lse = m + jnp.log2(den)
            for idx, member in enumerate(members):
                outs[member] = (pv[idx * SUB:(idx + 1) * SUB], lse[idx * SUB:(idx + 1) * SUB])
        lse_tile = jnp.zeros((SUB, LANES), F32)
        for slab in range(n_slab):
            cols = slice(slab * LANES, (slab + 1) * LANES)
            (o_lo, l_lo), (o_hi, l_hi) = outs[(slab, "lo")], outs[(slab, "hi")]
            o_ref[rows, cols] = jnp.where(lo_half, o_lo, o_hi).astype(o_ref.dtype)
            lse_tile = jnp.where(lane == 2 * slab, l_lo,
                                 jnp.where(lane == 2 * slab + 1, l_hi, lse_tile))
        if lse_refs:
            lse_refs[0][rows, :] = lse_tile


def _band_attn(q_arr, k_arr, v_arr, *, q_blk, k_blk, v_blk, hq, hk, max_off, tq, with_lse):
    N, L, _ = q_arr.shape
    assert hq == hk or (hq, hk) == (8, 2), (hq, hk)
    prev = -(-max_off // SUB) * SUB
    m = tq // prev
    qw, kw = hq * HEAD_DIM, hk * HEAD_DIM
    cur = lambda blk: (lambda n, i: (n, i, blk))
    prv = lambda blk: (lambda n, i: (n, jnp.maximum(i * m - 1, 0), blk))
    out_shape = [jax.ShapeDtypeStruct((N, L, qw), BF16)]
    out_specs = [pl.BlockSpec((None, tq, qw), cur(0))]
    if with_lse:
        out_shape.append(jax.ShapeDtypeStruct((N, L, LANES), F32))
        out_specs.append(pl.BlockSpec((None, tq, LANES), cur(0)))
    return pl.pallas_call(
        functools.partial(_band_kernel, shared_kv=hq != hk, prev=prev, max_off=max_off),
        grid=(N, L // tq),
        in_specs=[
            pl.BlockSpec((None, tq, qw), cur(q_blk)),
            pl.BlockSpec((None, prev, kw), prv(k_blk)),
            pl.BlockSpec((None, tq, kw), cur(k_blk)),
            pl.BlockSpec((None, prev, kw), prv(v_blk)),
            pl.BlockSpec((None, tq, kw), cur(v_blk)),
        ],
        out_specs=tuple(out_specs), out_shape=tuple(out_shape),
        compiler_params=_cparams(2), name="band_attn",
    )(q_arr, k_arr, k_arr, v_arr, v_arr)


FF_CHUNK = 512

def _merge_ffn_kernel(x_ref, o0_ref, l0_ref, o1_ref, l1_ref, o2_ref, l2_ref, oc_ref, os_ref, ow_ref,
                      gate_ref, wa_ref, wb_ref, wo_ref, gm_ref, wu_ref, wd_ref, gf_ref,
                      out_ref, s_o1, s_l1, s_o2, s_l2, x1_scr, *, final_norm):
    step = pl.program_id(0)
    T = x_ref.shape[0]

    @pl.when(step == 0)
    def _():
        x1_scr[1] = jnp.zeros(x1_scr.shape[1:], F32)

    x1_prev = x1_scr[(step + 1) % 2]
    u = _rms(x1_prev, gm_ref[...]).astype(BF16)
    acc = x1_prev
    for c in range(D_FF // FF_CHUNK):
        cols = slice(c * FF_CHUNK, (c + 1) * FF_CHUNK)
        h = jnp.maximum(_dot(u, wu_ref[:, cols]), 0.0)
        acc = acc + _dot((h * h).astype(BF16), wd_ref[cols, :])
    out_ref[...] = _rms(acc, gf_ref[...]) if final_norm else acc

    for d, o_ref, l_ref, s_o, s_l in ((4, o1_ref, l1_ref, s_o1, s_l1), (16, o2_ref, l2_ref, s_o2, s_l2)):
        rows = T // d
        for r in range(d):
            o_r = o_ref[r].astype(F32)
            s_l[pl.ds(r, rows, stride=d), :] = l_ref[r]
            for k in range(s_o.shape[0]):
                s_o[k, pl.ds(r, rows, stride=d), :] = o_r[:, k * LANES:(k + 1) * LANES]
    gather = lambda s: jnp.concatenate([s[k] for k in range(s.shape[0])], axis=1)

    l0, l1, l2 = l0_ref[...], s_l1[...], s_l2[...]
    mx = jnp.maximum(jnp.maximum(l0, l1), l2)
    e0, e1, e2 = jnp.exp2(l0 - mx), jnp.exp2(l1 - mx), jnp.exp2(l2 - mx)
    inv = 1.0 / (e0 + e1 + e2)
    lo_half = lax.broadcasted_iota(jnp.int32, (1, LANES), 1) < HEAD_DIM

    def spread(w, lane_of_head):
        col = lambda h: jnp.broadcast_to(w[:, lane_of_head(h):lane_of_head(h) + 1], (T, LANES))
        return jnp.concatenate([jnp.where(lo_half, col(2 * p), col(2 * p + 1))
                                for p in range(A_SLOTS // 2)], axis=1)

    slot = lambda h: h
    y_a = (spread(e0 * inv, slot) * o0_ref[...].astype(F32) + spread(e1 * inv, slot) * gather(s_o1)
           + spread(e2 * inv, slot) * gather(s_o2))

    gb = gate_ref[:, MERGE_GATE:MERGE_GATE + LANES].astype(F32)
    y_b = (spread(gb, lambda h: 3 * h) * oc_ref[...].astype(F32)
           + spread(gb, lambda h: 3 * h + 1) * os_ref[...].astype(F32)
           + spread(gb, lambda h: 3 * h + 2) * ow_ref[...].astype(F32))
    merged = (gate_ref[:, :D_MODEL].astype(F32) * _dot(y_a.astype(BF16), wa_ref[...])
              + gate_ref[:, D_MODEL:MERGE_GATE].astype(F32) * _dot(y_b.astype(BF16), wb_ref[...]))
    x1_scr[step % 2] = x_ref[...] + _dot(merged.astype(BF16), wo_ref[...])


def _merge_ffn(x, o0, l0, o1, l1, o2, l2, o_cmp, o_slc, o_win, gates, wa, wb, wo,
               g_mlp, w_up, w_down, g_final, final_norm):
    B, S, D = x.shape
    T = ROW_TILE
    nt = S // T
    n_tiles = B * nt
    qw = A_SLOTS * HEAD_DIM

    def tile_map(shift):
        def index_map(k):
            m = jnp.clip(k - shift, 0, n_tiles - 1)
            return (m // nt, m % nt, 0)
        return index_map

    nat = lambda w, shift=0: pl.BlockSpec((None, T, w), tile_map(shift))
    dil = lambda d, w: pl.BlockSpec((d, T // d, w), tile_map(0))
    full = lambda a: pl.BlockSpec(a.shape, lambda k: (0,) * a.ndim, pipeline_mode=pl.Buffered(1))
    o_scr, l_scr = pltpu.VMEM((qw // LANES, T, LANES), F32), pltpu.VMEM((T, LANES), F32)
    return pl.pallas_call(
        functools.partial(_merge_ffn_kernel, final_norm=final_norm), grid=(n_tiles + 1,),
        in_specs=[nat(D), nat(qw), nat(LANES), dil(4, qw), dil(4, LANES), dil(16, qw),
                  dil(16, LANES), nat(B_Q), nat(B_Q), nat(B_Q), nat(G_COLS), full(wa),
                  full(wb), full(wo), full(g_mlp), full(w_up), full(w_down), full(g_final)],
        out_specs=nat(D, shift=1), out_shape=jax.ShapeDtypeStruct((B, S, D), F32),
        scratch_shapes=[o_scr, l_scr, o_scr, l_scr, pltpu.VMEM((2, T, D), F32)],
        compiler_params=_cparams(1), name="merge_ffn",
    )(x, o0, l0, o1, l1, o2, l2, o_cmp, o_slc, o_win, gates, wa, wb, wo,
      g_mlp, w_up, w_down, g_final)


def _rope_table(seq):
    half = ROT_DIM // 2
    pos = jnp.arange(seq, dtype=F32)[:, None]
    inv = ROPE_THETA ** (-jnp.arange(0, ROT_DIM, 2, dtype=F32) / ROT_DIM)
    ang = pos * inv[None, :]
    cos, sin = jnp.cos(ang), jnp.sin(ang)
    pad = lambda parts, fill: jnp.tile(jnp.concatenate(
        parts + [jnp.full((seq, HEAD_DIM - ROT_DIM), fill, F32)], axis=1), (1, LANES // HEAD_DIM))
    zero = jnp.zeros_like(sin)
    return jnp.stack([pad([cos, cos], 1.0), pad([-sin, zero], 0.0), pad([zero, sin], 0.0)])


def _layer_weights(w_in):
    o3 = A_QKV + B_COLS
    o4 = o3 + B_GATE
    w_main = w_in.astype(BF16)
    w_gate = jnp.concatenate(
        [w_main[:, o4:], jnp.pad(w_main[:, o3:o4], ((0, 0), (0, G_PAD - B_GATE)))], axis=1)
    return w_main, w_gate


def _overlap_matrix(nc_pad, ns):
    c_start = jnp.arange(nc_pad) * CMP_STRIDE
    s_start = jnp.arange(ns) * SEL_LEN
    ov = (c_start[:, None] < s_start[None, :] + SEL_LEN) & (c_start[:, None] + CMP_LEN > s_start[None, :])
    return ov.astype(BF16)


def _layer(x, g_mix, w_in, cmp_pos_k, cmp_w1_k, cmp_w2_k, cmp_pos_v, cmp_w1_v, cmp_w2_v,
           w_branch_a, w_branch_b, w_out, g_mlp, w_up, w_down, g_final, final_norm, tab):
    B, S, D = x.shape
    G = B_KV
    a0, a1, a2, nat_b, cmp_in, gates = _in_proj(x, g_mix.reshape(1, D), *_layer_weights(w_in), tab)

    a_out = []
    for arr, (w, d) in zip((a0, a1, a2), DIL_PAIRS):
        a_out.append(_band_attn(arr, arr, arr, q_blk=0, k_blk=1, v_blk=2, hq=A_SLOTS, hk=A_SLOTS,
                                max_off=w // d, tq=min(BAND_TQ, S // d), with_lse=True))
    (o0, l0), (o1, l1), (o2, l2) = a_out

    nch = S // CMP_STRIDE
    kv_w = G * HEAD_DIM
    k_slc = nat_b[:, :, B_Q + 2 * kv_w:B_Q + 3 * kv_w].reshape(B, S, G, HEAD_DIM)
    v_slc = nat_b[:, :, B_Q + 3 * kv_w:B_Q + 4 * kv_w].reshape(B, S, G, HEAD_DIM)
    pos = jnp.stack([cmp_pos_k, cmp_pos_v]).astype(BF16)[:, :, None, :]
    pos = jnp.pad(pos, ((0, 0), (0, 0), (0, G - 1), (0, 0))).reshape(2, 1, CMP_LEN * G * HEAD_DIM)
    pos = jnp.broadcast_to(pos, (2, 8, CMP_LEN * G * HEAD_DIM))
    w1 = jnp.stack([cmp_w1_k, cmp_w1_v]).astype(BF16).reshape(2, CMP_LEN, 1, HEAD_DIM, CMP_HIDDEN)
    w1 = jnp.broadcast_to(w1, (2, CMP_LEN, G, HEAD_DIM, CMP_HIDDEN)).reshape(
        2, CMP_LEN * G * HEAD_DIM, CMP_HIDDEN)
    kcvc = _compress(cmp_in, pos, w1, jnp.stack([cmp_w2_k, cmp_w2_v]).astype(BF16))
    ns = S // SEL_LEN
    o_cmp, selb = _cmp_attn(nat_b, kcvc[0], kcvc[1].transpose(0, 1, 3, 2),
                            _overlap_matrix(nch, ns).T)

    onehot = ((jnp.arange(S) // SEL_LEN)[:, None] == jnp.arange(ns)[None, :]).astype(BF16)
    o_slc = _slc_attn(nat_b, selb, k_slc.transpose(0, 2, 1, 3), onehot, v_slc.transpose(0, 2, 3, 1))

    kw_blk = (B_Q + 4 * G * HEAD_DIM) // (G * HEAD_DIM)
    (o_win,) = _band_attn(nat_b, nat_b, nat_b, q_blk=0, k_blk=kw_blk, v_blk=kw_blk + 1,
                          hq=B_HEADS, hk=B_KV, max_off=WIN_LEN - 1, tq=min(BAND_TQ, S),
                          with_lse=False)

    return _merge_ffn(x, o0, l0, o1, l1, o2, l2, o_cmp, o_slc, o_win, gates,
                      w_branch_a.astype(BF16), w_branch_b.astype(BF16), w_out.astype(BF16),
                      g_mlp.reshape(1, D), w_up.astype(BF16), w_down.astype(BF16),
                      g_final.reshape(1, D), final_norm)


def kernel(x, norm_mix_g, w_in, cmp_pos_k, cmp_w1_k, cmp_w2_k, cmp_pos_v, cmp_w1_v, cmp_w2_v,
           w_branch_a, w_branch_b, w_out, norm_mlp_g, w_up, w_down, norm_final_g):
    B, S, D = x.shape
    depth = w_in.shape[0]
    tab = _rope_table(S)
    for l in range(depth):
        x = _layer(x, norm_mix_g[l], w_in[l], cmp_pos_k[l], cmp_w1_k[l], cmp_w2_k[l],
                   cmp_pos_v[l], cmp_w1_v[l], cmp_w2_v[l], w_branch_a[l], w_branch_b[l],
                   w_out[l], norm_mlp_g[l], w_up[l], w_down[l], norm_final_g, l == depth - 1, tab)
    return x
```

```python
import functools

import jax
import jax.numpy as jnp
from jax import lax
from jax.experimental import pallas as pl
from jax.experimental.pallas import tpu as pltpu

F32 = jnp.float32
BF16 = jnp.bfloat16

D_MODEL = 1024
HEAD_DIM = 64
ROT_DIM = HEAD_DIM // 4
ROPE_THETA = 500000.0
EPS = 1e-6
DIL_PAIRS = ((128, 1), (512, 4), (2048, 16))
A_SLOTS = 8
A_HEADS = A_SLOTS * len(DIL_PAIRS)
B_HEADS = 8
B_KV = 2
CMP_LEN = 32
CMP_STRIDE = 16
CMP_HIDDEN = 4 * HEAD_DIM
SEL_LEN = 64
SEL_TOP = 16
WIN_LEN = 512
FORCE = 1e4
D_FF = 4 * D_MODEL
A_QKV = 3 * A_HEADS * HEAD_DIM
B_Q = B_HEADS * HEAD_DIM
B_KV_COLS = 6 * B_KV * HEAD_DIM
B_GATE = 3 * B_HEADS
MERGE_GATE = 2 * D_MODEL

LANES = 128
NEG = -1e30
QSCALE = HEAD_DIM ** -0.5 * 1.4426950408889634

A_COLS = 3 * A_SLOTS * HEAD_DIM
B_COLS = B_Q + B_KV_COLS
PROJ_CHUNK = 256
G_PAD = PROJ_CHUNK
G_COLS = MERGE_GATE + G_PAD

ROW_TILE = 512
VMEM_LIMIT = 56 * 1024 * 1024


def _cparams(n_axes, vmem=VMEM_LIMIT):
    return pltpu.CompilerParams(dimension_semantics=("arbitrary",) * n_axes, vmem_limit_bytes=vmem)


def _rms(x, g):
    return x * lax.rsqrt(jnp.mean(x * x, axis=-1, keepdims=True) + EPS) * g


def _dot(a, b):
    return jnp.dot(a, b, preferred_element_type=F32)


def _dot_nt(a, b):
    return lax.dot_general(a, b, (((1,), (1,)), ((), ())), preferred_element_type=F32)


def _reduce_rows(x, op):
    while x.shape[0] > 8 and x.shape[0] % 16 == 0:
        half = x.shape[0] // 2
        x = op(x[:half], x[half:])
    red = jnp.max if op is jnp.maximum else jnp.sum
    return red(x, axis=0, keepdims=True)


def _split_bf16(x, parts):
    out = []
    for _ in range(parts - 1):
        hi = x.astype(BF16)
        out.append(hi)
        x = x - hi.astype(F32)
    out.append(x.astype(BF16))
    return out


_B_ROT = (True,) * (B_Q // LANES) + (True, False, True, False, True, False)


def _rotary(x, cos_t, sin_a, sin_b):
    half = ROT_DIM // 2
    return x * cos_t + pltpu.roll(x, LANES - half, 1) * sin_a + pltpu.roll(x, half, 1) * sin_b


def _in_proj_kernel(x_ref, g_ref, w_ref, wg_ref, tab_ref,
                    a0_ref, a1_ref, a2_ref, b_ref, cmp_ref, gate_ref, u_scr):
    T = x_ref.shape[0]
    u = _rms(x_ref[...], g_ref[...])
    nslab = u_scr.shape[0]
    for k in range(nslab):
        u_scr[k] = u[:, k * LANES:(k + 1) * LANES]
    u_nat = u.astype(BF16)

    def dilated(load, d):
        rows = T // d
        return jnp.concatenate([load(pl.ds(r, rows, stride=d)) for r in range(d)], axis=0)

    def u_rows(rows):
        return jnp.concatenate([u_scr[k, rows, :] for k in range(nslab)], axis=1).astype(BF16)

    q_slabs = A_SLOTS * HEAD_DIM // LANES
    per = PROJ_CHUNK // LANES

    def project(lhs, weights, chunk_cols, rot, tables, store, n_q=0):
        for c, w_col in enumerate(chunk_cols):
            res = _dot(lhs, weights[:, w_col:w_col + PROJ_CHUNK])
            for k in range(per):
                slab = res[:, k * LANES:(k + 1) * LANES]
                if rot[c * per + k]:
                    slab = _rotary(slab, tables[0], tables[1], tables[2])
                if c * per + k < n_q:
                    slab = slab * QSCALE
                store((c * per + k) * LANES, slab)

    a_rot = (True,) * (2 * q_slabs) + (False,) * q_slabs
    group_w = A_SLOTS * HEAD_DIM

    def group_cols(g):
        return [t * A_HEADS * HEAD_DIM + g * group_w + c
                for t in range(3) for c in range(0, group_w, PROJ_CHUNK)]

    tab0 = tuple(tab_ref[k] for k in range(3))

    def store_a0(c0, res):
        a0_ref[:, c0:c0 + LANES] = res.astype(BF16)

    project(u_nat, w_ref, group_cols(0), a_rot, tab0, store_a0, n_q=q_slabs)

    for g, (d, a_ref) in enumerate(((4, a1_ref), (16, a2_ref)), start=1):
        rows = T // d
        tabs = tuple(dilated(lambda rs, k=k: tab_ref[k, rs, :], d) for k in range(3))

        def store_ad(c0, res, a_ref=a_ref, d=d, rows=rows):
            for r in range(d):
                a_ref[r, :, c0:c0 + LANES] = res[r * rows:(r + 1) * rows].astype(BF16)

        project(dilated(u_rows, d), w_ref, group_cols(g), a_rot, tabs, store_ad, n_q=q_slabs)

    def store_b(c0, res):
        b_ref[:, c0:c0 + LANES] = res.astype(BF16)
        if B_Q <= c0 < B_Q + 2 * LANES:
            cmp_ref[:, c0 - B_Q:c0 - B_Q + LANES] = res

    project(u_nat, w_ref, range(A_QKV, A_QKV + B_COLS, PROJ_CHUNK), _B_ROT, tab0, store_b,
            n_q=q_slabs)

    def store_g(c0, res):
        gate_ref[:, c0:c0 + LANES] = jax.nn.sigmoid(res).astype(gate_ref.dtype)

    project(u_nat, wg_ref, range(0, G_COLS, PROJ_CHUNK), (False,) * (G_COLS // LANES), tab0, store_g)


def _in_proj(x, g, w_main, w_gate, tab):
    B, S, D = x.shape
    T = ROW_TILE
    nt = S // T
    out_shape = (
        jax.ShapeDtypeStruct((B, S, A_COLS), BF16),
        jax.ShapeDtypeStruct((B * 4, S // 4, A_COLS), BF16),
        jax.ShapeDtypeStruct((B * 16, S // 16, A_COLS), BF16),
        jax.ShapeDtypeStruct((B, S, B_COLS), BF16),
        jax.ShapeDtypeStruct((B, S, 2 * LANES), F32),
        jax.ShapeDtypeStruct((B, S, G_COLS), BF16),
    )
    in_specs = [
        pl.BlockSpec((None, T, D), lambda b, i: (b, i, 0)),
        pl.BlockSpec((1, D), lambda b, i: (0, 0)),
        pl.BlockSpec(w_main.shape, lambda b, i: (0, 0), pipeline_mode=pl.Buffered(1)),
        pl.BlockSpec(w_gate.shape, lambda b, i: (0, 0), pipeline_mode=pl.Buffered(1)),
        pl.BlockSpec((3, T, LANES), lambda b, i: (0, i, 0)),
    ]
    out_specs = (
        pl.BlockSpec((None, T, A_COLS), lambda b, i: (b, i, 0)),
        pl.BlockSpec((4, T // 4, A_COLS), lambda b, i: (b, i, 0)),
        pl.BlockSpec((16, T // 16, A_COLS), lambda b, i: (b, i, 0)),
        pl.BlockSpec((None, T, B_COLS), lambda b, i: (b, i, 0)),
        pl.BlockSpec((None, T, 2 * LANES), lambda b, i: (b, i, 0)),
        pl.BlockSpec((None, T, G_COLS), lambda b, i: (b, i, 0)),
    )
    return pl.pallas_call(
        _in_proj_kernel, grid=(B, nt), in_specs=in_specs, out_specs=out_specs, out_shape=out_shape,
        scratch_shapes=[pltpu.VMEM((D // LANES, T, LANES), F32)], compiler_params=_cparams(2),
        name="in_proj",
    )(x, g, w_main, w_gate, tab)


def _gelu_tanh(x):
    return 0.5 * x * (1.0 + jnp.tanh(0.7978845608028654 * (x + 0.044715 * (x * x * x))))


def _compress_kernel(x_ref, pos_ref, w1_ref, w2_ref, o_ref):
    nch = x_ref.shape[0] // CMP_STRIDE
    half = CMP_STRIDE * LANES
    chunks = jnp.concatenate(
        [x_ref[pl.ds(j, nch, stride=CMP_STRIDE), :] for j in range(CMP_STRIDE)], axis=1).astype(BF16)
    lane_group = (lax.broadcasted_iota(jnp.int32, (1, half), 1) % LANES) // HEAD_DIM
    pb = _dot(pos_ref[...], w1_ref[...])[0:1]
    for g in range(B_KV):
        c = jnp.where(lane_group == g, chunks, jnp.zeros((), BF16))
        p1 = _dot(c, w1_ref[:half, :])
        p2 = _dot(c, w1_ref[half:, :])
        p2_next = pltpu.roll(p2, nch - 1, 0)
        h = _gelu_tanh(p1 + p2_next + pb)
        o_ref[g] = _dot(h.astype(BF16), w2_ref[...]).astype(BF16)


def _compress(cmp_in, pos, w1, w2):
    B, S, _ = cmp_in.shape
    nch = S // CMP_STRIDE
    kx = 2 * CMP_STRIDE * LANES
    return pl.pallas_call(
        _compress_kernel, grid=(2, B),
        in_specs=[
            pl.BlockSpec((None, S, LANES), lambda t, b: (b, 0, t)),
            pl.BlockSpec((None, 8, kx), lambda t, b: (t, 0, 0)),
            pl.BlockSpec((None, kx, CMP_HIDDEN), lambda t, b: (t, 0, 0)),
            pl.BlockSpec((None, CMP_HIDDEN, HEAD_DIM), lambda t, b: (t, 0, 0)),
        ],
        out_specs=pl.BlockSpec((None, None, B_KV, nch, HEAD_DIM), lambda t, b: (t, b, 0, 0, 0)),
        out_shape=jax.ShapeDtypeStruct((2, B, B_KV, nch, HEAD_DIM), BF16),
        compiler_params=_cparams(2), name="compress",
    )(cmp_in, pos, w1, w2)


CMP_TQ = 1024


CMP_KEY_CHUNK = 128


def _cmp_attn_kernel(q_ref, kc_ref, vct_ref, ovt_ref, o_ref, sel_ref, imp_scr):
    i = pl.program_id(1)
    tq = q_ref.shape[0]
    nc = kc_ref.shape[1]
    ns = ovt_ref.shape[0]
    rep = B_HEADS // B_KV
    t = i * tq + lax.broadcasted_iota(jnp.int32, (1, tq), 1)
    live = jnp.where(t >= CMP_LEN - 1, 1.0, 0.0)
    blk = lax.broadcasted_iota(jnp.int32, (ns, 1), 0)
    blk_f = blk.astype(F32)
    cur = jnp.right_shift(t, SEL_LEN.bit_length() - 1)
    removed = -3e38
    forced = (blk == 0) | (blk == cur) | (blk == cur - 1)
    keep = jnp.where(forced | (blk > cur), 0.0, 1.0)
    pinned = jnp.where(forced, removed, jnp.where(blk > cur, -FORCE, 0.0))

    def attend(nk):
        n = lax.broadcasted_iota(jnp.int32, (nk, 1), 0)
        hidden = jnp.where((n * CMP_STRIDE + (CMP_LEN - 1)) <= t, 0.0, NEG)
        ovt = ovt_ref[:, :nk]
        o_t = []
        for g in range(B_KV):
            kc = kc_ref[g, :nk, :]
            vct = vct_ref[g, :, :nk]
            psum = jnp.zeros((nk, tq), F32)
            for r in range(rep):
                h = g * rep + r
                qh = q_ref[:, h * HEAD_DIM:(h + 1) * HEAD_DIM]
                s = _dot_nt(kc, qh) + hidden
                p = jnp.exp2(s - _reduce_rows(s, jnp.maximum))
                p = p * (live / _reduce_rows(p, jnp.add))
                o_t.append(_dot(vct, p.astype(BF16)))
                psum = psum + p
            imp = sum(_dot(ovt, piece) for piece in _split_bf16(psum, 3))
            imp_scr[:, g * tq:(g + 1) * tq] = imp * keep + pinned
        o_ref[...] = jnp.transpose(jnp.concatenate(o_t, axis=0)).astype(BF16)

    n_chunks = nc // CMP_KEY_CHUNK
    last_chunk = ((i + 1) * (tq // CMP_STRIDE) - 1) // CMP_KEY_CHUNK
    for c in range(n_chunks):
        pl.when(last_chunk == c)(functools.partial(attend, (c + 1) * CMP_KEY_CHUNK))

    def take_next(imp, allow=None):
        rank = blk_f[:imp.shape[0]]
        mx = jnp.max(imp, axis=0, keepdims=True)
        first = jnp.min(jnp.where(imp == mx, rank, float(ns)), axis=0, keepdims=True)
        hit = rank == first
        if allow is not None:
            hit = hit & allow
        return jnp.where(hit, removed, imp)

    def rank_blocks(nb):
        imp = imp_scr[:nb, :]
        for _ in range(SEL_TOP - 3):
            imp = take_next(imp)
        imp_scr[:nb, :] = imp

    early = (i + 1) * tq <= (ns // 2) * SEL_LEN
    pl.when(early)(functools.partial(rank_blocks, ns // 2))
    pl.when(jnp.logical_not(early))(functools.partial(rank_blocks, ns))

    @pl.when(i * tq < 2 * SEL_LEN)
    def _():
        t2 = jnp.concatenate([t] * B_KV, axis=1)
        more = take_next(imp_scr[...], t2 < 2 * SEL_LEN)
        imp_scr[...] = take_next(more, t2 < SEL_LEN)

    for g in range(B_KV):
        picked = imp_scr[:, g * tq:(g + 1) * tq] < 0.5 * removed
        sel_ref[g] = jnp.where(picked, 0.0, NEG).astype(BF16)


def _cmp_attn(nat_b, kc, vct, overlap_t):
    B, S, _ = nat_b.shape
    nc = kc.shape[2]
    ns = overlap_t.shape[0]
    tq = CMP_TQ
    return pl.pallas_call(
        _cmp_attn_kernel, grid=(B, S // tq),
        in_specs=[
            pl.BlockSpec((None, tq, B_Q), lambda b, i: (b, i, 0)),
            pl.BlockSpec((None, B_KV, nc, HEAD_DIM), lambda b, i: (b, 0, 0, 0)),
            pl.BlockSpec((None, B_KV, HEAD_DIM, nc), lambda b, i: (b, 0, 0, 0)),
            pl.BlockSpec((ns, nc), lambda b, i: (0, 0)),
        ],
        out_specs=(
            pl.BlockSpec((None, tq, B_Q), lambda b, i: (b, i, 0)),
            pl.BlockSpec((None, B_KV, ns, tq), lambda b, i: (b, 0, 0, i)),
        ),
        out_shape=(
            jax.ShapeDtypeStruct((B, S, B_Q), BF16),
            jax.ShapeDtypeStruct((B, B_KV, ns, S), BF16),
        ),
        scratch_shapes=[pltpu.VMEM((ns, B_KV * tq), F32)],
        compiler_params=_cparams(2), name="cmp_attn",
    )(nat_b, kc, vct, overlap_t)


SLC_TK = 256
SLC_TQ = 2 * SLC_TK


SLC_VROWS = 80
SLC_CW = 256


def _slc_attn_kernel(q_ref, selt_ref, kin_ref, hot_ref, vin_ref, o_ref,
                     k_ref, vt_ref, qat_scr, m_scr, acc_scr, s_a, s_b):
    i = pl.program_id(2)
    tq = q_ref.shape[0]
    rep = B_HEADS // B_KV

    @pl.when(i == 0)
    def _():
        k_ref[:, :HEAD_DIM] = kin_ref[...]
        k_ref[:, HEAD_DIM:] = hot_ref[...]
        vt_ref[:HEAD_DIM, :] = vin_ref[...]
        pad_rows = vt_ref.shape[0] - HEAD_DIM
        row = lax.broadcasted_iota(jnp.int32, (pad_rows, vt_ref.shape[1]), 0)
        vt_ref[HEAD_DIM:, :] = jnp.where(row == 0, 1.0, 0.0).astype(BF16)

    selt = selt_ref[...]
    for r in range(rep):
        cols = slice(r * tq, (r + 1) * tq)
        q_r = q_ref[:, r * HEAD_DIM:(r + 1) * HEAD_DIM].astype(F32)
        qat_scr[:HEAD_DIM, cols] = jnp.transpose(q_r).astype(BF16)
        qat_scr[HEAD_DIM:, cols] = selt
    m_scr[...] = jnp.full(m_scr.shape, NEG, F32)
    acc_scr[...] = jnp.zeros(acc_scr.shape, F32)
    rel = (lax.broadcasted_iota(jnp.int32, (SLC_TK, tq), 1)
           - lax.broadcasted_iota(jnp.int32, (SLC_TK, tq), 0))

    def scores(j, s_buf, q_from=0):
        k0 = pl.multiple_of(j * SLC_TK, SLC_TK)
        k_tile = k_ref[pl.ds(k0, SLC_TK), :]
        if q_from == 0:
            s_buf[...] = _dot(k_tile, qat_scr[...])
        else:
            for r in range(rep):
                cols = slice(r * tq + q_from, (r + 1) * tq)
                s_buf[:, cols] = _dot(k_tile, qat_scr[:, cols])

    def update(j, s_buf, diag_offset=None):
        k0 = pl.multiple_of(j * SLC_TK, SLC_TK)
        vt = vt_ref[:, pl.ds(k0, SLC_TK)]
        for c in range(rep * tq // SLC_CW):
            cols = slice(c * SLC_CW, (c + 1) * SLC_CW)
            q_lo = (c * SLC_CW) % tq
            if diag_offset is not None and q_lo + SLC_CW <= diag_offset:
                continue
            s = s_buf[:, cols]
            if diag_offset is not None:
                s = jnp.where(rel[:, q_lo:q_lo + SLC_CW] >= diag_offset, s, NEG)
            m_prev = m_scr[:, cols]
            m_new = jnp.maximum(m_prev, jnp.max(s, axis=0, keepdims=True))
            alpha = jnp.exp2(m_prev - m_new)
            p = jnp.exp2(s - m_new).astype(BF16)
            acc_scr[:, cols] = alpha * acc_scr[:, cols] + _dot(vt, p)
            m_scr[:, cols] = m_new

    def tile_pair(jj, carry):
        a = 2 * jj
        scores(a + 1, s_b)
        update(a, s_a)
        scores(a + 2, s_a)
        update(a + 1, s_b)
        return carry

    def two_pairs(jj, carry):
        tile_pair(2 * jj, carry)
        return tile_pair(2 * jj + 1, carry)

    scores(0, s_a)
    lax.fori_loop(0, i // 2, two_pairs, 0)

    @pl.when(i % 2 == 1)
    def _():
        tile_pair(i - 1, 0)

    scores(2 * i + 1, s_b, q_from=SLC_TK)
    update(2 * i, s_a, diag_offset=0)
    update(2 * i + 1, s_b, diag_offset=SLC_TK)

    for r in range(rep):
        acc = acc_scr[:, r * tq:(r + 1) * tq]
        o_t = acc[:HEAD_DIM] / acc[HEAD_DIM:HEAD_DIM + 1]
        o_ref[:, r * HEAD_DIM:(r + 1) * HEAD_DIM] = jnp.transpose(o_t).astype(BF16)


def _slc_attn(nat_b, selt, k_t, onehot, v_t):
    B, S, _ = nat_b.shape
    G = B_KV
    rep = B_HEADS // B_KV
    ns = selt.shape[2]
    tq = SLC_TQ
    gw = rep * HEAD_DIM
    return pl.pallas_call(
        _slc_attn_kernel, grid=(B, G, S // tq),
        in_specs=[
            pl.BlockSpec((None, tq, gw), lambda b, g, i: (b, i, g)),
            pl.BlockSpec((None, None, ns, tq), lambda b, g, i: (b, g, 0, i)),
            pl.BlockSpec((None, None, S, HEAD_DIM), lambda b, g, i: (b, g, 0, 0)),
            pl.BlockSpec((S, ns), lambda b, g, i: (0, 0), pipeline_mode=pl.Buffered(1)),
            pl.BlockSpec((None, None, HEAD_DIM, S), lambda b, g, i: (b, g, 0, 0)),
        ],
        out_specs=pl.BlockSpec((None, tq, gw), lambda b, g, i: (b, i, g)),
        out_shape=jax.ShapeDtypeStruct((B, S, B_Q), BF16),
        scratch_shapes=[pltpu.VMEM((S, HEAD_DIM + ns), BF16), pltpu.VMEM((SLC_VROWS, S), BF16),
                        pltpu.VMEM((HEAD_DIM + ns, rep * tq), BF16),
                        pltpu.VMEM((1, rep * tq), F32), pltpu.VMEM((SLC_VROWS, rep * tq), F32),
                        pltpu.VMEM((SLC_TK, rep * tq), F32), pltpu.VMEM((SLC_TK, rep * tq), F32)],
        compiler_params=_cparams(3), name="slc_attn",
    )(nat_b, selt, k_t, onehot, v_t)


SUB = 128
BAND_TQ = 1024


def _band_kernel(q_ref, kp_ref, kc_ref, vp_ref, vc_ref, o_ref, *lse_refs, shared_kv, prev, max_off):
    i = pl.program_id(1)
    tq = q_ref.shape[0]
    n_slab = q_ref.shape[1] // LANES
    width = prev + SUB
    r_i = lax.broadcasted_iota(jnp.int32, (SUB, width), 0)
    c_i = lax.broadcasted_iota(jnp.int32, (SUB, width), 1)
    diff = r_i - c_i + prev
    in_band = (diff >= 0) & (diff <= max_off)
    lane = lax.broadcasted_iota(jnp.int32, (1, LANES), 1)
    lo_half = lane < HEAD_DIM
    half_mask = {"lo": lo_half, "hi": jnp.logical_not(lo_half)}

    def window(p_ref, c_ref, sub, slab):
        cols = slice(slab * LANES, (slab + 1) * LANES)
        lo = sub * SUB
        parts = []
        if lo < prev:
            parts.append(p_ref[lo:prev, cols])
        parts.append(c_ref[max(0, lo - prev):lo + SUB, cols])
        return jnp.concatenate(parts, axis=0) if len(parts) > 1 else parts[0]

    def swap_halves(x):
        return jnp.concatenate([x[:, HEAD_DIM:], x[:, :HEAD_DIM]], axis=1)

    for sub in range(tq // SUB):
        rows = slice(sub * SUB, (sub + 1) * SUB)
        k_start = i * tq + sub * SUB - prev
        bias = jnp.where(in_band & (c_i + k_start >= 0), 0.0, NEG)
        if shared_kv:
            k2 = window(kp_ref, kc_ref, sub, 0)
            v2 = window(vp_ref, vc_ref, sub, 0)
            groups = [
                (((0, "lo"), (1, "lo"), (2, "hi"), (3, "hi")), k2, v2),
                (((0, "hi"), (1, "hi"), (2, "lo"), (3, "lo")), swap_halves(k2), swap_halves(v2)),
            ]
        else:
            groups = [(((p, "lo"), (p, "hi")), window(kp_ref, kc_ref, sub, p),
                       window(vp_ref, vc_ref, sub, p)) for p in range(n_slab)]
        outs = {}
        for members, k2, v2 in groups:
            nm = len(members)
            zero = jnp.zeros((), BF16)
            lhs = jnp.concatenate(
                [jnp.where(half_mask[half], q_ref[rows, slab * LANES:(slab + 1) * LANES], zero)
                 for slab, half in members], axis=0)
            s = _dot_nt(lhs, k2)
            s = (s.reshape(nm, SUB, width) + bias[None]).reshape(nm * SUB, width)
            m = jnp.max(s, axis=-1, keepdims=True)
            p = jnp.exp2(s - m)
            den = jnp.sum(p, axis=-1, keepdims=True)
            pv = _dot(p.astype(BF16), v2) / den
            lse = m + jnp.log2(den)
            for idx, member in enumerate(members):
                outs[member] = (pv[idx * SUB:(idx + 1) * SUB], lse[idx * SUB:(idx + 1) * SUB])
        lse_tile = jnp.zeros((SUB, LANES), F32)
        for slab in range(n_slab):
            cols = slice(slab * LANES, (slab + 1) * LANES)
            (o_lo, l_lo), (o_hi, l_hi) = outs[(slab, "lo")], outs[(slab, "hi")]
            o_ref[rows, cols] = jnp.where(lo_half, o_lo, o_hi).astype(o_ref.dtype)
            lse_tile = jnp.where(lane == 2 * slab, l_lo,
                                 jnp.where(lane == 2 * slab + 1, l_hi, lse_tile))
        if lse_refs:
            lse_refs[0][rows, :] = lse_tile


def _band_attn(q_arr, k_arr, v_arr, *, q_blk, k_blk, v_blk, hq, hk, max_off, tq, with_lse):
    N, L, _ = q_arr.shape
    assert hq == hk or (hq, hk) == (8, 2), (hq, hk)
    prev = -(-max_off // SUB) * SUB
    m = tq // prev
    qw, kw = hq * HEAD_DIM, hk * HEAD_DIM
    cur = lambda blk: (lambda n, i: (n, i, blk))
    prv = lambda blk: (lambda n, i: (n, jnp.maximum(i * m - 1, 0), blk))
    out_shape = [jax.ShapeDtypeStruct((N, L, qw), BF16)]
    out_specs = [pl.BlockSpec((None, tq, qw), cur(0))]
    if with_lse:
        out_shape.append(jax.ShapeDtypeStruct((N, L, LANES), F32))
        out_specs.append(pl.BlockSpec((None, tq, LANES), cur(0)))
    return pl.pallas_call(
        functools.partial(_band_kernel, shared_kv=hq != hk, prev=prev, max_off=max_off),
        grid=(N, L // tq),
        in_specs=[
            pl.BlockSpec((None, tq, qw), cur(q_blk)),
            pl.BlockSpec((None, prev, kw), prv(k_blk)),
            pl.BlockSpec((None, tq, kw), cur(k_blk)),
            pl.BlockSpec((None, prev, kw), prv(v_blk)),
            pl.BlockSpec((None, tq, kw), cur(v_blk)),
        ],
        out_specs=tuple(out_specs), out_shape=tuple(out_shape),
        compiler_params=_cparams(2), name="band_attn",
    )(q_arr, k_arr, k_arr, v_arr, v_arr)


FF_CHUNK = 512

def _merge_ffn_kernel(x_ref, o0_ref, l0_ref, o1_ref, l1_ref, o2_ref, l2_ref, oc_ref, os_ref, ow_ref,
                      gate_ref, wa_ref, wb_ref, wo_ref, gm_ref, wu_ref, wd_ref, gf_ref,
                      out_ref, s_o1, s_l1, s_o2, s_l2, x1_scr, *, final_norm):
    step = pl.program_id(0)
    T = x_ref.shape[0]

    @pl.when(step == 0)
    def _():
        x1_scr[1] = jnp.zeros(x1_scr.shape[1:], F32)

    x1_prev = x1_scr[(step + 1) % 2]
    u = _rms(x1_prev, gm_ref[...]).astype(BF16)
    acc = x1_prev
    for c in range(D_FF // FF_CHUNK):
        cols = slice(c * FF_CHUNK, (c + 1) * FF_CHUNK)
        h = jnp.maximum(_dot(u, wu_ref[:, cols]), 0.0)
        acc = acc + _dot((h * h).astype(BF16), wd_ref[cols, :])
    out_ref[...] = _rms(acc, gf_ref[...]) if final_norm else acc

    for d, o_ref, l_ref, s_o, s_l in ((4, o1_ref, l1_ref, s_o1, s_l1), (16, o2_ref, l2_ref, s_o2, s_l2)):
        rows = T // d
        for r in range(d):
            o_r = o_ref[r].astype(F32)
            s_l[pl.ds(r, rows, stride=d), :] = l_ref[r]
            for k in range(s_o.shape[0]):
                s_o[k, pl.ds(r, rows, stride=d), :] = o_r[:, k * LANES:(k + 1) * LANES]
    gather = lambda s: jnp.concatenate([s[k] for k in range(s.shape[0])], axis=1)

    l0, l1, l2 = l0_ref[...], s_l1[...], s_l2[...]
    mx = jnp.maximum(jnp.maximum(l0, l1), l2)
    e0, e1, e2 = jnp.exp2(l0 - mx), jnp.exp2(l1 - mx), jnp.exp2(l2 - mx)
    inv = 1.0 / (e0 + e1 + e2)
    lo_half = lax.broadcasted_iota(jnp.int32, (1, LANES), 1) < HEAD_DIM

    def spread(w, lane_of_head):
        col = lambda h: jnp.broadcast_to(w[:, lane_of_head(h):lane_of_head(h) + 1], (T, LANES))
        return jnp.concatenate([jnp.where(lo_half, col(2 * p), col(2 * p + 1))
                                for p in range(A_SLOTS // 2)], axis=1)

    slot = lambda h: h
    y_a = (spread(e0 * inv, slot) * o0_ref[...].astype(F32) + spread(e1 * inv, slot) * gather(s_o1)
           + spread(e2 * inv, slot) * gather(s_o2))

    gb = gate_ref[:, MERGE_GATE:MERGE_GATE + LANES].astype(F32)
    y_b = (spread(gb, lambda h: 3 * h) * oc_ref[...].astype(F32)
           + spread(gb, lambda h: 3 * h + 1) * os_ref[...].astype(F32)
           + spread(gb, lambda h: 3 * h + 2) * ow_ref[...].astype(F32))
    merged = (gate_ref[:, :D_MODEL].astype(F32) * _dot(y_a.astype(BF16), wa_ref[...])
              + gate_ref[:, D_MODEL:MERGE_GATE].astype(F32) * _dot(y_b.astype(BF16), wb_ref[...]))
    x1_scr[step % 2] = x_ref[...] + _dot(merged.astype(BF16), wo_ref[...])


def _merge_ffn(x, o0, l0, o1, l1, o2, l2, o_cmp, o_slc, o_win, gates, wa, wb, wo,
               g_mlp, w_up, w_down, g_final, final_norm):
    B, S, D = x.shape
    T = ROW_TILE
    nt = S // T
    n_tiles = B * nt
    qw = A_SLOTS * HEAD_DIM

    def tile_map(shift):
        def index_map(k):
            m = jnp.clip(k - shift, 0, n_tiles - 1)
            return (m // nt, m % nt, 0)
        return index_map

    nat = lambda w, shift=0: pl.BlockSpec((None, T, w), tile_map(shift))
    dil = lambda d, w: pl.BlockSpec((d, T // d, w), tile_map(0))
    full = lambda a: pl.BlockSpec(a.shape, lambda k: (0,) * a.ndim, pipeline_mode=pl.Buffered(1))
    o_scr, l_scr = pltpu.VMEM((qw // LANES, T, LANES), F32), pltpu.VMEM((T, LANES), F32)
    return pl.pallas_call(
        functools.partial(_merge_ffn_kernel, final_norm=final_norm), grid=(n_tiles + 1,),
        in_specs=[nat(D), nat(qw), nat(LANES), dil(4, qw), dil(4, LANES), dil(16, qw),
                  dil(16, LANES), nat(B_Q), nat(B_Q), nat(B_Q), nat(G_COLS), full(wa),
                  full(wb), full(wo), full(g_mlp), full(w_up), full(w_down), full(g_final)],
        out_specs=nat(D, shift=1), out_shape=jax.ShapeDtypeStruct((B, S, D), F32),
        scratch_shapes=[o_scr, l_scr, o_scr, l_scr, pltpu.VMEM((2, T, D), F32)],
        compiler_params=_cparams(1), name="merge_ffn",
    )(x, o0, l0, o1, l1, o2, l2, o_cmp, o_slc, o_win, gates, wa, wb, wo,
      g_mlp, w_up, w_down, g_final)


def _rope_table(seq):
    half = ROT_DIM // 2
    pos = jnp.arange(seq, dtype=F32)[:, None]
    inv = ROPE_THETA ** (-jnp.arange(0, ROT_DIM, 2, dtype=F32) / ROT_DIM)
    ang = pos * inv[None, :]
    cos, sin = jnp.cos(ang), jnp.sin(ang)
    pad = lambda parts, fill: jnp.tile(jnp.concatenate(
        parts + [jnp.full((seq, HEAD_DIM - ROT_DIM), fill, F32)], axis=1), (1, LANES // HEAD_DIM))
    zero = jnp.zeros_like(sin)
    return jnp.stack([pad([cos, cos], 1.0), pad([-sin, zero], 0.0), pad([zero, sin], 0.0)])


def _layer_weights(w_in):
    o3 = A_QKV + B_COLS
    o4 = o3 + B_GATE
    w_main = w_in.astype(BF16)
    w_gate = jnp.concatenate(
        [w_main[:, o4:], jnp.pad(w_main[:, o3:o4], ((0, 0), (0, G_PAD - B_GATE)))], axis=1)
    return w_main, w_gate


def _overlap_matrix(nc_pad, ns):
    c_start = jnp.arange(nc_pad) * CMP_STRIDE
    s_start = jnp.arange(ns) * SEL_LEN
    ov = (c_start[:, None] < s_start[None, :] + SEL_LEN) & (c_start[:, None] + CMP_LEN > s_start[None, :])
    return ov.astype(BF16)


def _layer(x, g_mix, w_in, cmp_pos_k, cmp_w1_k, cmp_w2_k, cmp_pos_v, cmp_w1_v, cmp_w2_v,
           w_branch_a, w_branch_b, w_out, g_mlp, w_up, w_down, g_final, final_norm, tab):
    B, S, D = x.shape
    G = B_KV
    a0, a1, a2, nat_b, cmp_in, gates = _in_proj(x, g_mix.reshape(1, D), *_layer_weights(w_in), tab)

    a_out = []
    for arr, (w, d) in zip((a0, a1, a2), DIL_PAIRS):
        a_out.append(_band_attn(arr, arr, arr, q_blk=0, k_blk=1, v_blk=2, hq=A_SLOTS, hk=A_SLOTS,
                                max_off=w // d, tq=min(BAND_TQ, S // d), with_lse=True))
    (o0, l0), (o1, l1), (o2, l2) = a_out

    nch = S // CMP_STRIDE
    kv_w = G * HEAD_DIM
    k_slc = nat_b[:, :, B_Q + 2 * kv_w:B_Q + 3 * kv_w].reshape(B, S, G, HEAD_DIM)
    v_slc = nat_b[:, :, B_Q + 3 * kv_w:B_Q + 4 * kv_w].reshape(B, S, G, HEAD_DIM)
    pos = jnp.stack([cmp_pos_k, cmp_pos_v]).astype(BF16)[:, :, None, :]
    pos = jnp.pad(pos, ((0, 0), (0, 0), (0, G - 1), (0, 0))).reshape(2, 1, CMP_LEN * G * HEAD_DIM)
    pos = jnp.broadcast_to(pos, (2, 8, CMP_LEN * G * HEAD_DIM))
    w1 = jnp.stack([cmp_w1_k, cmp_w1_v]).astype(BF16).reshape(2, CMP_LEN, 1, HEAD_DIM, CMP_HIDDEN)
    w1 = jnp.broadcast_to(w1, (2, CMP_LEN, G, HEAD_DIM, CMP_HIDDEN)).reshape(
        2, CMP_LEN * G * HEAD_DIM, CMP_HIDDEN)
    kcvc = _compress(cmp_in, pos, w1, jnp.stack([cmp_w2_k, cmp_w2_v]).astype(BF16))
    ns = S // SEL_LEN
    o_cmp, selb = _cmp_attn(nat_b, kcvc[0], kcvc[1].transpose(0, 1, 3, 2),
                            _overlap_matrix(nch, ns).T)

    onehot = ((jnp.arange(S) // SEL_LEN)[:, None] == jnp.arange(ns)[None, :]).astype(BF16)
    o_slc = _slc_attn(nat_b, selb, k_slc.transpose(0, 2, 1, 3), onehot, v_slc.transpose(0, 2, 3, 1))

    kw_blk = (B_Q + 4 * G * HEAD_DIM) // (G * HEAD_DIM)
    (o_win,) = _band_attn(nat_b, nat_b, nat_b, q_blk=0, k_blk=kw_blk, v_blk=kw_blk + 1,
                          hq=B_HEADS, hk=B_KV, max_off=WIN_LEN - 1, tq=min(BAND_TQ, S),
                          with_lse=False)

    return _merge_ffn(x, o0, l0, o1, l1, o2, l2, o_cmp, o_slc, o_win, gates,
                      w_branch_a.astype(BF16), w_branch_b.astype(BF16), w_out.astype(BF16),
                      g_mlp.reshape(1, D), w_up.astype(BF16), w_down.astype(BF16),
                      g_final.reshape(1, D), final_norm)


def kernel(x, norm_mix_g, w_in, cmp_pos_k, cmp_w1_k, cmp_w2_k, cmp_pos_v, cmp_w1_v, cmp_w2_v,
           w_branch_a, w_branch_b, w_out, norm_mlp_g, w_up, w_down, norm_final_g):
    B, S, D = x.shape
    depth = w_in.shape[0]
    tab = _rope_table(S)
    for l in range(depth):
        x = _layer(x, norm_mix_g[l], w_in[l], cmp_pos_k[l], cmp_w1_k[l], cmp_w2_k[l],
                   cmp_pos_v[l], cmp_w1_v[l], cmp_w2_v[l], w_branch_a[l], w_branch_b[l],
                   w_out[l], norm_mlp_g[l], w_up[l], w_down[l], norm_final_g, l == depth - 1, tab)
    return x
```

```python
import functools

import jax
import jax.numpy as jnp
from jax import lax
from jax.experimental import pallas as pl
from jax.experimental.pallas import tpu as pltpu

F32 = jnp.float32
BF16 = jnp.bfloat16

D_MODEL = 1024
HEAD_DIM = 64
ROT_DIM = HEAD_DIM // 4
ROPE_THETA = 500000.0
EPS = 1e-6
DIL_PAIRS = ((128, 1), (512, 4), (2048, 16))
A_SLOTS = 8
A_HEADS = A_SLOTS * len(DIL_PAIRS)
B_HEADS = 8
B_KV = 2
CMP_LEN = 32
CMP_STRIDE = 16
CMP_HIDDEN = 4 * HEAD_DIM
SEL_LEN = 64
SEL_TOP = 16
WIN_LEN = 512
FORCE = 1e4
D_FF = 4 * D_MODEL
A_QKV = 3 * A_HEADS * HEAD_DIM
B_Q = B_HEADS * HEAD_DIM
B_KV_COLS = 6 * B_KV * HEAD_DIM
B_GATE = 3 * B_HEADS
MERGE_GATE = 2 * D_MODEL

LANES = 128
NEG = -1e30
QSCALE = HEAD_DIM ** -0.5 * 1.4426950408889634

A_COLS = 3 * A_SLOTS * HEAD_DIM
B_COLS = B_Q + B_KV_COLS
PROJ_CHUNK = 256
G_PAD = PROJ_CHUNK
G_COLS = MERGE_GATE + G_PAD

ROW_TILE = 512
VMEM_LIMIT = 56 * 1024 * 1024


def _cparams(n_axes, vmem=VMEM_LIMIT):
    return pltpu.CompilerParams(dimension_semantics=("arbitrary",) * n_axes, vmem_limit_bytes=vmem)


def _rms(x, g):
    return x * lax.rsqrt(jnp.mean(x * x, axis=-1, keepdims=True) + EPS) * g


def _dot(a, b):
    return jnp.dot(a, b, preferred_element_type=F32)


def _dot_nt(a, b):
    return lax.dot_general(a, b, (((1,), (1,)), ((), ())), preferred_element_type=F32)


def _reduce_rows(x, op):
    while x.shape[0] > 8 and x.shape[0] % 16 == 0:
        half = x.shape[0] // 2
        x = op(x[:half], x[half:])
    red = jnp.max if op is jnp.maximum else jnp.sum
    return red(x, axis=0, keepdims=True)


def _split_bf16(x, parts):
    out = []
    for _ in range(parts - 1):
        hi = x.astype(BF16)
        out.append(hi)
        x = x - hi.astype(F32)
    out.append(x.astype(BF16))
    return out


_B_ROT = (True,) * (B_Q // LANES) + (True, False, True, False, True, False)


def _rotary(x, cos_t, sin_a, sin_b):
    half = ROT_DIM // 2
    return x * cos_t + pltpu.roll(x, LANES - half, 1) * sin_a + pltpu.roll(x, half, 1) * sin_b


def _in_proj_kernel(x_ref, g_ref, w_ref, wg_ref, tab_ref,
                    a0_ref, a1_ref, a2_ref, b_ref, cmp_ref, gate_ref, u_scr):
    T = x_ref.shape[0]
    u = _rms(x_ref[...], g_ref[...])
    nslab = u_scr.shape[0]
    for k in range(nslab):
        u_scr[k] = u[:, k * LANES:(k + 1) * LANES]
    u_nat = u.astype(BF16)

    def dilated(load, d):
        rows = T // d
        return jnp.concatenate([load(pl.ds(r, rows, stride=d)) for r in range(d)], axis=0)

    def u_rows(rows):
        return jnp.concatenate([u_scr[k, rows, :] for k in range(nslab)], axis=1).astype(BF16)

    q_slabs = A_SLOTS * HEAD_DIM // LANES
    per = PROJ_CHUNK // LANES

    def project(lhs, weights, chunk_cols, rot, tables, store, n_q=0):
        for c, w_col in enumerate(chunk_cols):
            res = _dot(lhs, weights[:, w_col:w_col + PROJ_CHUNK])
            for k in range(per):
                slab = res[:, k * LANES:(k + 1) * LANES]
                if rot[c * per + k]:
                    slab = _rotary(slab, tables[0], tables[1], tables[2])
                if c * per + k < n_q:
                    slab = slab * QSCALE
                store((c * per + k) * LANES, slab)

    a_rot = (True,) * (2 * q_slabs) + (False,) * q_slabs
    group_w = A_SLOTS * HEAD_DIM

    def group_cols(g):
        return [t * A_HEADS * HEAD_DIM + g * group_w + c
                for t in range(3) for c in range(0, group_w, PROJ_CHUNK)]

    tab0 = tuple(tab_ref[k] for k in range(3))

    def store_a0(c0, res):
        a0_ref[:, c0:c0 + LANES] = res.astype(BF16)

    project(u_nat, w_ref, group_cols(0), a_rot, tab0, store_a0, n_q=q_slabs)

    for g, (d, a_ref) in enumerate(((4, a1_ref), (16, a2_ref)), start=1):
        rows = T // d
        tabs = tuple(dilated(lambda rs, k=k: tab_ref[k, rs, :], d) for k in range(3))

        def store_ad(c0, res, a_ref=a_ref, d=d, rows=rows):
            for r in range(d):
                a_ref[r, :, c0:c0 + LANES] = res[r * rows:(r + 1) * rows].astype(BF16)

        project(dilated(u_rows, d), w_ref, group_cols(g), a_rot, tabs, store_ad, n_q=q_slabs)

    def store_b(c0, res):
        b_ref[:, c0:c0 + LANES] = res.astype(BF16)
        if B_Q <= c0 < B_Q + 2 * LANES:
            cmp_ref[:, c0 - B_Q:c0 - B_Q + LANES] = res

    project(u_nat, w_ref, range(A_QKV, A_QKV + B_COLS, PROJ_CHUNK), _B_ROT, tab0, store_b,
            n_q=q_slabs)

    def store_g(c0, res):
        gate_ref[:, c0:c0 + LANES] = jax.nn.sigmoid(res).astype(gate_ref.dtype)

    project(u_nat, wg_ref, range(0, G_COLS, PROJ_CHUNK), (False,) * (G_COLS // LANES), tab0, store_g)


def _in_proj(x, g, w_main, w_gate, tab):
    B, S, D = x.shape
    T = ROW_TILE
    nt = S // T
    out_shape = (
        jax.ShapeDtypeStruct((B, S, A_COLS), BF16),
        jax.ShapeDtypeStruct((B * 4, S // 4, A_COLS), BF16),
        jax.ShapeDtypeStruct((B * 16, S // 16, A_COLS), BF16),
        jax.ShapeDtypeStruct((B, S, B_COLS), BF16),
        jax.ShapeDtypeStruct((B, S, 2 * LANES), F32),
        jax.ShapeDtypeStruct((B, S, G_COLS), BF16),
    )
    in_specs = [
        pl.BlockSpec((None, T, D), lambda b, i: (b, i, 0)),
        pl.BlockSpec((1, D), lambda b, i: (0, 0)),
        pl.BlockSpec(w_main.shape, lambda b, i: (0, 0), pipeline_mode=pl.Buffered(1)),
        pl.BlockSpec(w_gate.shape, lambda b, i: (0, 0), pipeline_mode=pl.Buffered(1)),
        pl.BlockSpec((3, T, LANES), lambda b, i: (0, i, 0)),
    ]
    out_specs = (
        pl.BlockSpec((None, T, A_COLS), lambda b, i: (b, i, 0)),
        pl.BlockSpec((4, T // 4, A_COLS), lambda b, i: (b, i, 0)),
        pl.BlockSpec((16, T // 16, A_COLS), lambda b, i: (b, i, 0)),
        pl.BlockSpec((None, T, B_COLS), lambda b, i: (b, i, 0)),
        pl.BlockSpec((None, T, 2 * LANES), lambda b, i: (b, i, 0)),
        pl.BlockSpec((None, T, G_COLS), lambda b, i: (b, i, 0)),
    )
    return pl.pallas_call(
        _in_proj_kernel, grid=(B, nt), in_specs=in_specs, out_specs=out_specs, out_shape=out_shape,
        scratch_shapes=[pltpu.VMEM((D // LANES, T, LANES), F32)], compiler_params=_cparams(2),
        name="in_proj",
    )(x, g, w_main, w_gate, tab)


def _gelu_tanh(x):
    return 0.5 * x * (1.0 + jnp.tanh(0.7978845608028654 * (x + 0.044715 * (x * x * x))))


def _compress_kernel(x_ref, pos_ref, w1_ref, w2_ref, o_ref):
    nch = x_ref.shape[0] // CMP_STRIDE
    half = CMP_STRIDE * LANES
    chunks = jnp.concatenate(
        [x_ref[pl.ds(j, nch, stride=CMP_STRIDE), :] for j in range(CMP_STRIDE)], axis=1).astype(BF16)
    pb = _dot(pos_ref[...], w1_ref[...])[0:1]
    p1 = _dot(chunks, w1_ref[:half, :])
    p2 = _dot(chunks, w1_ref[half:, :])
    p2_next = pltpu.roll(p2, nch - 1, 0)
    h = _gelu_tanh(p1 + p2_next + pb).astype(BF16)
    for g in range(B_KV):
        o_ref[g] = _dot(h[:, g * CMP_HIDDEN:(g + 1) * CMP_HIDDEN], w2_ref[...]).astype(BF16)


def _compress(cmp_in, pos, w1, w2):
    B, S, _ = cmp_in.shape
    nch = S // CMP_STRIDE
    kx = 2 * CMP_STRIDE * LANES
    return pl.pallas_call(
        _compress_kernel, grid=(2, B),
        in_specs=[
            pl.BlockSpec((None, S, LANES), lambda t, b: (b, 0, t)),
            pl.BlockSpec((None, 8, kx), lambda t, b: (t, 0, 0)),
            pl.BlockSpec((None, kx, B_KV * CMP_HIDDEN), lambda t, b: (t, 0, 0)),
            pl.BlockSpec((None, CMP_HIDDEN, HEAD_DIM), lambda t, b: (t, 0, 0)),
        ],
        out_specs=pl.BlockSpec((None, None, B_KV, nch, HEAD_DIM), lambda t, b: (t, b, 0, 0, 0)),
        out_shape=jax.ShapeDtypeStruct((2, B, B_KV, nch, HEAD_DIM), BF16),
        compiler_params=_cparams(2), name="compress",
    )(cmp_in, pos, w1, w2)


CMP_TQ = 1024


CMP_KEY_CHUNK = 128


def _cmp_attn_kernel(q_ref, kc_ref, vct_ref, ovt_ref, o_ref, sel_ref, imp_scr):
    i = pl.program_id(1)
    tq = q_ref.shape[0]
    nc = kc_ref.shape[1]
    ns = ovt_ref.shape[0]
    rep = B_HEADS // B_KV
    t = i * tq + lax.broadcasted_iota(jnp.int32, (1, tq), 1)
    live = jnp.where(t >= CMP_LEN - 1, 1.0, 0.0)
    blk = lax.broadcasted_iota(jnp.int32, (ns, 1), 0)
    blk_f = blk.astype(F32)
    cur = jnp.right_shift(t, SEL_LEN.bit_length() - 1)
    removed = -3e38
    forced = (blk == 0) | (blk == cur) | (blk == cur - 1)
    keep = jnp.where(forced | (blk > cur), 0.0, 1.0)
    pinned = jnp.where(forced, removed, jnp.where(blk > cur, -FORCE, 0.0))

    def attend(nk):
        n = lax.broadcasted_iota(jnp.int32, (nk, 1), 0)
        hidden = jnp.where((n * CMP_STRIDE + (CMP_LEN - 1)) <= t, 0.0, NEG)
        ovt = ovt_ref[:, :nk]
        o_t = []
        for g in range(B_KV):
            kc = kc_ref[g, :nk, :]
            vct = vct_ref[g, :, :nk]
            psum = jnp.zeros((nk, tq), F32)
            for r in range(rep):
                h = g * rep + r
                qh = q_ref[:, h * HEAD_DIM:(h + 1) * HEAD_DIM]
                s = _dot_nt(kc, qh) + hidden
                p = jnp.exp2(s - _reduce_rows(s, jnp.maximum))
                p = p * (live / _reduce_rows(p, jnp.add))
                o_t.append(_dot(vct, p.astype(BF16)))
                psum = psum + p
            imp = sum(_dot(ovt, piece) for piece in _split_bf16(psum, 3))
            imp_scr[:, g * tq:(g + 1) * tq] = imp * keep + pinned
        o_ref[...] = jnp.transpose(jnp.concatenate(o_t, axis=0)).astype(BF16)

    n_chunks = nc // CMP_KEY_CHUNK
    last_chunk = ((i + 1) * (tq // CMP_STRIDE) - 1) // CMP_KEY_CHUNK
    for c in range(n_chunks):
        pl.when(last_chunk == c)(functools.partial(attend, (c + 1) * CMP_KEY_CHUNK))

    def take_next(imp, allow=None):
        rank = blk_f[:imp.shape[0]]
        mx = jnp.max(imp, axis=0, keepdims=True)
        first = jnp.min(jnp.where(imp == mx, rank, float(ns)), axis=0, keepdims=True)
        hit = rank == first
        if allow is not None:
            hit = hit & allow
        return jnp.where(hit, removed, imp)

    def rank_blocks(nb):
        imp = imp_scr[:nb, :]
        for _ in range(SEL_TOP - 3):
            imp = take_next(imp)
        imp_scr[:nb, :] = imp

    early = (i + 1) * tq <= (ns // 2) * SEL_LEN
    pl.when(early)(functools.partial(rank_blocks, ns // 2))
    pl.when(jnp.logical_not(early))(functools.partial(rank_blocks, ns))

    @pl.when(i * tq < 2 * SEL_LEN)
    def _():
        t2 = jnp.concatenate([t] * B_KV, axis=1)
        more = take_next(imp_scr[...], t2 < 2 * SEL_LEN)
        imp_scr[...] = take_next(more, t2 < SEL_LEN)

    for g in range(B_KV):
        picked = imp_scr[:, g * tq:(g + 1) * tq] < 0.5 * removed
        sel_ref[g] = jnp.where(picked, 0.0, NEG).astype(BF16)


def _cmp_attn(nat_b, kc, vct, overlap_t):
    B, S, _ = nat_b.shape
    nc = kc.shape[2]
    ns = overlap_t.shape[0]
    tq = CMP_TQ
    return pl.pallas_call(
        _cmp_attn_kernel, grid=(B, S // tq),
        in_specs=[
            pl.BlockSpec((None, tq, B_Q), lambda b, i: (b, i, 0)),
            pl.BlockSpec((None, B_KV, nc, HEAD_DIM), lambda b, i: (b, 0, 0, 0)),
            pl.BlockSpec((None, B_KV, HEAD_DIM, nc), lambda b, i: (b, 0, 0, 0)),
            pl.BlockSpec((ns, nc), lambda b, i: (0, 0)),
        ],
        out_specs=(
            pl.BlockSpec((None, tq, B_Q), lambda b, i: (b, i, 0)),
            pl.BlockSpec((None, B_KV, ns, tq), lambda b, i: (b, 0, 0, i)),
        ),
        out_shape=(
            jax.ShapeDtypeStruct((B, S, B_Q), BF16),
            jax.ShapeDtypeStruct((B, B_KV, ns, S), BF16),
        ),
        scratch_shapes=[pltpu.VMEM((ns, B_KV * tq), F32)],
        compiler_params=_cparams(2), name="cmp_attn",
    )(nat_b, kc, vct, overlap_t)


SLC_TK = 256
SLC_TQ = 2 * SLC_TK


SLC_VROWS = 80
SLC_CW = 256


def _slc_attn_kernel(q_ref, selt_ref, kin_ref, hot_ref, vin_ref, o_ref,
                     k_ref, vt_ref, qat_scr, m_scr, acc_scr, s_a, s_b):
    i = pl.program_id(2)
    tq = q_ref.shape[0]
    rep = B_HEADS // B_KV

    @pl.when(i == 0)
    def _():
        k_ref[:, :HEAD_DIM] = kin_ref[...]
        k_ref[:, HEAD_DIM:] = hot_ref[...]
        vt_ref[:HEAD_DIM, :] = vin_ref[...]
        pad_rows = vt_ref.shape[0] - HEAD_DIM
        row = lax.broadcasted_iota(jnp.int32, (pad_rows, vt_ref.shape[1]), 0)
        vt_ref[HEAD_DIM:, :] = jnp.where(row == 0, 1.0, 0.0).astype(BF16)

    selt = selt_ref[...]
    for r in range(rep):
        cols = slice(r * tq, (r + 1) * tq)
        q_r = q_ref[:, r * HEAD_DIM:(r + 1) * HEAD_DIM].astype(F32)
        qat_scr[:HEAD_DIM, cols] = jnp.transpose(q_r).astype(BF16)
        qat_scr[HEAD_DIM:, cols] = selt
    m_scr[...] = jnp.full(m_scr.shape, NEG, F32)
    acc_scr[...] = jnp.zeros(acc_scr.shape, F32)
    rel = (lax.broadcasted_iota(jnp.int32, (SLC_TK, tq), 1)
           - lax.broadcasted_iota(jnp.int32, (SLC_TK, tq), 0))

    def scores(j, s_buf, q_from=0):
        k0 = pl.multiple_of(j * SLC_TK, SLC_TK)
        k_tile = k_ref[pl.ds(k0, SLC_TK), :]
        if q_from == 0:
            s_buf[...] = _dot(k_tile, qat_scr[...])
        else:
            for r in range(rep):
                cols = slice(r * tq + q_from, (r + 1) * tq)
                s_buf[:, cols] = _dot(k_tile, qat_scr[:, cols])

    def update(j, s_buf, diag_offset=None):
        k0 = pl.multiple_of(j * SLC_TK, SLC_TK)
        vt = vt_ref[:, pl.ds(k0, SLC_TK)]
        for c in range(rep * tq // SLC_CW):
            cols = slice(c * SLC_CW, (c + 1) * SLC_CW)
            q_lo = (c * SLC_CW) % tq
            if diag_offset is not None and q_lo + SLC_CW <= diag_offset:
                continue
            s = s_buf[:, cols]
            if diag_offset is not None:
                s = jnp.where(rel[:, q_lo:q_lo + SLC_CW] >= diag_offset, s, NEG)
            m_prev = m_scr[:, cols]
            m_new = jnp.maximum(m_prev, jnp.max(s, axis=0, keepdims=True))
            alpha = jnp.exp2(m_prev - m_new)
            p = jnp.exp2(s - m_new).astype(BF16)
            acc_scr[:, cols] = alpha * acc_scr[:, cols] + _dot(vt, p)
            m_scr[:, cols] = m_new

    def tile_pair(jj, carry):
        a = 2 * jj
        scores(a + 1, s_b)
        update(a, s_a)
        scores(a + 2, s_a)
        update(a + 1, s_b)
        return carry

    def two_pairs(jj, carry):
        tile_pair(2 * jj, carry)
        return tile_pair(2 * jj + 1, carry)

    scores(0, s_a)
    lax.fori_loop(0, i // 2, two_pairs, 0)

    @pl.when(i % 2 == 1)
    def _():
        tile_pair(i - 1, 0)

    scores(2 * i + 1, s_b, q_from=SLC_TK)
    update(2 * i, s_a, diag_offset=0)
    update(2 * i + 1, s_b, diag_offset=SLC_TK)

    for r in range(rep):
        acc = acc_scr[:, r * tq:(r + 1) * tq]
        o_t = acc[:HEAD_DIM] / acc[HEAD_DIM:HEAD_DIM + 1]
        o_ref[:, r * HEAD_DIM:(r + 1) * HEAD_DIM] = jnp.transpose(o_t).astype(BF16)


def _slc_attn(nat_b, selt, k_t, onehot, v_t):
    B, S, _ = nat_b.shape
    G = B_KV
    rep = B_HEADS // B_KV
    ns = selt.shape[2]
    tq = SLC_TQ
    gw = rep * HEAD_DIM
    return pl.pallas_call(
        _slc_attn_kernel, grid=(B, G, S // tq),
        in_specs=[
            pl.BlockSpec((None, tq, gw), lambda b, g, i: (b, i, g)),
            pl.BlockSpec((None, None, ns, tq), lambda b, g, i: (b, g, 0, i)),
            pl.BlockSpec((None, None, S, HEAD_DIM), lambda b, g, i: (b, g, 0, 0)),
            pl.BlockSpec((S, ns), lambda b, g, i: (0, 0), pipeline_mode=pl.Buffered(1)),
            pl.BlockSpec((None, None, HEAD_DIM, S), lambda b, g, i: (b, g, 0, 0)),
        ],
        out_specs=pl.BlockSpec((None, tq, gw), lambda b, g, i: (b, i, g)),
        out_shape=jax.ShapeDtypeStruct((B, S, B_Q), BF16),
        scratch_shapes=[pltpu.VMEM((S, HEAD_DIM + ns), BF16), pltpu.VMEM((SLC_VROWS, S), BF16),
                        pltpu.VMEM((HEAD_DIM + ns, rep * tq), BF16),
                        pltpu.VMEM((1, rep * tq), F32), pltpu.VMEM((SLC_VROWS, rep * tq), F32),
                        pltpu.VMEM((SLC_TK, rep * tq), F32), pltpu.VMEM((SLC_TK, rep * tq), F32)],
        compiler_params=_cparams(3), name="slc_attn",
    )(nat_b, selt, k_t, onehot, v_t)


SUB = 128
BAND_TQ = 1024


def _band_kernel(q_ref, kp_ref, kc_ref, vp_ref, vc_ref, o_ref, *lse_refs, shared_kv, prev, max_off):
    i = pl.program_id(1)
    tq = q_ref.shape[0]
    n_slab = q_ref.shape[1] // LANES
    width = prev + SUB
    r_i = lax.broadcasted_iota(jnp.int32, (SUB, width), 0)
    c_i = lax.broadcasted_iota(jnp.int32, (SUB, width), 1)
    diff = r_i - c_i + prev
    in_band = (diff >= 0) & (diff <= max_off)
    lane = lax.broadcasted_iota(jnp.int32, (1, LANES), 1)
    lo_half = lane < HEAD_DIM
    half_mask = {"lo": lo_half, "hi": jnp.logical_not(lo_half)}

    def window(p_ref, c_ref, sub, slab):
        cols = slice(slab * LANES, (slab + 1) * LANES)
        lo = sub * SUB
        parts = []
        if lo < prev:
            parts.append(p_ref[lo:prev, cols])
        parts.append(c_ref[max(0, lo - prev):lo + SUB, cols])
        return jnp.concatenate(parts, axis=0) if len(parts) > 1 else parts[0]

    def swap_halves(x):
        return jnp.concatenate([x[:, HEAD_DIM:], x[:, :HEAD_DIM]], axis=1)

    for sub in range(tq // SUB):
        rows = slice(sub * SUB, (sub + 1) * SUB)
        k_start = i * tq + sub * SUB - prev
        bias = jnp.where(in_band & (c_i + k_start >= 0), 0.0, NEG)
        if shared_kv:
            k2 = window(kp_ref, kc_ref, sub, 0)
            v2 = window(vp_ref, vc_ref, sub, 0)
            groups = [
                (((0, "lo"), (1, "lo"), (2, "hi"), (3, "hi")), k2, v2),
                (((0, "hi"), (1, "hi"), (2, "lo"), (3, "lo")), swap_halves(k2), swap_halves(v2)),
            ]
        else:
            groups = [(((p, "lo"), (p, "hi")), window(kp_ref, kc_ref, sub, p),
                       window(vp_ref, vc_ref, sub, p)) for p in range(n_slab)]
        outs = {}
        for members, k2, v2 in groups:
            nm = len(members)
            zero = jnp.zeros((), BF16)
            lhs = jnp.concatenate(
                [jnp.where(half_mask[half], q_ref[rows, slab * LANES:(slab + 1) * LANES], zero)
                 for slab, half in members], axis=0)
            s = _dot_nt(lhs, k2)
            s = (s.reshape(nm, SUB, width) + bias[None]).reshape(nm * SUB, width)
            m = jnp.max(s, axis=-1, keepdims=True)
            p = jnp.exp2(s - m)
            den = jnp.sum(p, axis=-1, keepdims=True)
            pv = _dot(p.astype(BF16), v2) / den
            lse = m + jnp.log2(den)
            for idx, member in enumerate(members):
                outs[member] = (pv[idx * SUB:(idx + 1) * SUB], lse[idx * SUB:(idx + 1) * SUB])
        lse_tile = jnp.zeros((SUB, LANES), F32)
        for slab in range(n_slab):
            cols = slice(slab * LANES, (slab + 1) * LANES)
            (o_lo, l_lo), (o_hi, l_hi) = outs[(slab, "lo")], outs[(slab, "hi")]
            o_ref[rows, cols] = jnp.where(lo_half, o_lo, o_hi).astype(o_ref.dtype)
            lse_tile = jnp.where(lane == 2 * slab, l_lo,
                                 jnp.where(lane == 2 * slab + 1, l_hi, lse_tile))
        if lse_refs:
            lse_refs[0][rows, :] = lse_tile


def _band_attn(q_arr, k_arr, v_arr, *, q_blk, k_blk, v_blk, hq, hk, max_off, tq, with_lse):
    N, L, _ = q_arr.shape
    assert hq == hk or (hq, hk) == (8, 2), (hq, hk)
    prev = -(-max_off // SUB) * SUB
    m = tq // prev
    qw, kw = hq * HEAD_DIM, hk * HEAD_DIM
    cur = lambda blk: (lambda n, i: (n, i, blk))
    prv = lambda blk: (lambda n, i: (n, jnp.maximum(i * m - 1, 0), blk))
    out_shape = [jax.ShapeDtypeStruct((N, L, qw), BF16)]
    out_specs = [pl.BlockSpec((None, tq, qw), cur(0))]
    if with_lse:
        out_shape.append(jax.ShapeDtypeStruct((N, L, LANES), F32))
        out_specs.append(pl.BlockSpec((None, tq, LANES), cur(0)))
    return pl.pallas_call(
        functools.partial(_band_kernel, shared_kv=hq != hk, prev=prev, max_off=max_off),
        grid=(N, L // tq),
        in_specs=[
            pl.BlockSpec((None, tq, qw), cur(q_blk)),
            pl.BlockSpec((None, prev, kw), prv(k_blk)),
            pl.BlockSpec((None, tq, kw), cur(k_blk)),
            pl.BlockSpec((None, prev, kw), prv(v_blk)),
            pl.BlockSpec((None, tq, kw), cur(v_blk)),
        ],
        out_specs=tuple(out_specs), out_shape=tuple(out_shape),
        compiler_params=_cparams(2), name="band_attn",
    )(q_arr, k_arr, k_arr, v_arr, v_arr)


FF_CHUNK = 512

def _merge_ffn_kernel(x_ref, o0_ref, l0_ref, o1_ref, l1_ref, o2_ref, l2_ref, oc_ref, os_ref, ow_ref,
                      gate_ref, wa_ref, wb_ref, wo_ref, gm_ref, wu_ref, wd_ref, gf_ref,
                      out_ref, s_o1, s_l1, s_o2, s_l2, x1_scr, *, final_norm):
    step = pl.program_id(0)
    T = x_ref.shape[0]

    @pl.when(step == 0)
    def _():
        x1_scr[1] = jnp.zeros(x1_scr.shape[1:], F32)

    x1_prev = x1_scr[(step + 1) % 2]
    u = _rms(x1_prev, gm_ref[...]).astype(BF16)
    acc = x1_prev
    for c in range(D_FF // FF_CHUNK):
        cols = slice(c * FF_CHUNK, (c + 1) * FF_CHUNK)
        h = jnp.maximum(_dot(u, wu_ref[:, cols]), 0.0)
        acc = acc + _dot((h * h).astype(BF16), wd_ref[cols, :])
    out_ref[...] = _rms(acc, gf_ref[...]) if final_norm else acc

    for d, o_ref, l_ref, s_o, s_l in ((4, o1_ref, l1_ref, s_o1, s_l1), (16, o2_ref, l2_ref, s_o2, s_l2)):
        rows = T // d
        for r in range(d):
            o_r = o_ref[r].astype(F32)
            s_l[pl.ds(r, rows, stride=d), :] = l_ref[r]
            for k in range(s_o.shape[0]):
                s_o[k, pl.ds(r, rows, stride=d), :] = o_r[:, k * LANES:(k + 1) * LANES]
    gather = lambda s: jnp.concatenate([s[k] for k in range(s.shape[0])], axis=1)

    l0, l1, l2 = l0_ref[...], s_l1[...], s_l2[...]
    mx = jnp.maximum(jnp.maximum(l0, l1), l2)
    e0, e1, e2 = jnp.exp2(l0 - mx), jnp.exp2(l1 - mx), jnp.exp2(l2 - mx)
    inv = 1.0 / (e0 + e1 + e2)
    lo_half = lax.broadcasted_iota(jnp.int32, (1, LANES), 1) < HEAD_DIM

    def spread(w, lane_of_head):
        col = lambda h: jnp.broadcast_to(w[:, lane_of_head(h):lane_of_head(h) + 1], (T, LANES))
        return jnp.concatenate([jnp.where(lo_half, col(2 * p), col(2 * p + 1))
                                for p in range(A_SLOTS // 2)], axis=1)

    slot = lambda h: h
    y_a = (spread(e0 * inv, slot) * o0_ref[...].astype(F32) + spread(e1 * inv, slot) * gather(s_o1)
           + spread(e2 * inv, slot) * gather(s_o2))

    gb = gate_ref[:, MERGE_GATE:MERGE_GATE + LANES].astype(F32)
    y_b = (spread(gb, lambda h: 3 * h) * oc_ref[...].astype(F32)
           + spread(gb, lambda h: 3 * h + 1) * os_ref[...].astype(F32)
           + spread(gb, lambda h: 3 * h + 2) * ow_ref[...].astype(F32))
    merged = (gate_ref[:, :D_MODEL].astype(F32) * _dot(y_a.astype(BF16), wa_ref[...])
              + gate_ref[:, D_MODEL:MERGE_GATE].astype(F32) * _dot(y_b.astype(BF16), wb_ref[...]))
    x1_scr[step % 2] = x_ref[...] + _dot(merged.astype(BF16), wo_ref[...])


def _merge_ffn(x, o0, l0, o1, l1, o2, l2, o_cmp, o_slc, o_win, gates, wa, wb, wo,
               g_mlp, w_up, w_down, g_final, final_norm):
    B, S, D = x.shape
    T = ROW_TILE
    nt = S // T
    n_tiles = B * nt
    qw = A_SLOTS * HEAD_DIM

    def tile_map(shift):
        def index_map(k):
            m = jnp.clip(k - shift, 0, n_tiles - 1)
            return (m // nt, m % nt, 0)
        return index_map

    nat = lambda w, shift=0: pl.BlockSpec((None, T, w), tile_map(shift))
    dil = lambda d, w: pl.BlockSpec((d, T // d, w), tile_map(0))
    full = lambda a: pl.BlockSpec(a.shape, lambda k: (0,) * a.ndim, pipeline_mode=pl.Buffered(1))
    o_scr, l_scr = pltpu.VMEM((qw // LANES, T, LANES), F32), pltpu.VMEM((T, LANES), F32)
    return pl.pallas_call(
        functools.partial(_merge_ffn_kernel, final_norm=final_norm), grid=(n_tiles + 1,),
        in_specs=[nat(D), nat(qw), nat(LANES), dil(4, qw), dil(4, LANES), dil(16, qw),
                  dil(16, LANES), nat(B_Q), nat(B_Q), nat(B_Q), nat(G_COLS), full(wa),
                  full(wb), full(wo), full(g_mlp), full(w_up), full(w_down), full(g_final)],
        out_specs=nat(D, shift=1), out_shape=jax.ShapeDtypeStruct((B, S, D), F32),
        scratch_shapes=[o_scr, l_scr, o_scr, l_scr, pltpu.VMEM((2, T, D), F32)],
        compiler_params=_cparams(1), name="merge_ffn",
    )(x, o0, l0, o1, l1, o2, l2, o_cmp, o_slc, o_win, gates, wa, wb, wo,
      g_mlp, w_up, w_down, g_final)


def _rope_table(seq):
    half = ROT_DIM // 2
    pos = jnp.arange(seq, dtype=F32)[:, None]
    inv = ROPE_THETA ** (-jnp.arange(0, ROT_DIM, 2, dtype=F32) / ROT_DIM)
    ang = pos * inv[None, :]
    cos, sin = jnp.cos(ang), jnp.sin(ang)
    pad = lambda parts, fill: jnp.tile(jnp.concatenate(
        parts + [jnp.full((seq, HEAD_DIM - ROT_DIM), fill, F32)], axis=1), (1, LANES // HEAD_DIM))
    zero = jnp.zeros_like(sin)
    return jnp.stack([pad([cos, cos], 1.0), pad([-sin, zero], 0.0), pad([zero, sin], 0.0)])


def _layer_weights(w_in):
    o3 = A_QKV + B_COLS
    o4 = o3 + B_GATE
    w_main = w_in.astype(BF16)
    w_gate = jnp.concatenate(
        [w_main[:, o4:], jnp.pad(w_main[:, o3:o4], ((0, 0), (0, G_PAD - B_GATE)))], axis=1)
    return w_main, w_gate


def _overlap_matrix(nc_pad, ns):
    c_start = jnp.arange(nc_pad) * CMP_STRIDE
    s_start = jnp.arange(ns) * SEL_LEN
    ov = (c_start[:, None] < s_start[None, :] + SEL_LEN) & (c_start[:, None] + CMP_LEN > s_start[None, :])
    return ov.astype(BF16)


def _layer(x, g_mix, w_in, cmp_pos_k, cmp_w1_k, cmp_w2_k, cmp_pos_v, cmp_w1_v, cmp_w2_v,
           w_branch_a, w_branch_b, w_out, g_mlp, w_up, w_down, g_final, final_norm, tab):
    B, S, D = x.shape
    G = B_KV
    a0, a1, a2, nat_b, cmp_in, gates = _in_proj(x, g_mix.reshape(1, D), *_layer_weights(w_in), tab)

    a_out = []
    for arr, (w, d) in zip((a0, a1, a2), DIL_PAIRS):
        a_out.append(_band_attn(arr, arr, arr, q_blk=0, k_blk=1, v_blk=2, hq=A_SLOTS, hk=A_SLOTS,
                                max_off=w // d, tq=min(BAND_TQ, S // d), with_lse=True))
    (o0, l0), (o1, l1), (o2, l2) = a_out

    nch = S // CMP_STRIDE
    kv_w = G * HEAD_DIM
    k_slc = nat_b[:, :, B_Q + 2 * kv_w:B_Q + 3 * kv_w].reshape(B, S, G, HEAD_DIM)
    v_slc = nat_b[:, :, B_Q + 3 * kv_w:B_Q + 4 * kv_w].reshape(B, S, G, HEAD_DIM)
    pos = jnp.stack([cmp_pos_k, cmp_pos_v]).astype(BF16)[:, :, None, :]
    pos = jnp.broadcast_to(pos, (2, CMP_LEN, G, HEAD_DIM)).reshape(2, 1, CMP_LEN * G * HEAD_DIM)
    pos = jnp.broadcast_to(pos, (2, 8, CMP_LEN * G * HEAD_DIM))
    w1 = jnp.stack([cmp_w1_k, cmp_w1_v]).astype(BF16).reshape(2, CMP_LEN, 1, HEAD_DIM, 1, CMP_HIDDEN)
    same_group = jnp.eye(G, dtype=BF16).reshape(1, 1, G, 1, G, 1)
    w1 = (w1 * same_group).reshape(2, CMP_LEN * G * HEAD_DIM, G * CMP_HIDDEN)
    kcvc = _compress(cmp_in, pos, w1, jnp.stack([cmp_w2_k, cmp_w2_v]).astype(BF16))
    ns = S // SEL_LEN
    o_cmp, selb = _cmp_attn(nat_b, kcvc[0], kcvc[1].transpose(0, 1, 3, 2),
                            _overlap_matrix(nch, ns).T)

    onehot = ((jnp.arange(S) // SEL_LEN)[:, None] == jnp.arange(ns)[None, :]).astype(BF16)
    o_slc = _slc_attn(nat_b, selb, k_slc.transpose(0, 2, 1, 3), onehot, v_slc.transpose(0, 2, 3, 1))

    kw_blk = (B_Q + 4 * G * HEAD_DIM) // (G * HEAD_DIM)
    (o_win,) = _band_attn(nat_b, nat_b, nat_b, q_blk=0, k_blk=kw_blk, v_blk=kw_blk + 1,
                          hq=B_HEADS, hk=B_KV, max_off=WIN_LEN - 1, tq=min(BAND_TQ, S),
                          with_lse=False)

    return _merge_ffn(x, o0, l0, o1, l1, o2, l2, o_cmp, o_slc, o_win, gates,
                      w_branch_a.astype(BF16), w_branch_b.astype(BF16), w_out.astype(BF16),
                      g_mlp.reshape(1, D), w_up.astype(BF16), w_down.astype(BF16),
                      g_final.reshape(1, D), final_norm)


def kernel(x, norm_mix_g, w_in, cmp_pos_k, cmp_w1_k, cmp_w2_k, cmp_pos_v, cmp_w1_v, cmp_w2_v,
           w_branch_a, w_branch_b, w_out, norm_mlp_g, w_up, w_down, norm_final_g):
    B, S, D = x.shape
    depth = w_in.shape[0]
    tab = _rope_table(S)
    for l in range(depth):
        x = _layer(x, norm_mix_g[l], w_in[l], cmp_pos_k[l], cmp_w1_k[l], cmp_w2_k[l],
                   cmp_pos_v[l], cmp_w1_v[l], cmp_w2_v[l], w_branch_a[l], w_branch_b[l],
                   w_out[l], norm_mlp_g[l], w_up[l], w_down[l], norm_final_g, l == depth - 1, tab)
    return x
```
